```python
import math
import jax, jax.numpy as jnp
from jax import lax
import numpy as np

D_MODEL = 1024
BATCH = 16
SEQ = 2048
DEPTH = 1

CHUNK = 64
Q_BLOCK = 128
PLE_DIM = 256
D_FF = 4 * D_MODEL
EPS = 1e-6

GLA_HEADS = 4
GLA_DK = D_MODEL // 16
GLA_DV = D_MODEL // 8
GLA_GATE_RANK = 16
GLA_TAU = 16.0
GLA_WIDTH = GLA_HEADS * GLA_DV

MLA_HEADS = 8
MLA_NOPE = 64
MLA_ROPE = 32
MLA_V = 64
MLA_QK = MLA_NOPE + MLA_ROPE
MLA_Q_RANK = 256
MLA_KV_RANK = 128
MLA_WIDTH = MLA_HEADS * MLA_V
ROPE_THETA = 10000.0

D_MIX = GLA_WIDTH + MLA_WIDTH
IN_SPLITS = (GLA_HEADS * GLA_DK, GLA_HEADS * GLA_DK, GLA_WIDTH, GLA_GATE_RANK,
             GLA_WIDTH, MLA_Q_RANK, MLA_KV_RANK, MLA_ROPE)
D_IN = (GLA_HEADS * GLA_DK * 2 + GLA_WIDTH * 2 + GLA_GATE_RANK
        + MLA_Q_RANK + MLA_KV_RANK + MLA_ROPE)

kernel_name = "hymba_gla_mla_hybrid_block"


def rms_norm(x, g):
    xf = x.astype(jnp.float32)
    y = xf * lax.rsqrt(jnp.mean(xf * xf, axis=-1, keepdims=True) + EPS)
    return (y * g.astype(jnp.float32)).astype(x.dtype)


def split_cols(t, sizes):
    out, start = [], 0
    for s in sizes:
        out.append(t[..., start:start + s])
        start += s
    return out


def rope_cos_sin(positions):
    inv_freq = ROPE_THETA ** (-jnp.arange(0, MLA_ROPE, 2, dtype=jnp.float32) / MLA_ROPE)
    ang = positions.astype(jnp.float32)[..., None] * inv_freq
    return jnp.cos(ang)[:, :, None, :], jnp.sin(ang)[:, :, None, :]


def apply_rope(x, cos, sin):
    half = x.shape[-1] // 2
    x1, x2 = x[..., :half], x[..., half:]
    c, s = cos.astype(x.dtype), sin.astype(x.dtype)
    return jnp.concatenate([x1 * c - x2 * s, x2 * c + x1 * s], axis=-1)


def gla_chunked(q, k, v, log_a):
    out_dtype = v.dtype
    B, S, H, DK = q.shape
    DV = v.shape[-1]
    NC = S // CHUNK

    def to_chunks(t):
        return t.astype(jnp.float32).reshape(B, NC, CHUNK, H, -1).transpose(1, 0, 3, 2, 4)

    qc, kc, vc, gc = to_chunks(q), to_chunks(k), to_chunks(v), to_chunks(log_a)
    causal = jnp.tril(jnp.ones((CHUNK, CHUNK), dtype=bool))[:, :, None]

    def step(state, inp):
        qi, ki, vi, gi = inp
        b = jnp.cumsum(gi, axis=2)
        b_last = b[:, :, -1:, :]
        o_inter = jnp.einsum('bhtk,bhkv->bhtv', qi * jnp.exp(b), state)
        diff = b[:, :, :, None, :] - b[:, :, None, :, :]
        decay = jnp.exp(jnp.where(causal, diff, -jnp.inf))
        scores = jnp.einsum('bhtk,bhsk,bhtsk->bhts', qi, ki, decay)
        o_intra = jnp.einsum('bhts,bhsv->bhtv', scores, vi)
        k_dec = ki * jnp.exp(b_last - b)
        state = state * jnp.exp(b_last[:, :, 0, :, None]) + jnp.einsum('bhsk,bhsv->bhkv', k_dec, vi)
        return state, o_inter + o_intra

    state0 = jnp.zeros((B, H, DK, DV), dtype=jnp.float32)
    _, oc = lax.scan(step, state0, (qc, kc, vc, gc))
    return oc.transpose(1, 0, 3, 2, 4).reshape(B, S, H, DV).astype(out_dtype)


def mla_attention(q, k, v):
    S = q.shape[1]
    scale = 1.0 / math.sqrt(MLA_QK)
    outs = []
    for blk in range(S // Q_BLOCK):
        qs, qe = blk * Q_BLOCK, (blk + 1) * Q_BLOCK
        qb, kb, vb = q[:, qs:qe], k[:, :qe], v[:, :qe]
        logits = jnp.einsum('bqhd,bkhd->bhqk', qb, kb).astype(jnp.float32) * scale
        q_chunk = (jnp.arange(qs, qe) // CHUNK)[:, None]
        k_chunk = (jnp.arange(qe) // CHUNK)[None, :]
        logits = jnp.where(k_chunk <= q_chunk, logits, -jnp.inf)
        probs = jax.nn.softmax(logits, axis=-1).astype(vb.dtype)
        outs.append(jnp.einsum('bhqk,bkhd->bqhd', probs, vb))
    return jnp.concatenate(outs, axis=1)


def setup_inputs(seed: int = 0) -> dict:
    key = jax.random.key(seed)
    ks = jax.random.split(key, 24)
    f32 = jnp.float32

    def w(k, shape, fan_in):
        return jax.random.normal(k, shape, f32) * (fan_in ** -0.5)

    def gain(k, shape):
        return 1.0 + 0.1 * jax.random.normal(k, shape, f32)

    x = jax.random.normal(ks[0], (BATCH, SEQ, D_MODEL), f32)
    p = jax.random.normal(ks[1], (DEPTH, BATCH, SEQ, PLE_DIM), f32)
    offset = jax.random.randint(ks[2], (BATCH,), 0, 64, dtype=jnp.int32) * CHUNK
    positions = (offset[:, None] + jnp.arange(SEQ, dtype=jnp.int32)[None, :]).astype(jnp.int32)
    return {
        "x": x,
        "p": p,
        "positions": positions,
        "attn_norm": gain(ks[3], (DEPTH, D_MODEL)),
        "w_in": w(ks[4], (DEPTH, D_MODEL, D_IN), D_MODEL),
        "gla_gate_w2": w(ks[5], (DEPTH, GLA_GATE_RANK, GLA_HEADS * GLA_DK), GLA_GATE_RANK),
        "gla_gate_b": 0.1 * jax.random.normal(ks[6], (DEPTH, GLA_HEADS * GLA_DK), f32),
        "gla_out_norm": gain(ks[7], (DEPTH, GLA_DV)),
        "mla_q_norm": gain(ks[8], (DEPTH, MLA_Q_RANK)),
        "mla_w_uq": w(ks[9], (DEPTH, MLA_Q_RANK, MLA_HEADS * MLA_QK), MLA_Q_RANK),
        "mla_kv_norm": gain(ks[10], (DEPTH, MLA_KV_RANK)),
        "mla_w_ukv": w(ks[11], (DEPTH, MLA_KV_RANK, MLA_HEADS * (MLA_NOPE + MLA_V)), MLA_KV_RANK),
        "qk_norm_q": gain(ks[12], (DEPTH, MLA_QK)),
        "qk_norm_k": gain(ks[13], (DEPTH, MLA_QK)),
        "w_out": w(ks[14], (DEPTH, D_MIX, D_MODEL), D_MIX),
        "mlp_norm": gain(ks[15], (DEPTH, D_MODEL)),
        "w_mlp_up": w(ks[16], (DEPTH, D_MODEL, D_FF), D_MODEL),
        "w_mlp_down": w(ks[17], (DEPTH, D_FF, D_MODEL), D_FF),
        "ple_norm": gain(ks[18], (DEPTH, D_MODEL)),
        "w_ple_gate": w(ks[19], (DEPTH, D_MODEL, D_MODEL), D_MODEL),
        "b_ple_gate": 0.1 * jax.random.normal(ks[20], (DEPTH, D_MODEL), f32),
        "w_ple_proj": w(ks[21], (DEPTH, PLE_DIM, D_MODEL), PLE_DIM),
    }


def reference(x, p, positions, attn_norm, w_in, gla_gate_w2, gla_gate_b, gla_out_norm,
              mla_q_norm, mla_w_uq, mla_kv_norm, mla_w_ukv, qk_norm_q, qk_norm_k,
              w_out, mlp_norm, w_mlp_up, w_mlp_down, ple_norm, w_ple_gate, b_ple_gate,
              w_ple_proj):
    B, S, _ = x.shape
    cos, sin = rope_cos_sin(positions)
    h = x
    for i in range(DEPTH):
        n = rms_norm(h, attn_norm[i])
        z = n @ w_in[i]
        gq, gk, gv, g_low, g_r, c_q, c_kv, k_r = split_cols(z, IN_SPLITS)

        q_g = gq.reshape(B, S, GLA_HEADS, GLA_DK) * (GLA_DK ** -0.5)
        k_g = gk.reshape(B, S, GLA_HEADS, GLA_DK)
        v_g = gv.reshape(B, S, GLA_HEADS, GLA_DV)
        log_a = jax.nn.log_sigmoid((g_low @ gla_gate_w2[i] + gla_gate_b[i]).astype(jnp.float32)) / GLA_TAU
        log_a = log_a.reshape(B, S, GLA_HEADS, GLA_DK)
        o_g = gla_chunked(q_g, k_g, v_g, log_a)
        o_g = rms_norm(o_g, gla_out_norm[i]).reshape(B, S, GLA_WIDTH) * jax.nn.silu(g_r)

        q_m = (rms_norm(c_q, mla_q_norm[i]) @ mla_w_uq[i]).reshape(B, S, MLA_HEADS, MLA_QK)
        kv = (rms_norm(c_kv, mla_kv_norm[i]) @ mla_w_ukv[i]).reshape(B, S, MLA_HEADS, MLA_NOPE + MLA_V)
        k_nope, v_m = kv[..., :MLA_NOPE], kv[..., MLA_NOPE:]
        k_rope = jnp.broadcast_to(k_r[:, :, None, :], (B, S, MLA_HEADS, MLA_ROPE))
        k_m = jnp.concatenate([k_nope, k_rope], axis=-1)
        q_m = rms_norm(q_m, qk_norm_q[i])
        k_m = rms_norm(k_m, qk_norm_k[i])
        q_m = jnp.concatenate([q_m[..., :MLA_NOPE], apply_rope(q_m[..., MLA_NOPE:], cos, sin)], axis=-1)
        k_m = jnp.concatenate([k_m[..., :MLA_NOPE], apply_rope(k_m[..., MLA_NOPE:], cos, sin)], axis=-1)
        o_m = mla_attention(q_m, k_m, v_m).reshape(B, S, MLA_WIDTH)

        h = h + jnp.concatenate([o_g, o_m], axis=-1) @ w_out[i]

        m = rms_norm(h, mlp_norm[i])
        h = h + jnp.square(jax.nn.relu(m @ w_mlp_up[i])) @ w_mlp_down[i]

        gate = jax.nn.sigmoid(rms_norm(h, ple_norm[i]) @ w_ple_gate[i] + b_ple_gate[i])
        h = h + (p[i] @ w_ple_proj[i]) * gate
    return h
```

```python
import functools
import math

import jax
import jax.numpy as jnp
from jax import lax
from jax.experimental import pallas as pl
from jax.experimental.pallas import tpu as pltpu

F32 = jnp.float32
BF16 = jnp.bfloat16

D_MODEL = 1024
CHUNK = 64
PLE_DIM = 256
D_FF = 4 * D_MODEL
EPS = 1e-6
GLA_HEADS = 4
GLA_DK = 64
GLA_DV = 128
GLA_GATE_RANK = 16
GLA_TAU = 16.0
GLA_QK = GLA_HEADS * GLA_DK
GLA_WIDTH = GLA_HEADS * GLA_DV
MLA_HEADS = 8
MLA_NOPE = 64
MLA_ROPE = 32
MLA_V = 64
MLA_QK = MLA_NOPE + MLA_ROPE
MLA_Q_RANK = 256
MLA_KV_RANK = 128
MLA_WIDTH = MLA_HEADS * MLA_V
ROPE_THETA = 10000.0
IN_SPLITS = (GLA_QK, GLA_QK, GLA_WIDTH, GLA_GATE_RANK, GLA_WIDTH, MLA_Q_RANK, MLA_KV_RANK, MLA_ROPE)

LANES = 128
HALF_ROPE = MLA_ROPE // 2
HEAD_PAD = MLA_HEADS * LANES
D_IN_PAD = 2048
VMEM_LIMIT_BYTES = 56 * 1024 * 1024

TM_PROJ = 512
TM_TAIL = 512
FF_CHUNK = 1024
TQ = 256
GLA_FAST_MIN_LOG_DECAY = -80.0

ROW_INVFREQ, ROW_SIGN, ROW_MASK96, ROW_KRMASK = 0, 1, 2, 3


def _const_spec(shape):
    return pl.BlockSpec(shape, lambda *_: (0,) * len(shape), pipeline_mode=pl.Buffered(1))


def _rms_scale(v, n):
    return lax.rsqrt(jnp.sum(v * v, axis=-1, keepdims=True) * (1.0 / n) + EPS)


def _proj_kernel(x_ref, pos_ref, g_attn_ref, w_in_ref, w2_ref, b2_ref, gqn_ref, w_uq_ref,
                 gkvn_ref, w_uk_ref, w_uv_ref, gq_l_ref, gk_l_ref, tab_ref,
                 qg_ref, kg_ref, vg_ref, la_ref, gr_ref, q_ref, k_ref, v_ref):
    x = x_ref[...]
    xn = (x * _rms_scale(x, D_MODEL) * g_attn_ref[...]).astype(BF16)

    def proj(lo, hi):
        return jnp.dot(xn, w_in_ref[:, lo:hi], preferred_element_type=F32)

    zqk = proj(0, 2 * GLA_QK)
    qg_ref[...] = (zqk[:, :GLA_QK] * (GLA_DK ** -0.5)).astype(BF16)
    kg_ref[...] = zqk[:, GLA_QK:].astype(BF16)
    vg_ref[...] = proj(512, 1024).astype(BF16)
    gr = proj(1024, 1536)
    gr_ref[...] = (gr * jax.nn.sigmoid(gr)).astype(BF16)
    zc = proj(1536, 2048)
    cq = zc[:, :MLA_Q_RANK]
    ckv = zc[:, MLA_Q_RANK:MLA_Q_RANK + MLA_KV_RANK]
    slot = zc[:, MLA_Q_RANK + MLA_KV_RANK:]

    xg = jnp.dot(slot.astype(BF16), w2_ref[...], preferred_element_type=F32) + b2_ref[...]
    la_ref[...] = jax.nn.log_sigmoid(xg) * (1.0 / GLA_TAU)

    inv_freq = tab_ref[ROW_INVFREQ:ROW_INVFREQ + 1, :]
    sign = tab_ref[ROW_SIGN:ROW_SIGN + 1, :]
    mask96 = tab_ref[ROW_MASK96:ROW_MASK96 + 1, :]
    krmask = tab_ref[ROW_KRMASK:ROW_KRMASK + 1, :]
    ang = pos_ref[...].astype(F32) * inv_freq
    cos_k = jnp.cos(ang) * mask96
    sin_k = jnp.sin(ang) * sign
    scale = 1.0 / math.sqrt(MLA_QK)
    cos_q = cos_k * scale
    sin_q = sin_k * scale

    cqn = (cq * _rms_scale(cq, MLA_Q_RANK) * gqn_ref[...]).astype(BF16)
    qall = jnp.dot(cqn, w_uq_ref[...], preferred_element_type=F32)
    gq_l = gq_l_ref[...]
    for h in range(MLA_HEADS):
        sl = slice(h * LANES, (h + 1) * LANES)
        qh = qall[:, sl]
        r = lax.rsqrt(jnp.sum(qh * qh * mask96, axis=-1, keepdims=True) * (1.0 / MLA_QK) + EPS)
        qn = qh * r * gq_l
        q_ref[:, sl] = (qn * cos_q + pltpu.roll(qn, LANES // 2, 1) * sin_q).astype(BF16)

    ckvn = (ckv * _rms_scale(ckv, MLA_KV_RANK) * gkvn_ref[...]).astype(BF16)
    kall = jnp.dot(ckvn, w_uk_ref[...], preferred_element_type=F32)
    kr = slot * krmask
    gk_l = gk_l_ref[...]
    for h in range(MLA_HEADS):
        sl = slice(h * LANES, (h + 1) * LANES)
        kh = kall[:, sl] + kr
        r = lax.rsqrt(jnp.sum(kh * kh * mask96, axis=-1, keepdims=True) * (1.0 / MLA_QK) + EPS)
        kn = kh * r * gk_l
        k_ref[:, sl] = (kn * cos_k + pltpu.roll(kn, LANES // 2, 1) * sin_k).astype(BF16)
    v_ref[...] = jnp.dot(ckvn, w_uv_ref[...], preferred_element_type=F32).astype(BF16)


def _head_layout(nope, rope):
    x1, x2 = rope[..., :HALF_ROPE], rope[..., HALF_ROPE:]
    return jnp.concatenate([nope[..., :32], x1, x2, nope[..., 32:], x2, x1], axis=-1)


def _projection(x2, pos2, attn_norm, w_in, gate_w2, gate_b, q_norm, w_uq, kv_norm, w_ukv,
                qk_norm_q, qk_norm_k):
    t = x2.shape[0]
    tm = TM_PROJ
    assert t % tm == 0

    gq, gk, gv, g_low, g_r, c_q, c_kv, k_r = jnp.split(w_in, list(_cumsum(IN_SPLITS))[:-1], axis=1)
    zeros = lambda n: jnp.zeros((D_MODEL, n), w_in.dtype)
    kr_sw = jnp.concatenate([k_r[:, HALF_ROPE:], k_r[:, :HALF_ROPE]], axis=1)
    slot = jnp.concatenate([g_low, zeros(32 - GLA_GATE_RANK), k_r, zeros(32), kr_sw], axis=1)
    w_in_l = jnp.concatenate([gq, gk, gv, g_r, c_q, c_kv, slot], axis=1).astype(BF16)
    assert w_in_l.shape == (D_MODEL, D_IN_PAD)

    w2_l = jnp.zeros((LANES, GLA_QK), F32).at[:GLA_GATE_RANK].set(gate_w2).astype(BF16)

    w_uq_h = w_uq.reshape(MLA_Q_RANK, MLA_HEADS, MLA_QK)
    w_uq_l = _head_layout(w_uq_h[..., :MLA_NOPE], w_uq_h[..., MLA_NOPE:]).reshape(MLA_Q_RANK, HEAD_PAD)
    w_ukv_h = w_ukv.reshape(MLA_KV_RANK, MLA_HEADS, MLA_NOPE + MLA_V)
    w_uk_l = _head_layout(w_ukv_h[..., :MLA_NOPE],
                          jnp.zeros((MLA_KV_RANK, MLA_HEADS, MLA_ROPE), F32)).reshape(MLA_KV_RANK, HEAD_PAD)
    w_uv = w_ukv_h[..., MLA_NOPE:].reshape(MLA_KV_RANK, MLA_WIDTH)
    gq_l = _head_layout(qk_norm_q[:MLA_NOPE], qk_norm_q[MLA_NOPE:]).reshape(1, LANES)
    gk_l = _head_layout(qk_norm_k[:MLA_NOPE], qk_norm_k[MLA_NOPE:]).reshape(1, LANES)

    inv_freq = ROPE_THETA ** (-jnp.arange(0, MLA_ROPE, 2, dtype=F32) / MLA_ROPE)
    z32 = jnp.zeros((32,), F32)
    o16 = jnp.ones((HALF_ROPE,), F32)
    tab = jnp.zeros((8, LANES), F32)
    tab = tab.at[ROW_INVFREQ].set(jnp.concatenate([z32, inv_freq, inv_freq, z32, z32]))
    tab = tab.at[ROW_SIGN].set(jnp.concatenate([z32, -o16, o16, z32, z32]))
    tab = tab.at[ROW_MASK96].set(jnp.concatenate([jnp.ones((96,), F32), z32]))
    tab = tab.at[ROW_KRMASK].set(jnp.concatenate([z32, jnp.ones((32,), F32), z32, jnp.ones((32,), F32)]))

    row = lambda w: pl.BlockSpec((tm, w), lambda i: (i, 0))
    out_widths = (GLA_QK, GLA_QK, GLA_WIDTH, GLA_QK, GLA_WIDTH, HEAD_PAD, HEAD_PAD, MLA_WIDTH)
    out_dtypes = (BF16, BF16, BF16, F32, BF16, BF16, BF16, BF16)
    return pl.pallas_call(
        _proj_kernel,
        grid=(t // tm,),
        in_specs=[
            row(D_MODEL), row(1),
            _const_spec((1, D_MODEL)), _const_spec((D_MODEL, D_IN_PAD)),
            _const_spec((LANES, GLA_QK)), _const_spec((1, GLA_QK)),
            _const_spec((1, MLA_Q_RANK)), _const_spec((MLA_Q_RANK, HEAD_PAD)),
            _const_spec((1, MLA_KV_RANK)), _const_spec((MLA_KV_RANK, HEAD_PAD)),
            _const_spec((MLA_KV_RANK, MLA_WIDTH)),
            _const_spec((1, LANES)), _const_spec((1, LANES)), _const_spec((8, LANES)),
        ],
        out_specs=[row(w) for w in out_widths],
        out_shape=[jax.ShapeDtypeStruct((t, w), d) for w, d in zip(out_widths, out_dtypes)],
        compiler_params=pltpu.CompilerParams(
            dimension_semantics=("arbitrary",), vmem_limit_bytes=VMEM_LIMIT_BYTES),
        name="proj",
    )(x2, pos2, attn_norm.reshape(1, D_MODEL), w_in_l, w2_l, gate_b.reshape(1, GLA_QK),
      q_norm.reshape(1, MLA_Q_RANK), w_uq_l.astype(BF16), kv_norm.reshape(1, MLA_KV_RANK),
      w_uk_l.astype(BF16), w_uv.astype(BF16), gq_l, gk_l, tab)


def _cumsum(sizes):
    total = 0
    for s in sizes:
        total += s
        yield total


def _gla_kernel(q_ref, k_ref, v_ref, la_ref, gr_ref, gon_ref, o_ref, st_ref, kf_ref, b_ref):
    seq = q_ref.shape[1]
    st_ref[...] = jnp.zeros_like(st_ref)
    row = lax.broadcasted_iota(jnp.int32, (CHUNK, CHUNK), 0)
    col = lax.broadcasted_iota(jnp.int32, (CHUNK, CHUNK), 1)
    causal = col <= row
    tril = causal.astype(BF16)
    lane = lax.broadcasted_iota(jnp.int32, (1, LANES), 1)
    lane_q = lax.broadcasted_iota(jnp.int32, (1, GLA_QK), 1)
    head_mask = [(lane // GLA_DK) == e for e in range(2)]
    gon = gon_ref[...]
    nt = (((1,), (1,)), ((), ()))
    tn = (((0,), (0,)), ((), ()))

    def chunk(c, carry):
        r0 = pl.multiple_of(c * CHUNK, CHUNK)
        rows = pl.ds(r0, CHUNK)
        la = la_ref[0, rows, :]
        hi = la.astype(BF16)
        r1 = la - hi.astype(F32)
        mid = r1.astype(BF16)
        lo = (r1 - mid.astype(F32)).astype(BF16)
        b = (jnp.dot(tril, hi, preferred_element_type=F32)
             + jnp.dot(tril, mid, preferred_element_type=F32)
             + jnp.dot(tril, lo, preferred_element_type=F32))
        b_last = b[CHUNK - 1:CHUNK, :]
        q = q_ref[0, rows, :].astype(F32)
        k = k_ref[0, rows, :].astype(F32)
        qt = q * jnp.exp(b)
        kd = k * jnp.exp(b_last - b)
        eb_last = jnp.exp(b_last)
        a_h = [jnp.where(head_mask[h % 2], qt[:, (h // 2) * LANES:(h // 2 + 1) * LANES], 0.0).astype(BF16)
               for h in range(GLA_HEADS)]

        def fast_scores():
            kt = (k * jnp.exp(-b)).astype(BF16)
            return tuple(
                lax.dot_general(a_h[h], kt[:, (h // 2) * LANES:(h // 2 + 1) * LANES], nt,
                                preferred_element_type=F32)
                for h in range(GLA_HEADS))

        def exact_scores():
            kf_ref[...] = k
            b_ref[...] = b

            def key_row(s, sc):
                k_s = kf_ref[pl.ds(s, 1), :]
                b_s = b_ref[pl.ds(s, 1), :]
                prod = q * k_s * jnp.exp(jnp.minimum(b - b_s, 0.0))
                out = []
                for h in range(GLA_HEADS):
                    hm = (lane_q // GLA_DK) == h
                    col_h = jnp.sum(jnp.where(hm, prod, 0.0), axis=-1, keepdims=True)
                    out.append(jnp.where(col == s, col_h, sc[h]))
                return tuple(out)

            zero = jnp.zeros((CHUNK, CHUNK), F32)
            return lax.fori_loop(0, CHUNK, key_row, (zero,) * GLA_HEADS)

        scores = lax.cond(jnp.min(b_last) >= GLA_FAST_MIN_LOG_DECAY, fast_scores, exact_scores)

        for h in range(GLA_HEADS):
            pair = slice((h // 2) * LANES, (h // 2 + 1) * LANES)
            vsl = slice(h * GLA_DV, (h + 1) * GLA_DV)
            sc = jnp.where(causal, scores[h], 0.0).astype(BF16)
            v_h = v_ref[0, rows, vsl]
            st = st_ref[h]
            o = (lax.dot_general(a_h[h], st.astype(BF16), nt, preferred_element_type=F32)
                 + jnp.dot(sc, v_h, preferred_element_type=F32))
            o = o * _rms_scale(o, GLA_DV) * gon
            o_ref[0, rows, vsl] = (o * gr_ref[0, rows, vsl].astype(F32)).astype(BF16)
            kd_h = jnp.where(head_mask[h % 2], kd[:, pair], 0.0).astype(BF16)
            upd = lax.dot_general(v_h, kd_h, tn, preferred_element_type=F32)
            st_ref[h] = st * eb_last[:, pair] + upd
        return carry

    lax.fori_loop(0, seq // CHUNK, chunk, 0)


def _gla(qg, kg, vg, la, gr, out_norm):
    b, s, _ = qg.shape
    blk = lambda w: pl.BlockSpec((1, s, w), lambda i: (i, 0, 0))
    return pl.pallas_call(
        _gla_kernel,
        grid=(b,),
        in_specs=[blk(GLA_QK), blk(GLA_QK), blk(GLA_WIDTH), blk(GLA_QK), blk(GLA_WIDTH),
                  _const_spec((1, GLA_DV))],
        out_specs=blk(GLA_WIDTH),
        out_shape=jax.ShapeDtypeStruct((b, s, GLA_WIDTH), BF16),
        scratch_shapes=[pltpu.VMEM((GLA_HEADS, GLA_DV, LANES), F32),
                        pltpu.VMEM((CHUNK, GLA_QK), F32),
                        pltpu.VMEM((CHUNK, GLA_QK), F32)],
        compiler_params=pltpu.CompilerParams(
            dimension_semantics=("arbitrary",), vmem_limit_bytes=VMEM_LIMIT_BYTES),
        name="gla",
    )(qg, kg, vg, la, gr, out_norm.reshape(1, GLA_DV))


def _mla_kernel(q_ref, k_ref, v_ref, o_ref):
    qi = pl.program_id(2)
    tq = q_ref.shape[1]
    nt = (((1,), (1,)), ((), ()))
    row_chunk = lax.broadcasted_iota(jnp.int32, (tq, tq), 0) // CHUNK
    col_chunk = lax.broadcasted_iota(jnp.int32, (tq, tq), 1) // CHUNK
    visible = col_chunk <= row_chunk
    lane = lax.broadcasted_iota(jnp.int32, (1, LANES), 1)
    outs = []
    for e in range(2):
        hsl = slice(e * LANES, (e + 1) * LANES)
        q = q_ref[0, :, hsl]

        def tile(j):
            rows = pl.ds(pl.multiple_of(j * tq, tq), tq)
            s = lax.dot_general(q, k_ref[0, rows, hsl], nt, preferred_element_type=F32)
            return s, v_ref[0, rows, :]

        s, v = tile(qi)
        s = jnp.where(visible, s, -jnp.inf)
        m = jnp.max(s, axis=-1, keepdims=True)
        p = jnp.exp(s - m)
        l = jnp.sum(p, axis=-1, keepdims=True)
        acc = jnp.dot(p.astype(BF16), v, preferred_element_type=F32)

        def body(j, carry):
            m, l, acc = carry
            s, v = tile(j)
            m_new = jnp.maximum(m, jnp.max(s, axis=-1, keepdims=True))
            alpha = jnp.exp(m - m_new)
            p = jnp.exp(s - m_new)
            l = alpha * l + jnp.sum(p, axis=-1, keepdims=True)
            acc = alpha * acc + jnp.dot(p.astype(BF16), v, preferred_element_type=F32)
            return m_new, l, acc

        m, l, acc = lax.fori_loop(0, qi, body, (m, l, acc))
        outs.append(acc / l)
    o_ref[0] = jnp.where(lane < MLA_V, outs[0], outs[1]).astype(BF16)


def _mla(q, k, v):
    b, s, _ = q.shape
    tq = TQ
    assert s % tq == 0
    return pl.pallas_call(
        _mla_kernel,
        grid=(b, MLA_HEADS // 2, s // tq),
        in_specs=[pl.BlockSpec((1, tq, 2 * LANES), lambda i, p, j: (i, j, p)),
                  pl.BlockSpec((1, s, 2 * LANES), lambda i, p, j: (i, 0, p)),
                  pl.BlockSpec((1, s, LANES), lambda i, p, j: (i, 0, p))],
        out_specs=pl.BlockSpec((1, tq, LANES), lambda i, p, j: (i, j, p)),
        out_shape=jax.ShapeDtypeStruct((b, s, MLA_WIDTH), BF16),
        compiler_params=pltpu.CompilerParams(
            dimension_semantics=("arbitrary", "arbitrary", "arbitrary"),
            vmem_limit_bytes=VMEM_LIMIT_BYTES),
        name="mla",
    )(q, k, v)


def _tail_kernel(x_ref, og_ref, om_ref, p_ref, w_out_ref, g_mlp_ref, w_up_ref, w_down_ref,
                 g_ple_ref, w_gate_ref, b_gate_ref, w_pp_ref, o_ref):
    h = (x_ref[...]
         + jnp.dot(og_ref[...], w_out_ref[:GLA_WIDTH, :], preferred_element_type=F32)
         + jnp.dot(om_ref[...], w_out_ref[GLA_WIDTH:, :], preferred_element_type=F32))
    m = (h * _rms_scale(h, D_MODEL) * g_mlp_ref[...]).astype(BF16)
    mlp = None
    for c in range(D_FF // FF_CHUNK):
        cols = slice(c * FF_CHUNK, (c + 1) * FF_CHUNK)
        u = jnp.maximum(jnp.dot(m, w_up_ref[:, cols], preferred_element_type=F32), 0.0)
        d = jnp.dot((u * u).astype(BF16), w_down_ref[cols, :], preferred_element_type=F32)
        mlp = d if mlp is None else mlp + d
    h = h + mlp
    g = (h * _rms_scale(h, D_MODEL) * g_ple_ref[...]).astype(BF16)
    gate = jax.nn.sigmoid(jnp.dot(g, w_gate_ref[...], preferred_element_type=F32) + b_gate_ref[...])
    pp = jnp.dot(p_ref[...].astype(BF16), w_pp_ref[...], preferred_element_type=F32)
    o_ref[...] = h + pp * gate


def _tail(x2, og, om, p2, w_out, mlp_norm, w_up, w_down, ple_norm, w_gate, b_gate, w_pp):
    t = x2.shape[0]
    tm = TM_TAIL
    assert t % tm == 0
    row = lambda w: pl.BlockSpec((tm, w), lambda i: (i, 0))
    return pl.pallas_call(
        _tail_kernel,
        grid=(t // tm,),
        in_specs=[row(D_MODEL), row(GLA_WIDTH), row(MLA_WIDTH), row(PLE_DIM),
                  _const_spec((D_MODEL, D_MODEL)), _const_spec((1, D_MODEL)),
                  _const_spec((D_MODEL, D_FF)), _const_spec((D_FF, D_MODEL)),
                  _const_spec((1, D_MODEL)), _const_spec((D_MODEL, D_MODEL)),
                  _const_spec((1, D_MODEL)), _const_spec((PLE_DIM, D_MODEL))],
        out_specs=row(D_MODEL),
        out_shape=jax.ShapeDtypeStruct((t, D_MODEL), F32),
        compiler_params=pltpu.CompilerParams(
            dimension_semantics=("arbitrary",), vmem_limit_bytes=VMEM_LIMIT_BYTES),
        name="tail",
    )(x2, og, om, p2, w_out.astype(BF16), mlp_norm.reshape(1, D_MODEL), w_up.astype(BF16),
      w_down.astype(BF16), ple_norm.reshape(1, D_MODEL), w_gate.astype(BF16),
      b_gate.reshape(1, D_MODEL), w_pp.astype(BF16))


def kernel(x, p, positions, attn_norm, w_in, gla_gate_w2, gla_gate_b, gla_out_norm, mla_q_norm,
           mla_w_uq, mla_kv_norm, mla_w_ukv, qk_norm_q, qk_norm_k, w_out, mlp_norm, w_mlp_up,
           w_mlp_down, ple_norm, w_ple_gate, b_ple_gate, w_ple_proj):
    b, s, d = x.shape
    depth = w_in.shape[0]
    t = b * s
    pos2 = positions.reshape(t, 1)
    h = x.reshape(t, d)
    for i in range(depth):
        qg, kg, vg, la, gr, q, k, v = _projection(
            h, pos2, attn_norm[i], w_in[i], gla_gate_w2[i], gla_gate_b[i], mla_q_norm[i],
            mla_w_uq[i], mla_kv_norm[i], mla_w_ukv[i], qk_norm_q[i], qk_norm_k[i])
        seq = lambda a: a.reshape(b, s, a.shape[-1])
        og = _gla(seq(qg), seq(kg), seq(vg), seq(la), seq(gr), gla_out_norm[i])
        om = _mla(seq(q), seq(k), seq(v))
        h = _tail(h, og.reshape(t, GLA_WIDTH), om.reshape(t, MLA_WIDTH), p[i].reshape(t, PLE_DIM),
                  w_out[i], mlp_norm[i], w_mlp_up[i], w_mlp_down[i], ple_norm[i], w_ple_gate[i],
                  b_ple_gate[i], w_ple_proj[i])
    return h.reshape(b, s, d)
```

```python
import functools
import math

import jax
import jax.numpy as jnp
from jax import lax
from jax.experimental import pallas as pl
from jax.experimental.pallas import tpu as pltpu

F32 = jnp.float32
BF16 = jnp.bfloat16

D_MODEL = 1024
CHUNK = 64
PLE_DIM = 256
D_FF = 4 * D_MODEL
EPS = 1e-6
GLA_HEADS = 4
GLA_DK = 64
GLA_DV = 128
GLA_GATE_RANK = 16
GLA_TAU = 16.0
GLA_QK = GLA_HEADS * GLA_DK
GLA_WIDTH = GLA_HEADS * GLA_DV
MLA_HEADS = 8
MLA_NOPE = 64
MLA_ROPE = 32
MLA_V = 64
MLA_QK = MLA_NOPE + MLA_ROPE
MLA_Q_RANK = 256
MLA_KV_RANK = 128
MLA_WIDTH = MLA_HEADS * MLA_V
ROPE_THETA = 10000.0
IN_SPLITS = (GLA_QK, GLA_QK, GLA_WIDTH, GLA_GATE_RANK, GLA_WIDTH, MLA_Q_RANK, MLA_KV_RANK, MLA_ROPE)

LANES = 128
HALF_ROPE = MLA_ROPE // 2
HEAD_PAD = MLA_HEADS * LANES
D_IN_PAD = 2048
VMEM_LIMIT_BYTES = 56 * 1024 * 1024

TM_PROJ = 512
TM_TAIL = 512
FF_CHUNK = 1024
TQ = 256
GLA_FAST_MIN_LOG_DECAY = -80.0
MASKED_LOGIT = -1e30

ROW_INVFREQ, ROW_SIGN, ROW_MASK96, ROW_KRMASK = 0, 1, 2, 3


def _const_spec(shape):
    return pl.BlockSpec(shape, lambda *_: (0,) * len(shape), pipeline_mode=pl.Buffered(1))


def _rms_scale(v, n):
    return lax.rsqrt(jnp.sum(v * v, axis=-1, keepdims=True) * (1.0 / n) + EPS)


def _proj_kernel(x_ref, pos_ref, g_attn_ref, w_in_ref, w2_ref, b2_ref, gqn_ref, w_uq_ref,
                 gkvn_ref, w_uk_ref, w_uv_ref, gq_l_ref, gk_l_ref, tab_ref,
                 qg_ref, kg_ref, vg_ref, la_ref, gr_ref, q_ref, k_ref, v_ref):
    x = x_ref[...]
    xn = (x * _rms_scale(x, D_MODEL) * g_attn_ref[...]).astype(BF16)

    def proj(lo, hi):
        return jnp.dot(xn, w_in_ref[:, lo:hi], preferred_element_type=F32)

    zqk = proj(0, 2 * GLA_QK)
    qg_ref[...] = (zqk[:, :GLA_QK] * (GLA_DK ** -0.5)).astype(BF16)
    kg_ref[...] = zqk[:, GLA_QK:].astype(BF16)
    vg_ref[...] = proj(512, 1024).astype(BF16)
    gr = proj(1024, 1536)
    gr_ref[...] = (gr * jax.nn.sigmoid(gr)).astype(BF16)
    zc = proj(1536, 2048)
    cq = zc[:, :MLA_Q_RANK]
    ckv = zc[:, MLA_Q_RANK:MLA_Q_RANK + MLA_KV_RANK]
    slot = zc[:, MLA_Q_RANK + MLA_KV_RANK:]

    xg = jnp.dot(slot.astype(BF16), w2_ref[...], preferred_element_type=F32) + b2_ref[...]
    la_ref[...] = jax.nn.log_sigmoid(xg) * (1.0 / GLA_TAU)

    inv_freq = tab_ref[ROW_INVFREQ:ROW_INVFREQ + 1, :]
    sign = tab_ref[ROW_SIGN:ROW_SIGN + 1, :]
    mask96 = tab_ref[ROW_MASK96:ROW_MASK96 + 1, :]
    krmask = tab_ref[ROW_KRMASK:ROW_KRMASK + 1, :]
    ang = pos_ref[...].astype(F32) * inv_freq
    cos_k = jnp.cos(ang) * mask96
    sin_k = jnp.sin(ang) * sign
    scale = 1.0 / math.sqrt(MLA_QK)
    cos_q = cos_k * scale
    sin_q = sin_k * scale

    cqn = (cq * _rms_scale(cq, MLA_Q_RANK) * gqn_ref[...]).astype(BF16)
    qall = jnp.dot(cqn, w_uq_ref[...], preferred_element_type=F32)
    gq_l = gq_l_ref[...]
    for h in range(MLA_HEADS):
        sl = slice(h * LANES, (h + 1) * LANES)
        qh = qall[:, sl]
        r = lax.rsqrt(jnp.sum(qh * qh * mask96, axis=-1, keepdims=True) * (1.0 / MLA_QK) + EPS)
        qn = qh * r * gq_l
        q_ref[:, sl] = (qn * cos_q + pltpu.roll(qn, LANES // 2, 1) * sin_q).astype(BF16)

    ckvn = (ckv * _rms_scale(ckv, MLA_KV_RANK) * gkvn_ref[...]).astype(BF16)
    kall = jnp.dot(ckvn, w_uk_ref[...], preferred_element_type=F32)
    kr = slot * krmask
    gk_l = gk_l_ref[...]
    for h in range(MLA_HEADS):
        sl = slice(h * LANES, (h + 1) * LANES)
        kh = kall[:, sl] + kr
        r = lax.rsqrt(jnp.sum(kh * kh * mask96, axis=-1, keepdims=True) * (1.0 / MLA_QK) + EPS)
        kn = kh * r * gk_l
        k_ref[:, sl] = (kn * cos_k + pltpu.roll(kn, LANES // 2, 1) * sin_k).astype(BF16)
    v_ref[...] = jnp.dot(ckvn, w_uv_ref[...], preferred_element_type=F32).astype(BF16)


def _head_layout(nope, rope):
    x1, x2 = rope[..., :HALF_ROPE], rope[..., HALF_ROPE:]
    return jnp.concatenate([nope[..., :32], x1, x2, nope[..., 32:], x2, x1], axis=-1)


def _projection(x2, pos2, attn_norm, w_in, gate_w2, gate_b, q_norm, w_uq, kv_norm, w_ukv,
                qk_norm_q, qk_norm_k):
    t = x2.shape[0]
    tm = TM_PROJ
    assert t % tm == 0

    gq, gk, gv, g_low, g_r, c_q, c_kv, k_r = jnp.split(w_in, list(_cumsum(IN_SPLITS))[:-1], axis=1)
    zeros = lambda n: jnp.zeros((D_MODEL, n), w_in.dtype)
    kr_sw = jnp.concatenate([k_r[:, HALF_ROPE:], k_r[:, :HALF_ROPE]], axis=1)
    slot = jnp.concatenate([g_low, zeros(32 - GLA_GATE_RANK), k_r, zeros(32), kr_sw], axis=1)
    w_in_l = jnp.concatenate([gq, gk, gv, g_r, c_q, c_kv, slot], axis=1).astype(BF16)
    assert w_in_l.shape == (D_MODEL, D_IN_PAD)

    w2_l = jnp.zeros((LANES, GLA_QK), F32).at[:GLA_GATE_RANK].set(gate_w2).astype(BF16)

    w_uq_h = w_uq.reshape(MLA_Q_RANK, MLA_HEADS, MLA_QK)
    w_uq_l = _head_layout(w_uq_h[..., :MLA_NOPE], w_uq_h[..., MLA_NOPE:]).reshape(MLA_Q_RANK, HEAD_PAD)
    w_ukv_h = w_ukv.reshape(MLA_KV_RANK, MLA_HEADS, MLA_NOPE + MLA_V)
    w_uk_l = _head_layout(w_ukv_h[..., :MLA_NOPE],
                          jnp.zeros((MLA_KV_RANK, MLA_HEADS, MLA_ROPE), F32)).reshape(MLA_KV_RANK, HEAD_PAD)
    w_uv = w_ukv_h[..., MLA_NOPE:].reshape(MLA_KV_RANK, MLA_WIDTH)
    gq_l = _head_layout(qk_norm_q[:MLA_NOPE], qk_norm_q[MLA_NOPE:]).reshape(1, LANES)
    gk_l = _head_layout(qk_norm_k[:MLA_NOPE], qk_norm_k[MLA_NOPE:]).reshape(1, LANES)

    inv_freq = ROPE_THETA ** (-jnp.arange(0, MLA_ROPE, 2, dtype=F32) / MLA_ROPE)
    z32 = jnp.zeros((32,), F32)
    o16 = jnp.ones((HALF_ROPE,), F32)
    tab = jnp.zeros((8, LANES), F32)
    tab = tab.at[ROW_INVFREQ].set(jnp.concatenate([z32, inv_freq, inv_freq, z32, z32]))
    tab = tab.at[ROW_SIGN].set(jnp.concatenate([z32, -o16, o16, z32, z32]))
    tab = tab.at[ROW_MASK96].set(jnp.concatenate([jnp.ones((96,), F32), z32]))
    tab = tab.at[ROW_KRMASK].set(jnp.concatenate([z32, jnp.ones((32,), F32), z32, jnp.ones((32,), F32)]))

    row = lambda w: pl.BlockSpec((tm, w), lambda i: (i, 0))
    out_widths = (GLA_QK, GLA_QK, GLA_WIDTH, GLA_QK, GLA_WIDTH, HEAD_PAD, HEAD_PAD, MLA_WIDTH)
    out_dtypes = (BF16, BF16, BF16, F32, BF16, BF16, BF16, BF16)
    return pl.pallas_call(
        _proj_kernel,
        grid=(t // tm,),
        in_specs=[
            row(D_MODEL), row(1),
            _const_spec((1, D_MODEL)), _const_spec((D_MODEL, D_IN_PAD)),
            _const_spec((LANES, GLA_QK)), _const_spec((1, GLA_QK)),
            _const_spec((1, MLA_Q_RANK)), _const_spec((MLA_Q_RANK, HEAD_PAD)),
            _const_spec((1, MLA_KV_RANK)), _const_spec((MLA_KV_RANK, HEAD_PAD)),
            _const_spec((MLA_KV_RANK, MLA_WIDTH)),
            _const_spec((1, LANES)), _const_spec((1, LANES)), _const_spec((8, LANES)),
        ],
        out_specs=[row(w) for w in out_widths],
        out_shape=[jax.ShapeDtypeStruct((t, w), d) for w, d in zip(out_widths, out_dtypes)],
        compiler_params=pltpu.CompilerParams(
            dimension_semantics=("arbitrary",), vmem_limit_bytes=VMEM_LIMIT_BYTES),
        name="proj",
    )(x2, pos2, attn_norm.reshape(1, D_MODEL), w_in_l, w2_l, gate_b.reshape(1, GLA_QK),
      q_norm.reshape(1, MLA_Q_RANK), w_uq_l.astype(BF16), kv_norm.reshape(1, MLA_KV_RANK),
      w_uk_l.astype(BF16), w_uv.astype(BF16), gq_l, gk_l, tab)


def _cumsum(sizes):
    total = 0
    for s in sizes:
        total += s
        yield total


def _gla_kernel(q_ref, k_ref, v_ref, la_ref, gr_ref, gon_ref, o_ref, st_ref, kf_ref, b_ref):
    seq = q_ref.shape[1]
    st_ref[...] = jnp.zeros_like(st_ref)
    row = lax.broadcasted_iota(jnp.int32, (CHUNK, CHUNK), 0)
    col = lax.broadcasted_iota(jnp.int32, (CHUNK, CHUNK), 1)
    causal = col <= row
    tril = causal.astype(BF16)
    lane = lax.broadcasted_iota(jnp.int32, (1, LANES), 1)
    lane_q = lax.broadcasted_iota(jnp.int32, (1, GLA_QK), 1)
    head_mask = [(lane // GLA_DK) == e for e in range(2)]
    gon = gon_ref[...]
    nt = (((1,), (1,)), ((), ()))
    tn = (((0,), (0,)), ((), ()))

    def chunk(c, carry):
        r0 = pl.multiple_of(c * CHUNK, CHUNK)
        rows = pl.ds(r0, CHUNK)
        la = la_ref[0, rows, :]
        hi = la.astype(BF16)
        r1 = la - hi.astype(F32)
        mid = r1.astype(BF16)
        lo = (r1 - mid.astype(F32)).astype(BF16)
        b = (jnp.dot(tril, hi, preferred_element_type=F32)
             + jnp.dot(tril, mid, preferred_element_type=F32)
             + jnp.dot(tril, lo, preferred_element_type=F32))
        b_last = b[CHUNK - 1:CHUNK, :]
        q = q_ref[0, rows, :].astype(F32)
        k = k_ref[0, rows, :].astype(F32)
        qt = q * jnp.exp(b)
        kd = k * jnp.exp(b_last - b)
        eb_last = jnp.exp(b_last)
        a_h = [jnp.where(head_mask[h % 2], qt[:, (h // 2) * LANES:(h // 2 + 1) * LANES], 0.0).astype(BF16)
               for h in range(GLA_HEADS)]

        def fast_scores():
            kt = (k * jnp.exp(-b)).astype(BF16)
            return tuple(
                lax.dot_general(a_h[h], kt[:, (h // 2) * LANES:(h // 2 + 1) * LANES], nt,
                                preferred_element_type=F32)
                for h in range(GLA_HEADS))

        def exact_scores():
            kf_ref[...] = k
            b_ref[...] = b

            def key_row(s, sc):
                k_s = kf_ref[pl.ds(s, 1), :]
                b_s = b_ref[pl.ds(s, 1), :]
                prod = q * k_s * jnp.exp(jnp.minimum(b - b_s, 0.0))
                out = []
                for h in range(GLA_HEADS):
                    hm = (lane_q // GLA_DK) == h
                    col_h = jnp.sum(jnp.where(hm, prod, 0.0), axis=-1, keepdims=True)
                    out.append(jnp.where(col == s, col_h, sc[h]))
                return tuple(out)

            zero = jnp.zeros((CHUNK, CHUNK), F32)
            return lax.fori_loop(0, CHUNK, key_row, (zero,) * GLA_HEADS)

        scores = lax.cond(jnp.min(b_last) >= GLA_FAST_MIN_LOG_DECAY, fast_scores, exact_scores)

        for h in range(GLA_HEADS):
            pair = slice((h // 2) * LANES, (h // 2 + 1) * LANES)
            vsl = slice(h * GLA_DV, (h + 1) * GLA_DV)
            sc = jnp.where(causal, scores[h], 0.0).astype(BF16)
            v_h = v_ref[0, rows, vsl]
            st = st_ref[h]
            o = (lax.dot_general(a_h[h], st.astype(BF16), nt, preferred_element_type=F32)
                 + jnp.dot(sc, v_h, preferred_element_type=F32))
            o = o * _rms_scale(o, GLA_DV) * gon
            o_ref[0, rows, vsl] = (o * gr_ref[0, rows, vsl].astype(F32)).astype(BF16)
            kd_h = jnp.where(head_mask[h % 2], kd[:, pair], 0.0).astype(BF16)
            upd = lax.dot_general(v_h, kd_h, tn, preferred_element_type=F32)
            st_ref[h] = st * eb_last[:, pair] + upd
        return carry

    lax.fori_loop(0, seq // CHUNK, chunk, 0)


def _gla(qg, kg, vg, la, gr, out_norm):
    b, s, _ = qg.shape
    blk = lambda w: pl.BlockSpec((1, s, w), lambda i: (i, 0, 0))
    return pl.pallas_call(
        _gla_kernel,
        grid=(b,),
        in_specs=[blk(GLA_QK), blk(GLA_QK), blk(GLA_WIDTH), blk(GLA_QK), blk(GLA_WIDTH),
                  _const_spec((1, GLA_DV))],
        out_specs=blk(GLA_WIDTH),
        out_shape=jax.ShapeDtypeStruct((b, s, GLA_WIDTH), BF16),
        scratch_shapes=[pltpu.VMEM((GLA_HEADS, GLA_DV, LANES), F32),
                        pltpu.VMEM((CHUNK, GLA_QK), F32),
                        pltpu.VMEM((CHUNK, GLA_QK), F32)],
        compiler_params=pltpu.CompilerParams(
            dimension_semantics=("arbitrary",), vmem_limit_bytes=VMEM_LIMIT_BYTES),
        name="gla",
    )(qg, kg, vg, la, gr, out_norm.reshape(1, GLA_DV))


def _mla_tiles(n):
    pairs = [(i, i) for i in range(n)] + [(i, j) for i in range(1, n) for j in range(i)]
    return [i for i, _ in pairs], [j for _, j in pairs]


def _mla_kernel(ti_ref, tj_ref, q_ref, k_ref, v_ref, o_ref, s_ref, m_ref, acc_ref, v1_ref):
    seq = q_ref.shape[1]
    tq = TQ
    n = seq // tq
    n_tiles = n * (n + 1) // 2
    nt = (((1,), (1,)), ((), ()))
    heads = range(2)
    lane = lax.broadcasted_iota(jnp.int32, (1, LANES), 1)
    own = [(lane // MLA_V) == e for e in heads]

    def rows(t):
        return pl.ds(pl.multiple_of(t * tq, tq), tq)

    def logits(e, i, j):
        hsl = slice(e * LANES, (e + 1) * LANES)
        return lax.dot_general(q_ref[0, rows(i), hsl], k_ref[0, rows(j), hsl], nt,
                               preferred_element_type=F32)

    def lane_fold_max(a):
        out = a[:, :LANES]
        for c in range(1, tq // LANES):
            out = jnp.maximum(out, a[:, c * LANES:(c + 1) * LANES])
        return out

    v = v_ref[0]
    for e in heads:
        v1_ref[e] = jnp.where(own[e], v, 1.0).astype(BF16)

    def diag(i, carry):
        row_chunk = lax.broadcasted_iota(jnp.int32, (tq, tq), 0) // CHUNK
        col_chunk = lax.broadcasted_iota(jnp.int32, (tq, tq), 1) // CHUNK
        for e in heads:
            s = jnp.where(col_chunk <= row_chunk, logits(e, i, i), MASKED_LOGIT)
            s_ref[e, i] = s
            m_ref[e, i] = lane_fold_max(s)
            acc_ref[e, i] = jnp.zeros((tq, LANES), F32)
        return carry

    lax.fori_loop(0, n, diag, 0, unroll=2)

    def pass1(t, carry):
        i, j = ti_ref[t], tj_ref[t]
        for e in heads:
            s = logits(e, i, j)
            s_ref[e, t] = s
            m_ref[e, i] = jnp.maximum(m_ref[e, i], lane_fold_max(s))
        return carry

    lax.fori_loop(n, n_tiles, pass1, 0, unroll=2)

    def row_max(i, carry):
        for e in heads:
            m_ref[e, i] = jnp.broadcast_to(jnp.max(m_ref[e, i], axis=-1, keepdims=True), (tq, LANES))
        return carry

    lax.fori_loop(0, n, row_max, 0)

    def pass2(t, carry):
        i, j = ti_ref[t], tj_ref[t]
        for e in heads:
            s = s_ref[e, t]
            m = m_ref[e, i]
            p = jnp.concatenate(
                [jnp.exp(s[:, c * LANES:(c + 1) * LANES] - m) for c in range(tq // LANES)], axis=1)
            acc_ref[e, i] += jnp.dot(p.astype(BF16), v1_ref[e, rows(j), :], preferred_element_type=F32)
        return carry

    lax.fori_loop(0, n_tiles, pass2, 0, unroll=2)

    def finish(i, carry):
        out = []
        for e in heads:
            acc = acc_ref[e, i]
            denom = jnp.sum(jnp.where(lane == (1 - e) * MLA_V, acc, 0.0), axis=-1, keepdims=True)
            out.append(acc / denom)
        o_ref[0, rows(i), :] = jnp.where(own[0], out[0], out[1]).astype(BF16)
        return carry

    lax.fori_loop(0, n, finish, 0)


def _mla(q, k, v):
    b, s, _ = q.shape
    tq = TQ
    assert s % (2 * tq) == 0
    n = s // tq
    ti, tj = _mla_tiles(n)
    blk = lambda w: pl.BlockSpec((1, s, w), lambda i, p, *_: (i, 0, p))
    return pl.pallas_call(
        _mla_kernel,
        grid_spec=pltpu.PrefetchScalarGridSpec(
            num_scalar_prefetch=2,
            grid=(b, MLA_HEADS // 2),
            in_specs=[blk(2 * LANES), blk(2 * LANES), blk(LANES)],
            out_specs=blk(LANES),
            scratch_shapes=[pltpu.VMEM((2, len(ti), tq, tq), F32),
                            pltpu.VMEM((2, n, tq, LANES), F32),
                            pltpu.VMEM((2, n, tq, LANES), F32),
                            pltpu.VMEM((2, s, LANES), BF16)]),
        out_shape=jax.ShapeDtypeStruct((b, s, MLA_WIDTH), BF16),
        compiler_params=pltpu.CompilerParams(
            dimension_semantics=("arbitrary", "arbitrary"),
            vmem_limit_bytes=VMEM_LIMIT_BYTES),
        name="mla",
    )(jnp.asarray(ti, jnp.int32), jnp.asarray(tj, jnp.int32), q, k, v)


def _tail_kernel(x_ref, og_ref, om_ref, p_ref, w_out_ref, g_mlp_ref, w_up_ref, w_down_ref,
                 g_ple_ref, w_gate_ref, b_gate_ref, w_pp_ref, o_ref):
    h = (x_ref[...]
         + jnp.dot(og_ref[...], w_out_ref[:GLA_WIDTH, :], preferred_element_type=F32)
         + jnp.dot(om_ref[...], w_out_ref[GLA_WIDTH:, :], preferred_element_type=F32))
    m = (h * _rms_scale(h, D_MODEL) * g_mlp_ref[...]).astype(BF16)
    mlp = None
    for c in range(D_FF // FF_CHUNK):
        cols = slice(c * FF_CHUNK, (c + 1) * FF_CHUNK)
        u = jnp.maximum(jnp.dot(m, w_up_ref[:, cols], preferred_element_type=F32), 0.0)
        d = jnp.dot((u * u).astype(BF16), w_down_ref[cols, :], preferred_element_type=F32)
        mlp = d if mlp is None else mlp + d
    h = h + mlp
    g = (h * _rms_scale(h, D_MODEL) * g_ple_ref[...]).astype(BF16)
    gate = jax.nn.sigmoid(jnp.dot(g, w_gate_ref[...], preferred_element_type=F32) + b_gate_ref[...])
    pp = jnp.dot(p_ref[...].astype(BF16), w_pp_ref[...], preferred_element_type=F32)
    o_ref[...] = h + pp * gate


def _tail(x2, og, om, p2, w_out, mlp_norm, w_up, w_down, ple_norm, w_gate, b_gate, w_pp):
    t = x2.shape[0]
    tm = TM_TAIL
    assert t % tm == 0
    row = lambda w: pl.BlockSpec((tm, w), lambda i: (i, 0))
    return pl.pallas_call(
        _tail_kernel,
        grid=(t // tm,),
        in_specs=[row(D_MODEL), row(GLA_WIDTH), row(MLA_WIDTH), row(PLE_DIM),
                  _const_spec((D_MODEL, D_MODEL)), _const_spec((1, D_MODEL)),
                  _const_spec((D_MODEL, D_FF)), _const_spec((D_FF, D_MODEL)),
                  _const_spec((1, D_MODEL)), _const_spec((D_MODEL, D_MODEL)),
                  _const_spec((1, D_MODEL)), _const_spec((PLE_DIM, D_MODEL))],
        out_specs=row(D_MODEL),
        out_shape=jax.ShapeDtypeStruct((t, D_MODEL), F32),
        compiler_params=pltpu.CompilerParams(
            dimension_semantics=("arbitrary",), vmem_limit_bytes=VMEM_LIMIT_BYTES),
        name="tail",
    )(x2, og, om, p2, w_out.astype(BF16), mlp_norm.reshape(1, D_MODEL), w_up.astype(BF16),
      w_down.astype(BF16), ple_norm.reshape(1, D_MODEL), w_gate.astype(BF16),
      b_gate.reshape(1, D_MODEL), w_pp.astype(BF16))


def kernel(x, p, positions, attn_norm, w_in, gla_gate_w2, gla_gate_b, gla_out_norm, mla_q_norm,
           mla_w_uq, mla_kv_norm, mla_w_ukv, qk_norm_q, qk_norm_k, w_out, mlp_norm, w_mlp_up,
           w_mlp_down, ple_norm, w_ple_gate, b_ple_gate, w_ple_proj):
    b, s, d = x.shape
    depth = w_in.shape[0]
    t = b * s
    pos2 = positions.reshape(t, 1)
    h = x.reshape(t, d)
    for i in range(depth):
        qg, kg, vg, la, gr, q, k, v = _projection(
            h, pos2, attn_norm[i], w_in[i], gla_gate_w2[i], gla_gate_b[i], mla_q_norm[i],
            mla_w_uq[i], mla_kv_norm[i], mla_w_ukv[i], qk_norm_q[i], qk_norm_k[i])
        seq = lambda a: a.reshape(b, s, a.shape[-1])
        og = _gla(seq(qg), seq(kg), seq(vg), seq(la), seq(gr), gla_out_norm[i])
        om = _mla(seq(q), seq(k), seq(v))
        h = _tail(h, og.reshape(t, GLA_WIDTH), om.reshape(t, MLA_WIDTH), p[i].reshape(t, PLE_DIM),
                  w_out[i], mlp_norm[i], w_mlp_up[i], w_mlp_down[i], ple_norm[i], w_ple_gate[i],
                  b_ple_gate[i], w_ple_proj[i])
    return h.reshape(b, s, d)
```

```python
import functools
import math

import jax
import jax.numpy as jnp
from jax import lax
from jax.experimental import pallas as pl
from jax.experimental.pallas import tpu as pltpu

F32 = jnp.float32
BF16 = jnp.bfloat16

D_MODEL = 1024
CHUNK = 64
PLE_DIM = 256
D_FF = 4 * D_MODEL
EPS = 1e-6
GLA_HEADS = 4
GLA_DK = 64
GLA_DV = 128
GLA_GATE_RANK = 16
GLA_TAU = 16.0
GLA_QK = GLA_HEADS * GLA_DK
GLA_WIDTH = GLA_HEADS * GLA_DV
MLA_HEADS = 8
MLA_NOPE = 64
MLA_ROPE = 32
MLA_V = 64
MLA_QK = MLA_NOPE + MLA_ROPE
MLA_Q_RANK = 256
MLA_KV_RANK = 128
MLA_WIDTH = MLA_HEADS * MLA_V
ROPE_THETA = 10000.0
IN_SPLITS = (GLA_QK, GLA_QK, GLA_WIDTH, GLA_GATE_RANK, GLA_WIDTH, MLA_Q_RANK, MLA_KV_RANK, MLA_ROPE)

LANES = 128
SUBLANES = 8
HALF_ROPE = MLA_ROPE // 2
HEAD_PAD = MLA_HEADS * LANES
D_IN_PAD = 2048
VMEM_LIMIT_BYTES = 56 * 1024 * 1024

TM_PROJ = 512
TM_TAIL = 512
FF_CHUNK = 1024
TQ = 256
ROPE_ROWS = 64
GLA_FAST_MIN_LOG_DECAY = -80.0
MASKED_LOGIT = -1e30

ROW_FREQ16, ROW_MASK96, ROW_KRMASK, ROW_GQ, ROW_GQ_SW, ROW_GK, ROW_GK_SW = range(7)


def _const_spec(shape):
    return pl.BlockSpec(shape, lambda *_: (0,) * len(shape), pipeline_mode=pl.Buffered(1))


def _rms_scale(v, n):
    return lax.rsqrt(jnp.sum(v * v, axis=-1, keepdims=True) * (1.0 / n) + EPS)


def _proj_kernel(x_ref, pos_ref, g_attn_ref, w_in_ref, w2_ref, b2_ref, gqn_ref, w_uq_ref,
                 gkvn_ref, w_uk_ref, w_uv_ref, tab_ref,
                 qg_ref, kg_ref, vg_ref, la_ref, gr_ref, q_ref, k_ref, v_ref,
                 qs_ref, ks_ref, slot_ref):
    x = x_ref[...]
    xn = (x * _rms_scale(x, D_MODEL) * g_attn_ref[...]).astype(BF16)

    def proj(lo, hi):
        return jnp.dot(xn, w_in_ref[:, lo:hi], preferred_element_type=F32)

    zqk = proj(0, 2 * GLA_QK)
    qg_ref[...] = (zqk[:, :GLA_QK] * (GLA_DK ** -0.5)).astype(BF16)
    kg_ref[...] = zqk[:, GLA_QK:].astype(BF16)
    vg_ref[...] = proj(512, 1024).astype(BF16)
    gr = proj(1024, 1536)
    gr_ref[...] = (gr * jax.nn.sigmoid(gr)).astype(BF16)
    zc = proj(1536, 2048)
    cq = zc[:, :MLA_Q_RANK]
    ckv = zc[:, MLA_Q_RANK:MLA_Q_RANK + MLA_KV_RANK]
    slot = zc[:, MLA_Q_RANK + MLA_KV_RANK:]
    slot_ref[...] = slot

    xg = jnp.dot(slot.astype(BF16), w2_ref[...], preferred_element_type=F32) + b2_ref[...]
    la_ref[...] = (jnp.minimum(xg, 0.0) - jnp.log(1.0 + jnp.exp(-jnp.abs(xg)))) * (1.0 / GLA_TAU)

    cqn = (cq * _rms_scale(cq, MLA_Q_RANK) * gqn_ref[...]).astype(BF16)
    qs_ref[...] = jnp.dot(cqn, w_uq_ref[...], preferred_element_type=F32)
    ckvn = (ckv * _rms_scale(ckv, MLA_KV_RANK) * gkvn_ref[...]).astype(BF16)
    ks_ref[...] = jnp.dot(ckvn, w_uk_ref[...], preferred_element_type=F32)
    v_ref[...] = jnp.dot(ckvn, w_uv_ref[...], preferred_element_type=F32).astype(BF16)

    tab = lambda r: tab_ref[r:r + 1, :]
    mask96, krmask = tab(ROW_MASK96), tab(ROW_KRMASK)
    gq, gq_sw = tab(ROW_GQ), tab(ROW_GQ_SW)
    gk, gk_sw = tab(ROW_GK) * math.sqrt(MLA_QK), tab(ROW_GK_SW) * math.sqrt(MLA_QK)
    half = LANES // 2

    lane = lax.broadcasted_iota(jnp.int32, (1, LANES), 1)
    groups = ROPE_ROWS // SUBLANES
    assert groups * HALF_ROPE == LANES
    x1_lanes = (lane >= 32) & (lane < 32 + HALF_ROPE)
    x2_lanes = (lane >= 32 + HALF_ROPE) & (lane < 64)

    def rope_tables(rows):
        pos = jnp.broadcast_to(pos_ref[rows, :].astype(F32), (ROPE_ROWS, LANES))
        packed = jnp.zeros((SUBLANES, LANES), F32)
        for a in range(groups):
            packed = jnp.where(lane // HALF_ROPE == a, pos[a * SUBLANES:(a + 1) * SUBLANES], packed)
        ang = packed * tab(ROW_FREQ16)
        cos_p, sin_p = jnp.cos(ang), jnp.sin(ang)
        cos, sin = [], []
        for a in range(groups):
            s1 = (32 - HALF_ROPE * a) % LANES
            s2 = (32 + HALF_ROPE - HALF_ROPE * a) % LANES
            c1, c2 = pltpu.roll(cos_p, s1, 1), pltpu.roll(cos_p, s2, 1)
            n1, n2 = pltpu.roll(sin_p, s1, 1), pltpu.roll(sin_p, s2, 1)
            cos.append(jnp.where(x1_lanes, c1, jnp.where(x2_lanes, c2, mask96)))
            sin.append(jnp.where(x1_lanes, -n1, jnp.where(x2_lanes, n2, 0.0)))
        return jnp.concatenate(cos, axis=0), jnp.concatenate(sin, axis=0)

    sum96 = (lax.broadcasted_iota(jnp.int32, (LANES, LANES), 0) < MLA_QK).astype(BF16)

    def head_rms(v):
        ssq = jnp.dot((v * v).astype(BF16), sum96, preferred_element_type=F32)
        return lax.rsqrt(ssq + MLA_QK * EPS)

    def head_rows(rb, carry):
        rows = pl.ds(pl.multiple_of(rb * ROPE_ROWS, ROPE_ROWS), ROPE_ROWS)
        cos, sin = rope_tables(rows)
        kr = slot_ref[rows, :] * krmask
        kr_rot = pltpu.roll(kr, half, 1) * (sin * gk_sw)
        cos_q, sin_q, cos_k = cos * gq, sin * gq_sw, cos * gk
        for h in range(MLA_HEADS):
            sl = slice(h * LANES, (h + 1) * LANES)
            qh = qs_ref[rows, sl]
            q_ref[rows, sl] = ((qh * cos_q + pltpu.roll(qh, half, 1) * sin_q) * head_rms(qh)).astype(BF16)
            kh = ks_ref[rows, sl] + kr
            k_ref[rows, sl] = ((kh * cos_k + kr_rot) * head_rms(kh)).astype(BF16)
        return carry

    lax.fori_loop(0, x_ref.shape[0] // ROPE_ROWS, head_rows, 0, unroll=2)


def _head_layout(nope, rope):
    x1, x2 = rope[..., :HALF_ROPE], rope[..., HALF_ROPE:]
    return jnp.concatenate([nope[..., :32], x1, x2, nope[..., 32:], x2, x1], axis=-1)


def _projection(x2, pos2, attn_norm, w_in, gate_w2, gate_b, q_norm, w_uq, kv_norm, w_ukv,
                qk_norm_q, qk_norm_k):
    t = x2.shape[0]
    tm = TM_PROJ
    assert t % tm == 0

    gq, gk, gv, g_low, g_r, c_q, c_kv, k_r = jnp.split(w_in, list(_cumsum(IN_SPLITS))[:-1], axis=1)
    zeros = lambda n: jnp.zeros((D_MODEL, n), w_in.dtype)
    kr_sw = jnp.concatenate([k_r[:, HALF_ROPE:], k_r[:, :HALF_ROPE]], axis=1)
    slot = jnp.concatenate([g_low, zeros(32 - GLA_GATE_RANK), k_r, zeros(32), kr_sw], axis=1)
    w_in_l = jnp.concatenate([gq, gk, gv, g_r, c_q, c_kv, slot], axis=1).astype(BF16)
    assert w_in_l.shape == (D_MODEL, D_IN_PAD)

    w2_l = jnp.zeros((LANES, GLA_QK), F32).at[:GLA_GATE_RANK].set(gate_w2).astype(BF16)

    w_uq_h = w_uq.reshape(MLA_Q_RANK, MLA_HEADS, MLA_QK)
    w_uq_l = _head_layout(w_uq_h[..., :MLA_NOPE], w_uq_h[..., MLA_NOPE:]).reshape(MLA_Q_RANK, HEAD_PAD)
    w_ukv_h = w_ukv.reshape(MLA_KV_RANK, MLA_HEADS, MLA_NOPE + MLA_V)
    w_uk_l = _head_layout(w_ukv_h[..., :MLA_NOPE],
                          jnp.zeros((MLA_KV_RANK, MLA_HEADS, MLA_ROPE), F32)).reshape(MLA_KV_RANK, HEAD_PAD)
    w_uv = w_ukv_h[..., MLA_NOPE:].reshape(MLA_KV_RANK, MLA_WIDTH)
    gq_l = _head_layout(qk_norm_q[:MLA_NOPE], qk_norm_q[MLA_NOPE:])
    gk_l = _head_layout(qk_norm_k[:MLA_NOPE], qk_norm_k[MLA_NOPE:])

    inv_freq = ROPE_THETA ** (-jnp.arange(0, MLA_ROPE, 2, dtype=F32) / MLA_ROPE)
    z32 = jnp.zeros((32,), F32)
    tab = jnp.zeros((SUBLANES, LANES), F32)
    tab = tab.at[ROW_FREQ16].set(jnp.tile(inv_freq, LANES // HALF_ROPE))
    tab = tab.at[ROW_MASK96].set(jnp.concatenate([jnp.ones((96,), F32), z32]))
    tab = tab.at[ROW_KRMASK].set(jnp.concatenate([z32, jnp.ones((32,), F32), z32, jnp.ones((32,), F32)]))
    tab = tab.at[ROW_GQ].set(gq_l).at[ROW_GQ_SW].set(jnp.roll(gq_l, LANES // 2))
    tab = tab.at[ROW_GK].set(gk_l).at[ROW_GK_SW].set(jnp.roll(gk_l, LANES // 2))

    row = lambda w: pl.BlockSpec((tm, w), lambda i: (i, 0))
    out_widths = (GLA_QK, GLA_QK, GLA_WIDTH, GLA_QK, GLA_WIDTH, HEAD_PAD, HEAD_PAD, MLA_WIDTH)
    out_dtypes = (BF16, BF16, BF16, F32, BF16, BF16, BF16, BF16)
    return pl.pallas_call(
        _proj_kernel,
        grid=(t // tm,),
        in_specs=[
            row(D_MODEL), row(1),
            _const_spec((1, D_MODEL)), _const_spec((D_MODEL, D_IN_PAD)),
            _const_spec((LANES, GLA_QK)), _const_spec((1, GLA_QK)),
            _const_spec((1, MLA_Q_RANK)), _const_spec((MLA_Q_RANK, HEAD_PAD)),
            _const_spec((1, MLA_KV_RANK)), _const_spec((MLA_KV_RANK, HEAD_PAD)),
            _const_spec((MLA_KV_RANK, MLA_WIDTH)),
            _const_spec((8, LANES)),
        ],
        out_specs=[row(w) for w in out_widths],
        out_shape=[jax.ShapeDtypeStruct((t, w), d) for w, d in zip(out_widths, out_dtypes)],
        scratch_shapes=[pltpu.VMEM((tm, HEAD_PAD), F32), pltpu.VMEM((tm, HEAD_PAD), F32),
                        pltpu.VMEM((tm, LANES), F32)],
        compiler_params=pltpu.CompilerParams(
            dimension_semantics=("arbitrary",), vmem_limit_bytes=VMEM_LIMIT_BYTES),
        name="proj",
    )(x2, pos2, attn_norm.reshape(1, D_MODEL), w_in_l, w2_l, gate_b.reshape(1, GLA_QK),
      q_norm.reshape(1, MLA_Q_RANK), w_uq_l.astype(BF16), kv_norm.reshape(1, MLA_KV_RANK),
      w_uk_l.astype(BF16), w_uv.astype(BF16), tab)


def _cumsum(sizes):
    total = 0
    for s in sizes:
        total += s
        yield total


def _gla_kernel(q_ref, k_ref, v_ref, la_ref, gr_ref, gon_ref, o_ref, st_ref, kf_ref, b_ref):
    seq = q_ref.shape[1]
    st_ref[...] = jnp.zeros_like(st_ref)
    row = lax.broadcasted_iota(jnp.int32, (CHUNK, CHUNK), 0)
    col = lax.broadcasted_iota(jnp.int32, (CHUNK, CHUNK), 1)
    causal = col <= row
    tril = causal.astype(BF16)
    lane = lax.broadcasted_iota(jnp.int32, (1, LANES), 1)
    lane_q = lax.broadcasted_iota(jnp.int32, (1, GLA_QK), 1)
    head_mask = [(lane // GLA_DK) == e for e in range(2)]
    gon = gon_ref[...]
    nt = (((1,), (1,)), ((), ()))
    tn = (((0,), (0,)), ((), ()))

    def chunk(c, carry):
        r0 = pl.multiple_of(c * CHUNK, CHUNK)
        rows = pl.ds(r0, CHUNK)
        la = la_ref[0, rows, :]
        hi = la.astype(BF16)
        r1 = la - hi.astype(F32)
        mid = r1.astype(BF16)
        lo = (r1 - mid.astype(F32)).astype(BF16)
        b = (jnp.dot(tril, hi, preferred_element_type=F32)
             + jnp.dot(tril, mid, preferred_element_type=F32)
             + jnp.dot(tril, lo, preferred_element_type=F32))
        b_last = b[CHUNK - 1:CHUNK, :]
        q = q_ref[0, rows, :].astype(F32)
        k = k_ref[0, rows, :].astype(F32)
        qt = q * jnp.exp(b)
        kd = k * jnp.exp(b_last - b)
        eb_last = jnp.exp(b_last)
        a_h = [jnp.where(head_mask[h % 2], qt[:, (h // 2) * LANES:(h // 2 + 1) * LANES], 0.0).astype(BF16)
               for h in range(GLA_HEADS)]

        def fast_scores():
            kt = (k * jnp.exp(-b)).astype(BF16)
            return tuple(
                lax.dot_general(a_h[h], kt[:, (h // 2) * LANES:(h // 2 + 1) * LANES], nt,
                                preferred_element_type=F32)
                for h in range(GLA_HEADS))

        def exact_scores():
            kf_ref[...] = k
            b_ref[...] = b

            def key_row(s, sc):
                k_s = kf_ref[pl.ds(s, 1), :]
                b_s = b_ref[pl.ds(s, 1), :]
                prod = q * k_s * jnp.exp(jnp.minimum(b - b_s, 0.0))
                out = []
                for h in range(GLA_HEADS):
                    hm = (lane_q // GLA_DK) == h
                    col_h = jnp.sum(jnp.where(hm, prod, 0.0), axis=-1, keepdims=True)
                    out.append(jnp.where(col == s, col_h, sc[h]))
                return tuple(out)

            zero = jnp.zeros((CHUNK, CHUNK), F32)
            return lax.fori_loop(0, CHUNK, key_row, (zero,) * GLA_HEADS)

        scores = lax.cond(jnp.min(b_last) >= GLA_FAST_MIN_LOG_DECAY, fast_scores, exact_scores)

        for h in range(GLA_HEADS):
            pair = slice((h // 2) * LANES, (h // 2 + 1) * LANES)
            vsl = slice(h * GLA_DV, (h + 1) * GLA_DV)
            sc = jnp.where(causal, scores[h], 0.0).astype(BF16)
            v_h = v_ref[0, rows, vsl]
            st = st_ref[h]
            o = (lax.dot_general(a_h[h], st.astype(BF16), nt, preferred_element_type=F32)
                 + jnp.dot(sc, v_h, preferred_element_type=F32))
            o = o * _rms_scale(o, GLA_DV) * gon
            o_ref[0, rows, vsl] = (o * gr_ref[0, rows, vsl].astype(F32)).astype(BF16)
            kd_h = jnp.where(head_mask[h % 2], kd[:, pair], 0.0).astype(BF16)
            upd = lax.dot_general(v_h, kd_h, tn, preferred_element_type=F32)
            st_ref[h] = st * eb_last[:, pair] + upd
        return carry

    lax.fori_loop(0, seq // CHUNK, chunk, 0)


def _gla(qg, kg, vg, la, gr, out_norm):
    b, s, _ = qg.shape
    blk = lambda w: pl.BlockSpec((1, s, w), lambda i: (i, 0, 0))
    return pl.pallas_call(
        _gla_kernel,
        grid=(b,),
        in_specs=[blk(GLA_QK), blk(GLA_QK), blk(GLA_WIDTH), blk(GLA_QK), blk(GLA_WIDTH),
                  _const_spec((1, GLA_DV))],
        out_specs=blk(GLA_WIDTH),
        out_shape=jax.ShapeDtypeStruct((b, s, GLA_WIDTH), BF16),
        scratch_shapes=[pltpu.VMEM((GLA_HEADS, GLA_DV, LANES), F32),
                        pltpu.VMEM((CHUNK, GLA_QK), F32),
                        pltpu.VMEM((CHUNK, GLA_QK), F32)],
        compiler_params=pltpu.CompilerParams(
            dimension_semantics=("arbitrary",), vmem_limit_bytes=VMEM_LIMIT_BYTES),
        name="gla",
    )(qg, kg, vg, la, gr, out_norm.reshape(1, GLA_DV))


def _mla_tiles(n):
    pairs = [(i, i) for i in range(n)] + [(i, j) for i in range(1, n) for j in range(i)]
    return [i for i, _ in pairs], [j for _, j in pairs]


def _mla_kernel(ti_ref, tj_ref, q_ref, k_ref, v_ref, o_ref, s_ref, m_ref, acc_ref, v1_ref):
    seq = q_ref.shape[1]
    tq = TQ
    n = seq // tq
    n_tiles = n * (n + 1) // 2
    nt = (((1,), (1,)), ((), ()))
    heads = range(2)
    lane = lax.broadcasted_iota(jnp.int32, (1, LANES), 1)
    own = [(lane // MLA_V) == e for e in heads]

    def rows(t):
        return pl.ds(pl.multiple_of(t * tq, tq), tq)

    def logits(e, i, j):
        hsl = slice(e * LANES, (e + 1) * LANES)
        return lax.dot_general(q_ref[0, rows(i), hsl], k_ref[0, rows(j), hsl], nt,
                               preferred_element_type=F32)

    def lane_fold_max(a):
        out = a[:, :LANES]
        for c in range(1, tq // LANES):
            out = jnp.maximum(out, a[:, c * LANES:(c + 1) * LANES])
        return out

    v = v_ref[0]
    for e in heads:
        v1_ref[e] = jnp.where(own[e], v, 1.0).astype(BF16)

    def diag(i, carry):
        row_chunk = lax.broadcasted_iota(jnp.int32, (tq, tq), 0) // CHUNK
        col_chunk = lax.broadcasted_iota(jnp.int32, (tq, tq), 1) // CHUNK
        for e in heads:
            s = jnp.where(col_chunk <= row_chunk, logits(e, i, i), MASKED_LOGIT)
            s_ref[e, i] = s
            m_ref[e, i] = lane_fold_max(s)
            acc_ref[e, i] = jnp.zeros((tq, LANES), F32)
        return carry

    lax.fori_loop(0, n, diag, 0, unroll=2)

    def pass1(t, carry):
        i, j = ti_ref[t], tj_ref[t]
        for e in heads:
            s = logits(e, i, j)
            s_ref[e, t] = s
            m_ref[e, i] = jnp.maximum(m_ref[e, i], lane_fold_max(s))
        return carry

    lax.fori_loop(n, n_tiles, pass1, 0, unroll=2)

    def row_max(i, carry):
        for e in heads:
            m_ref[e, i] = jnp.broadcast_to(jnp.max(m_ref[e, i], axis=-1, keepdims=True), (tq, LANES))
        return carry

    lax.fori_loop(0, n, row_max, 0)

    def pass2(t, carry):
        i, j = ti_ref[t], tj_ref[t]
        for e in heads:
            s = s_ref[e, t]
            m = m_ref[e, i]
            p = jnp.concatenate(
                [jnp.exp(s[:, c * LANES:(c + 1) * LANES] - m) for c in range(tq // LANES)], axis=1)
            acc_ref[e, i] += jnp.dot(p.astype(BF16), v1_ref[e, rows(j), :], preferred_element_type=F32)
        return carry

    lax.fori_loop(0, n_tiles, pass2, 0, unroll=2)

    def finish(i, carry):
        out = []
        for e in heads:
            acc = acc_ref[e, i]
            denom = jnp.sum(jnp.where(lane == (1 - e) * MLA_V, acc, 0.0), axis=-1, keepdims=True)
            out.append(acc / denom)
        o_ref[0, rows(i), :] = jnp.where(own[0], out[0], out[1]).astype(BF16)
        return carry

    lax.fori_loop(0, n, finish, 0)


def _mla(q, k, v):
    b, s, _ = q.shape
    tq = TQ
    assert s % (2 * tq) == 0
    n = s // tq
    ti, tj = _mla_tiles(n)
    blk = lambda w: pl.BlockSpec((1, s, w), lambda i, p, *_: (i, 0, p))
    return pl.pallas_call(
        _mla_kernel,
        grid_spec=pltpu.PrefetchScalarGridSpec(
            num_scalar_prefetch=2,
            grid=(b, MLA_HEADS // 2),
            in_specs=[blk(2 * LANES), blk(2 * LANES), blk(LANES)],
            out_specs=blk(LANES),
            scratch_shapes=[pltpu.VMEM((2, len(ti), tq, tq), F32),
                            pltpu.VMEM((2, n, tq, LANES), F32),
                            pltpu.VMEM((2, n, tq, LANES), F32),
                            pltpu.VMEM((2, s, LANES), BF16)]),
        out_shape=jax.ShapeDtypeStruct((b, s, MLA_WIDTH), BF16),
        compiler_params=pltpu.CompilerParams(
            dimension_semantics=("arbitrary", "arbitrary"),
            vmem_limit_bytes=VMEM_LIMIT_BYTES),
        name="mla",
    )(jnp.asarray(ti, jnp.int32), jnp.asarray(tj, jnp.int32), q, k, v)


def _tail_kernel(x_ref, og_ref, om_ref, p_ref, w_out_ref, g_mlp_ref, w_up_ref, w_down_ref,
                 g_ple_ref, w_gate_ref, b_gate_ref, w_pp_ref, o_ref):
    h = (x_ref[...]
         + jnp.dot(og_ref[...], w_out_ref[:GLA_WIDTH, :], preferred_element_type=F32)
         + jnp.dot(om_ref[...], w_out_ref[GLA_WIDTH:, :], preferred_element_type=F32))
    m = (h * _rms_scale(h, D_MODEL) * g_mlp_ref[...]).astype(BF16)
    mlp = None
    for c in range(D_FF // FF_CHUNK):
        cols = slice(c * FF_CHUNK, (c + 1) * FF_CHUNK)
        u = jnp.maximum(jnp.dot(m, w_up_ref[:, cols], preferred_element_type=F32), 0.0)
        d = jnp.dot((u * u).astype(BF16), w_down_ref[cols, :], preferred_element_type=F32)
        mlp = d if mlp is None else mlp + d
    h = h + mlp
    g = (h * _rms_scale(h, D_MODEL) * g_ple_ref[...]).astype(BF16)
    gate = jax.nn.sigmoid(jnp.dot(g, w_gate_ref[...], preferred_element_type=F32) + b_gate_ref[...])
    pp = jnp.dot(p_ref[...].astype(BF16), w_pp_ref[...], preferred_element_type=F32)
    o_ref[...] = h + pp * gate


def _tail(x2, og, om, p2, w_out, mlp_norm, w_up, w_down, ple_norm, w_gate, b_gate, w_pp):
    t = x2.shape[0]
    tm = TM_TAIL
    assert t % tm == 0
    row = lambda w: pl.BlockSpec((tm, w), lambda i: (i, 0))
    return pl.pallas_call(
        _tail_kernel,
        grid=(t // tm,),
        in_specs=[row(D_MODEL), row(GLA_WIDTH), row(MLA_WIDTH), row(PLE_DIM),
                  _const_spec((D_MODEL, D_MODEL)), _const_spec((1, D_MODEL)),
                  _const_spec((D_MODEL, D_FF)), _const_spec((D_FF, D_MODEL)),
                  _const_spec((1, D_MODEL)), _const_spec((D_MODEL, D_MODEL)),
                  _const_spec((1, D_MODEL)), _const_spec((PLE_DIM, D_MODEL))],
        out_specs=row(D_MODEL),
        out_shape=jax.ShapeDtypeStruct((t, D_MODEL), F32),
        compiler_params=pltpu.CompilerParams(
            dimension_semantics=("arbitrary",), vmem_limit_bytes=VMEM_LIMIT_BYTES),
        name="tail",
    )(x2, og, om, p2, w_out.astype(BF16), mlp_norm.reshape(1, D_MODEL), w_up.astype(BF16),
      w_down.astype(BF16), ple_norm.reshape(1, D_MODEL), w_gate.astype(BF16),
      b_gate.reshape(1, D_MODEL), w_pp.astype(BF16))


def kernel(x, p, positions, attn_norm, w_in, gla_gate_w2, gla_gate_b, gla_out_norm, mla_q_norm,
           mla_w_uq, mla_kv_norm, mla_w_ukv, qk_norm_q, qk_norm_k, w_out, mlp_norm, w_mlp_up,
           w_mlp_down, ple_norm, w_ple_gate, b_ple_gate, w_ple_proj):
    b, s, d = x.shape
    depth = w_in.shape[0]
    t = b * s
    pos2 = positions.reshape(t, 1)
    h = x.reshape(t, d)
    for i in range(depth):
        qg, kg, vg, la, gr, q, k, v = _projection(
            h, pos2, attn_norm[i], w_in[i], gla_gate_w2[i], gla_gate_b[i], mla_q_norm[i],
            mla_w_uq[i], mla_kv_norm[i], mla_w_ukv[i], qk_norm_q[i], qk_norm_k[i])
        seq = lambda a: a.reshape(b, s, a.shape[-1])
        og = _gla(seq(qg), seq(kg), seq(vg), seq(la), seq(gr), gla_out_norm[i])
        om = _mla(seq(q), seq(k), seq(v))
        h = _tail(h, og.reshape(t, GLA_WIDTH), om.reshape(t, MLA_WIDTH), p[i].reshape(t, PLE_DIM),
                  w_out[i], mlp_norm[i], w_mlp_up[i], w_mlp_down[i], ple_norm[i], w_ple_gate[i],
                  b_ple_gate[i], w_ple_proj[i])
    return h.reshape(b, s, d)
```

```python
import functools
import math

import jax
import jax.numpy as jnp
from jax import lax
from jax.experimental import pallas as pl
from jax.experimental.pallas import tpu as pltpu

F32 = jnp.float32
BF16 = jnp.bfloat16

D_MODEL = 1024
CHUNK = 64
PLE_DIM = 256
D_FF = 4 * D_MODEL
EPS = 1e-6
GLA_HEADS = 4
GLA_DK = 64
GLA_DV = 128
GLA_GATE_RANK = 16
GLA_TAU = 16.0
GLA_QK = GLA_HEADS * GLA_DK
GLA_WIDTH = GLA_HEADS * GLA_DV
MLA_HEADS = 8
MLA_NOPE = 64
MLA_ROPE = 32
MLA_V = 64
MLA_QK = MLA_NOPE + MLA_ROPE
MLA_Q_RANK = 256
MLA_KV_RANK = 128
MLA_WIDTH = MLA_HEADS * MLA_V
ROPE_THETA = 10000.0
IN_SPLITS = (GLA_QK, GLA_QK, GLA_WIDTH, GLA_GATE_RANK, GLA_WIDTH, MLA_Q_RANK, MLA_KV_RANK, MLA_ROPE)

LANES = 128
SUBLANES = 8
HALF_ROPE = MLA_ROPE // 2
HEAD_PAD = MLA_HEADS * LANES
D_IN_PAD = 2048
VMEM_LIMIT_BYTES = 56 * 1024 * 1024

TM_PROJ = 512
TM_TAIL = 512
FF_CHUNK = 1024
TQ = 256
ROPE_ROWS = 64
GLA_SEQS_PER_STEP = (4, 2, 1)
GLA_ROWS_PER_STEP = 512
GLA_FAST_MIN_LOG_DECAY = -80.0
MASKED_LOGIT = -1e30

ROW_FREQ16, ROW_MASK96, ROW_KRMASK, ROW_GQ, ROW_GQ_SW, ROW_GK, ROW_GK_SW = range(7)


def _const_spec(shape):
    return pl.BlockSpec(shape, lambda *_: (0,) * len(shape), pipeline_mode=pl.Buffered(1))


def _rms_scale(v, n):
    return lax.rsqrt(jnp.sum(v * v, axis=-1, keepdims=True) * (1.0 / n) + EPS)


def _proj_kernel(x_ref, pos_ref, g_attn_ref, w_in_ref, w2_ref, b2_ref, gqn_ref, w_uq_ref,
                 gkvn_ref, w_uk_ref, w_uv_ref, tab_ref,
                 qg_ref, kg_ref, vg_ref, la_ref, gr_ref, q_ref, k_ref, v_ref,
                 qs_ref, ks_ref, slot_ref):
    x = x_ref[...]
    xn = (x * _rms_scale(x, D_MODEL) * g_attn_ref[...]).astype(BF16)

    def proj(lo, hi):
        return jnp.dot(xn, w_in_ref[:, lo:hi], preferred_element_type=F32)

    zqk = proj(0, 2 * GLA_QK)
    qg_ref[...] = (zqk[:, :GLA_QK] * (GLA_DK ** -0.5)).astype(BF16)
    kg_ref[...] = zqk[:, GLA_QK:].astype(BF16)
    vg_ref[...] = proj(512, 1024).astype(BF16)
    gr = proj(1024, 1536)
    gr_ref[...] = (gr * jax.nn.sigmoid(gr)).astype(BF16)
    zc = proj(1536, 2048)
    cq = zc[:, :MLA_Q_RANK]
    ckv = zc[:, MLA_Q_RANK:MLA_Q_RANK + MLA_KV_RANK]
    slot = zc[:, MLA_Q_RANK + MLA_KV_RANK:]
    slot_ref[...] = slot

    xg = jnp.dot(slot.astype(BF16), w2_ref[...], preferred_element_type=F32) + b2_ref[...]
    la_ref[...] = (jnp.minimum(xg, 0.0) - jnp.log(1.0 + jnp.exp(-jnp.abs(xg)))) * (1.0 / GLA_TAU)

    cqn = (cq * _rms_scale(cq, MLA_Q_RANK) * gqn_ref[...]).astype(BF16)
    qs_ref[...] = jnp.dot(cqn, w_uq_ref[...], preferred_element_type=F32)
    ckvn = (ckv * _rms_scale(ckv, MLA_KV_RANK) * gkvn_ref[...]).astype(BF16)
    ks_ref[...] = jnp.dot(ckvn, w_uk_ref[...], preferred_element_type=F32)
    v_ref[...] = jnp.dot(ckvn, w_uv_ref[...], preferred_element_type=F32).astype(BF16)

    tab = lambda r: tab_ref[r:r + 1, :]
    mask96, krmask = tab(ROW_MASK96), tab(ROW_KRMASK)
    gq, gq_sw = tab(ROW_GQ), tab(ROW_GQ_SW)
    gk, gk_sw = tab(ROW_GK) * math.sqrt(MLA_QK), tab(ROW_GK_SW) * math.sqrt(MLA_QK)
    half = LANES // 2

    lane = lax.broadcasted_iota(jnp.int32, (1, LANES), 1)
    groups = ROPE_ROWS // SUBLANES
    assert groups * HALF_ROPE == LANES
    x1_lanes = (lane >= 32) & (lane < 32 + HALF_ROPE)
    x2_lanes = (lane >= 32 + HALF_ROPE) & (lane < 64)

    def rope_tables(rows):
        pos = jnp.broadcast_to(pos_ref[rows, :].astype(F32), (ROPE_ROWS, LANES))
        packed = jnp.zeros((SUBLANES, LANES), F32)
        for a in range(groups):
            packed = jnp.where(lane // HALF_ROPE == a, pos[a * SUBLANES:(a + 1) * SUBLANES], packed)
        ang = packed * tab(ROW_FREQ16)
        cos_p, sin_p = jnp.cos(ang), jnp.sin(ang)
        cos, sin = [], []
        for a in range(groups):
            s1 = (32 - HALF_ROPE * a) % LANES
            s2 = (32 + HALF_ROPE - HALF_ROPE * a) % LANES
            c1, c2 = pltpu.roll(cos_p, s1, 1), pltpu.roll(cos_p, s2, 1)
            n1, n2 = pltpu.roll(sin_p, s1, 1), pltpu.roll(sin_p, s2, 1)
            cos.append(jnp.where(x1_lanes, c1, jnp.where(x2_lanes, c2, mask96)))
            sin.append(jnp.where(x1_lanes, -n1, jnp.where(x2_lanes, n2, 0.0)))
        return jnp.concatenate(cos, axis=0), jnp.concatenate(sin, axis=0)

    sum96 = (lax.broadcasted_iota(jnp.int32, (LANES, LANES), 0) < MLA_QK).astype(BF16)

    def head_rms(v):
        ssq = jnp.dot((v * v).astype(BF16), sum96, preferred_element_type=F32)
        return lax.rsqrt(ssq + MLA_QK * EPS)

    def head_rows(rb, carry):
        rows = pl.ds(pl.multiple_of(rb * ROPE_ROWS, ROPE_ROWS), ROPE_ROWS)
        cos, sin = rope_tables(rows)
        kr = slot_ref[rows, :] * krmask
        kr_rot = pltpu.roll(kr, half, 1) * (sin * gk_sw)
        cos_q, sin_q, cos_k = cos * gq, sin * gq_sw, cos * gk
        for h in range(MLA_HEADS):
            sl = slice(h * LANES, (h + 1) * LANES)
            qh = qs_ref[rows, sl]
            q_ref[rows, sl] = ((qh * cos_q + pltpu.roll(qh, half, 1) * sin_q) * head_rms(qh)).astype(BF16)
            kh = ks_ref[rows, sl] + kr
            k_ref[rows, sl] = ((kh * cos_k + kr_rot) * head_rms(kh)).astype(BF16)
        return carry

    lax.fori_loop(0, x_ref.shape[0] // ROPE_ROWS, head_rows, 0, unroll=2)


def _head_layout(nope, rope):
    x1, x2 = rope[..., :HALF_ROPE], rope[..., HALF_ROPE:]
    return jnp.concatenate([nope[..., :32], x1, x2, nope[..., 32:], x2, x1], axis=-1)


def _projection(x2, pos2, attn_norm, w_in, gate_w2, gate_b, q_norm, w_uq, kv_norm, w_ukv,
                qk_norm_q, qk_norm_k):
    t = x2.shape[0]
    tm = TM_PROJ
    assert t % tm == 0

    gq, gk, gv, g_low, g_r, c_q, c_kv, k_r = jnp.split(w_in, list(_cumsum(IN_SPLITS))[:-1], axis=1)
    zeros = lambda n: jnp.zeros((D_MODEL, n), w_in.dtype)
    kr_sw = jnp.concatenate([k_r[:, HALF_ROPE:], k_r[:, :HALF_ROPE]], axis=1)
    slot = jnp.concatenate([g_low, zeros(32 - GLA_GATE_RANK), k_r, zeros(32), kr_sw], axis=1)
    w_in_l = jnp.concatenate([gq, gk, gv, g_r, c_q, c_kv, slot], axis=1).astype(BF16)
    assert w_in_l.shape == (D_MODEL, D_IN_PAD)

    w2_l = jnp.zeros((LANES, GLA_QK), F32).at[:GLA_GATE_RANK].set(gate_w2).astype(BF16)

    w_uq_h = w_uq.reshape(MLA_Q_RANK, MLA_HEADS, MLA_QK)
    w_uq_l = _head_layout(w_uq_h[..., :MLA_NOPE], w_uq_h[..., MLA_NOPE:]).reshape(MLA_Q_RANK, HEAD_PAD)
    w_ukv_h = w_ukv.reshape(MLA_KV_RANK, MLA_HEADS, MLA_NOPE + MLA_V)
    w_uk_l = _head_layout(w_ukv_h[..., :MLA_NOPE],
                          jnp.zeros((MLA_KV_RANK, MLA_HEADS, MLA_ROPE), F32)).reshape(MLA_KV_RANK, HEAD_PAD)
    w_uv = w_ukv_h[..., MLA_NOPE:].reshape(MLA_KV_RANK, MLA_WIDTH)
    gq_l = _head_layout(qk_norm_q[:MLA_NOPE], qk_norm_q[MLA_NOPE:])
    gk_l = _head_layout(qk_norm_k[:MLA_NOPE], qk_norm_k[MLA_NOPE:])

    inv_freq = ROPE_THETA ** (-jnp.arange(0, MLA_ROPE, 2, dtype=F32) / MLA_ROPE)
    z32 = jnp.zeros((32,), F32)
    tab = jnp.zeros((SUBLANES, LANES), F32)
    tab = tab.at[ROW_FREQ16].set(jnp.tile(inv_freq, LANES // HALF_ROPE))
    tab = tab.at[ROW_MASK96].set(jnp.concatenate([jnp.ones((96,), F32), z32]))
    tab = tab.at[ROW_KRMASK].set(jnp.concatenate([z32, jnp.ones((32,), F32), z32, jnp.ones((32,), F32)]))
    tab = tab.at[ROW_GQ].set(gq_l).at[ROW_GQ_SW].set(jnp.roll(gq_l, LANES // 2))
    tab = tab.at[ROW_GK].set(gk_l).at[ROW_GK_SW].set(jnp.roll(gk_l, LANES // 2))

    row = lambda w: pl.BlockSpec((tm, w), lambda i: (i, 0))
    out_widths = (GLA_QK, GLA_QK, GLA_WIDTH, GLA_QK, GLA_WIDTH, HEAD_PAD, HEAD_PAD, MLA_WIDTH)
    out_dtypes = (BF16, BF16, BF16, F32, BF16, BF16, BF16, BF16)
    return pl.pallas_call(
        _proj_kernel,
        grid=(t // tm,),
        in_specs=[
            row(D_MODEL), row(1),
            _const_spec((1, D_MODEL)), _const_spec((D_MODEL, D_IN_PAD)),
            _const_spec((LANES, GLA_QK)), _const_spec((1, GLA_QK)),
            _const_spec((1, MLA_Q_RANK)), _const_spec((MLA_Q_RANK, HEAD_PAD)),
            _const_spec((1, MLA_KV_RANK)), _const_spec((MLA_KV_RANK, HEAD_PAD)),
            _const_spec((MLA_KV_RANK, MLA_WIDTH)),
            _const_spec((8, LANES)),
        ],
        out_specs=[row(w) for w in out_widths],
        out_shape=[jax.ShapeDtypeStruct((t, w), d) for w, d in zip(out_widths, out_dtypes)],
        scratch_shapes=[pltpu.VMEM((tm, HEAD_PAD), F32), pltpu.VMEM((tm, HEAD_PAD), F32),
                        pltpu.VMEM((tm, LANES), F32)],
        compiler_params=pltpu.CompilerParams(
            dimension_semantics=("arbitrary",), vmem_limit_bytes=VMEM_LIMIT_BYTES),
        name="proj",
    )(x2, pos2, attn_norm.reshape(1, D_MODEL), w_in_l, w2_l, gate_b.reshape(1, GLA_QK),
      q_norm.reshape(1, MLA_Q_RANK), w_uq_l.astype(BF16), kv_norm.reshape(1, MLA_KV_RANK),
      w_uk_l.astype(BF16), w_uv.astype(BF16), tab)


def _cumsum(sizes):
    total = 0
    for s in sizes:
        total += s
        yield total


def _gla_kernel(q_ref, k_ref, v_ref, la_ref, gr_ref, gon_ref, o_ref, st_ref, kf_ref, b_ref):
    nseq, rows_per_step = q_ref.shape[0], q_ref.shape[1]

    @pl.when(pl.program_id(1) == 0)
    def _():
        st_ref[...] = jnp.zeros_like(st_ref)

    row = lax.broadcasted_iota(jnp.int32, (CHUNK, CHUNK), 0)
    col = lax.broadcasted_iota(jnp.int32, (CHUNK, CHUNK), 1)
    causal = col <= row
    tril = causal.astype(BF16)
    lane = lax.broadcasted_iota(jnp.int32, (1, LANES), 1)
    lane_q = lax.broadcasted_iota(jnp.int32, (1, GLA_QK), 1)
    head_mask = [(lane // GLA_DK) == e for e in range(2)]
    gon = gon_ref[...]
    nt = (((1,), (1,)), ((), ()))
    tn = (((0,), (0,)), ((), ()))

    def exact_scores(q, k, b):
        kf_ref[...] = k
        b_ref[...] = b

        def key_row(s, sc):
            k_s = kf_ref[pl.ds(s, 1), :]
            b_s = b_ref[pl.ds(s, 1), :]
            prod = q * k_s * jnp.exp(jnp.minimum(b - b_s, 0.0))
            out = []
            for h in range(GLA_HEADS):
                hm = (lane_q // GLA_DK) == h
                col_h = jnp.sum(jnp.where(hm, prod, 0.0), axis=-1, keepdims=True)
                out.append(jnp.where(col == s, col_h, sc[h]))
            return tuple(out)

        zero = jnp.zeros((CHUNK, CHUNK), F32)
        return lax.fori_loop(0, CHUNK, key_row, (zero,) * GLA_HEADS)

    pair = lambda h: slice((h // 2) * LANES, (h // 2 + 1) * LANES)
    vsl = lambda h: slice(h * GLA_DV, (h + 1) * GLA_DV)
    heads = range(GLA_HEADS)

    def chunk(seqs, c, factorised):
        rows = pl.ds(pl.multiple_of(c * CHUNK, CHUNK), CHUNK)
        pieces = []
        for n in seqs:
            la = la_ref[n, rows, :]
            hi = la.astype(BF16)
            r1 = la - hi.astype(F32)
            mid = r1.astype(BF16)
            pieces += [hi, mid, (r1 - mid.astype(F32)).astype(BF16)]
        cum = jnp.dot(tril, jnp.concatenate(pieces, axis=1), preferred_element_type=F32)
        q, k, b, a_h, kd, eb_last, scores = {}, {}, {}, {}, {}, {}, {}
        for i, n in enumerate(seqs):
            parts = [cum[:, (3 * i + j) * GLA_QK:(3 * i + j + 1) * GLA_QK] for j in range(3)]
            b[n] = parts[0] + parts[1] + parts[2]
            b_last = b[n][CHUNK - 1:CHUNK, :]
            q[n] = q_ref[n, rows, :].astype(F32)
            k[n] = k_ref[n, rows, :].astype(F32)
            qt = q[n] * jnp.exp(b[n])
            kd[n] = k[n] * jnp.exp(b_last - b[n])
            eb_last[n] = jnp.exp(b_last)
            a_h[n] = [jnp.where(head_mask[h % 2], qt[:, pair(h)], 0.0).astype(BF16) for h in heads]
        for n in seqs:
            if factorised:
                kt = (k[n] * jnp.exp(-b[n])).astype(BF16)
                scores[n] = [lax.dot_general(a_h[n][h], kt[:, pair(h)], nt, preferred_element_type=F32)
                             for h in heads]
            else:
                scores[n] = exact_scores(q[n], k[n], b[n])
        v, st, o = {}, {}, {}
        for n in seqs:
            for h in heads:
                sc = jnp.where(causal, scores[n][h], 0.0).astype(BF16)
                v[n, h] = v_ref[n, rows, vsl(h)]
                st[n, h] = st_ref[n, h]
                o[n, h] = (lax.dot_general(a_h[n][h], st[n, h].astype(BF16), nt, preferred_element_type=F32)
                           + jnp.dot(sc, v[n, h], preferred_element_type=F32))
        for n in seqs:
            for h in heads:
                kd_h = jnp.where(head_mask[h % 2], kd[n][:, pair(h)], 0.0).astype(BF16)
                upd = lax.dot_general(v[n, h], kd_h, tn, preferred_element_type=F32)
                st_ref[n, h] = st[n, h] * eb_last[n][:, pair(h)] + upd
        for n in seqs:
            for h in heads:
                on = o[n, h] * _rms_scale(o[n, h], GLA_DV) * gon
                o_ref[n, rows, vsl(h)] = (on * gr_ref[n, rows, vsl(h)].astype(F32)).astype(BF16)

    def factorised_chunks():
        def body(c, carry):
            chunk(range(nseq), c, True)
            return carry
        lax.fori_loop(0, rows_per_step // CHUNK, body, 0)

    def exact_chunks():
        def body(c, carry):
            for n in range(nseq):
                chunk([n], c, False)
            return carry
        lax.fori_loop(0, rows_per_step // CHUNK, body, 0)

    factorisable = jnp.min(la_ref[...]) >= GLA_FAST_MIN_LOG_DECAY / CHUNK
    lax.cond(factorisable, factorised_chunks, exact_chunks)


def _gla(qg, kg, vg, la, gr, out_norm):
    b, s, _ = qg.shape
    nseq = max(n for n in GLA_SEQS_PER_STEP if b % n == 0)
    rows = min(GLA_ROWS_PER_STEP, s)
    assert s % rows == 0 and rows % CHUNK == 0
    blk = lambda w: pl.BlockSpec((nseq, rows, w), lambda i, j: (i, j, 0))
    return pl.pallas_call(
        _gla_kernel,
        grid=(b // nseq, s // rows),
        in_specs=[blk(GLA_QK), blk(GLA_QK), blk(GLA_WIDTH), blk(GLA_QK), blk(GLA_WIDTH),
                  _const_spec((1, GLA_DV))],
        out_specs=blk(GLA_WIDTH),
        out_shape=jax.ShapeDtypeStruct((b, s, GLA_WIDTH), BF16),
        scratch_shapes=[pltpu.VMEM((nseq, GLA_HEADS, GLA_DV, LANES), F32),
                        pltpu.VMEM((CHUNK, GLA_QK), F32),
                        pltpu.VMEM((CHUNK, GLA_QK), F32)],
        compiler_params=pltpu.CompilerParams(
            dimension_semantics=("arbitrary", "arbitrary"), vmem_limit_bytes=VMEM_LIMIT_BYTES),
        name="gla",
    )(qg, kg, vg, la, gr, out_norm.reshape(1, GLA_DV))


def _mla_tiles(n):
    pairs = [(i, i) for i in range(n)] + [(i, j) for i in range(1, n) for j in range(i)]
    return [i for i, _ in pairs], [j for _, j in pairs]


def _mla_kernel(ti_ref, tj_ref, q_ref, k_ref, v_ref, o_ref, s_ref, m_ref, acc_ref, v1_ref):
    seq = q_ref.shape[1]
    tq = TQ
    n = seq // tq
    n_tiles = n * (n + 1) // 2
    nt = (((1,), (1,)), ((), ()))
    heads = range(2)
    lane = lax.broadcasted_iota(jnp.int32, (1, LANES), 1)
    own = [(lane // MLA_V) == e for e in heads]

    def rows(t):
        return pl.ds(pl.multiple_of(t * tq, tq), tq)

    def logits(e, i, j):
        hsl = slice(e * LANES, (e + 1) * LANES)
        return lax.dot_general(q_ref[0, rows(i), hsl], k_ref[0, rows(j), hsl], nt,
                               preferred_element_type=F32)

    def lane_fold_max(a):
        out = a[:, :LANES]
        for c in range(1, tq // LANES):
            out = jnp.maximum(out, a[:, c * LANES:(c + 1) * LANES])
        return out

    v = v_ref[0]
    for e in heads:
        v1_ref[e] = jnp.where(own[e], v, 1.0).astype(BF16)

    def diag(i, carry):
        row_chunk = lax.broadcasted_iota(jnp.int32, (tq, tq), 0) // CHUNK
        col_chunk = lax.broadcasted_iota(jnp.int32, (tq, tq), 1) // CHUNK
        for e in heads:
            s = jnp.where(col_chunk <= row_chunk, logits(e, i, i), MASKED_LOGIT)
            s_ref[e, i] = s
            m_ref[e, i] = lane_fold_max(s)
            acc_ref[e, i] = jnp.zeros((tq, LANES), F32)
        return carry

    lax.fori_loop(0, n, diag, 0, unroll=2)

    def pass1(t, carry):
        i, j = ti_ref[t], tj_ref[t]
        for e in heads:
            s = logits(e, i, j)
            s_ref[e, t] = s
            m_ref[e, i] = jnp.maximum(m_ref[e, i], lane_fold_max(s))
        return carry

    lax.fori_loop(n, n_tiles, pass1, 0, unroll=2)

    def row_max(i, carry):
        for e in heads:
            m_ref[e, i] = jnp.broadcast_to(jnp.max(m_ref[e, i], axis=-1, keepdims=True), (tq, LANES))
        return carry

    lax.fori_loop(0, n, row_max, 0)

    def pass2(t, carry):
        i, j = ti_ref[t], tj_ref[t]
        for e in heads:
            s = s_ref[e, t]
            m = m_ref[e, i]
            p = jnp.concatenate(
                [jnp.exp(s[:, c * LANES:(c + 1) * LANES] - m) for c in range(tq // LANES)], axis=1)
            acc_ref[e, i] += jnp.dot(p.astype(BF16), v1_ref[e, rows(j), :], preferred_element_type=F32)
        return carry

    lax.fori_loop(0, n_tiles, pass2, 0, unroll=2)

    def finish(i, carry):
        out = []
        for e in heads:
            acc = acc_ref[e, i]
            denom = jnp.sum(jnp.where(lane == (1 - e) * MLA_V, acc, 0.0), axis=-1, keepdims=True)
            out.append(acc / denom)
        o_ref[0, rows(i), :] = jnp.where(own[0], out[0], out[1]).astype(BF16)
        return carry

    lax.fori_loop(0, n, finish, 0)


def _mla(q, k, v):
    b, s, _ = q.shape
    tq = TQ
    assert s % (2 * tq) == 0
    n = s // tq
    ti, tj = _mla_tiles(n)
    blk = lambda w: pl.BlockSpec((1, s, w), lambda i, p, *_: (i, 0, p))
    return pl.pallas_call(
        _mla_kernel,
        grid_spec=pltpu.PrefetchScalarGridSpec(
            num_scalar_prefetch=2,
            grid=(b, MLA_HEADS // 2),
            in_specs=[blk(2 * LANES), blk(2 * LANES), blk(LANES)],
            out_specs=blk(LANES),
            scratch_shapes=[pltpu.VMEM((2, len(ti), tq, tq), F32),
                            pltpu.VMEM((2, n, tq, LANES), F32),
                            pltpu.VMEM((2, n, tq, LANES), F32),
                            pltpu.VMEM((2, s, LANES), BF16)]),
        out_shape=jax.ShapeDtypeStruct((b, s, MLA_WIDTH), BF16),
        compiler_params=pltpu.CompilerParams(
            dimension_semantics=("arbitrary", "arbitrary"),
            vmem_limit_bytes=VMEM_LIMIT_BYTES),
        name="mla",
    )(jnp.asarray(ti, jnp.int32), jnp.asarray(tj, jnp.int32), q, k, v)


def _tail_kernel(x_ref, og_ref, om_ref, p_ref, w_out_ref, g_mlp_ref, w_up_ref, w_down_ref,
                 g_ple_ref, w_gate_ref, b_gate_ref, w_pp_ref, o_ref):
    h = (x_ref[...]
         + jnp.dot(og_ref[...], w_out_ref[:GLA_WIDTH, :], preferred_element_type=F32)
         + jnp.dot(om_ref[...], w_out_ref[GLA_WIDTH:, :], preferred_element_type=F32))
    m = (h * _rms_scale(h, D_MODEL) * g_mlp_ref[...]).astype(BF16)
    mlp = None
    for c in range(D_FF // FF_CHUNK):
        cols = slice(c * FF_CHUNK, (c + 1) * FF_CHUNK)
        u = jnp.maximum(jnp.dot(m, w_up_ref[:, cols], preferred_element_type=F32), 0.0)
        d = jnp.dot((u * u).astype(BF16), w_down_ref[cols, :], preferred_element_type=F32)
        mlp = d if mlp is None else mlp + d
    h = h + mlp
    g = (h * _rms_scale(h, D_MODEL) * g_ple_ref[...]).astype(BF16)
    gate = jax.nn.sigmoid(jnp.dot(g, w_gate_ref[...], preferred_element_type=F32) + b_gate_ref[...])
    pp = jnp.dot(p_ref[...].astype(BF16), w_pp_ref[...], preferred_element_type=F32)
    o_ref[...] = h + pp * gate


def _tail(x2, og, om, p2, w_out, mlp_norm, w_up, w_down, ple_norm, w_gate, b_gate, w_pp):
    t = x2.shape[0]
    tm = TM_TAIL
    assert t % tm == 0
    row = lambda w: pl.BlockSpec((tm, w), lambda i: (i, 0))
    return pl.pallas_call(
        _tail_kernel,
        grid=(t // tm,),
        in_specs=[row(D_MODEL), row(GLA_WIDTH), row(MLA_WIDTH), row(PLE_DIM),
                  _const_spec((D_MODEL, D_MODEL)), _const_spec((1, D_MODEL)),
                  _const_spec((D_MODEL, D_FF)), _const_spec((D_FF, D_MODEL)),
                  _const_spec((1, D_MODEL)), _const_spec((D_MODEL, D_MODEL)),
                  _const_spec((1, D_MODEL)), _const_spec((PLE_DIM, D_MODEL))],
        out_specs=row(D_MODEL),
        out_shape=jax.ShapeDtypeStruct((t, D_MODEL), F32),
        compiler_params=pltpu.CompilerParams(
            dimension_semantics=("arbitrary",), vmem_limit_bytes=VMEM_LIMIT_BYTES),
        name="tail",
    )(x2, og, om, p2, w_out.astype(BF16), mlp_norm.reshape(1, D_MODEL), w_up.astype(BF16),
      w_down.astype(BF16), ple_norm.reshape(1, D_MODEL), w_gate.astype(BF16),
      b_gate.reshape(1, D_MODEL), w_pp.astype(BF16))


def kernel(x, p, positions, attn_norm, w_in, gla_gate_w2, gla_gate_b, gla_out_norm, mla_q_norm,
           mla_w_uq, mla_kv_norm, mla_w_ukv, qk_norm_q, qk_norm_k, w_out, mlp_norm, w_mlp_up,
           w_mlp_down, ple_norm, w_ple_gate, b_ple_gate, w_ple_proj):
    b, s, d = x.shape
    depth = w_in.shape[0]
    t = b * s
    pos2 = positions.reshape(t, 1)
    h = x.reshape(t, d)
    for i in range(depth):
        qg, kg, vg, la, gr, q, k, v = _projection(
            h, pos2, attn_norm[i], w_in[i], gla_gate_w2[i], gla_gate_b[i], mla_q_norm[i],
            mla_w_uq[i], mla_kv_norm[i], mla_w_ukv[i], qk_norm_q[i], qk_norm_k[i])
        seq = lambda a: a.reshape(b, s, a.shape[-1])
        og = _gla(seq(qg), seq(kg), seq(vg), seq(la), seq(gr), gla_out_norm[i])
        om = _mla(seq(q), seq(k), seq(v))
        h = _tail(h, og.reshape(t, GLA_WIDTH), om.reshape(t, MLA_WIDTH), p[i].reshape(t, PLE_DIM),
                  w_out[i], mlp_norm[i], w_mlp_up[i], w_mlp_down[i], ple_norm[i], w_ple_gate[i],
                  b_ple_gate[i], w_ple_proj[i])
    return h.reshape(b, s, d)
```

```python
import functools
import math

import jax
import jax.numpy as jnp
from jax import lax
from jax.experimental import pallas as pl
from jax.experimental.pallas import tpu as pltpu

F32 = jnp.float32
BF16 = jnp.bfloat16

D_MODEL = 1024
CHUNK = 64
PLE_DIM = 256
D_FF = 4 * D_MODEL
EPS = 1e-6
GLA_HEADS = 4
GLA_DK = 64
GLA_DV = 128
GLA_GATE_RANK = 16
GLA_TAU = 16.0
GLA_QK = GLA_HEADS * GLA_DK
GLA_WIDTH = GLA_HEADS * GLA_DV
MLA_HEADS = 8
MLA_NOPE = 64
MLA_ROPE = 32
MLA_V = 64
MLA_QK = MLA_NOPE + MLA_ROPE
MLA_Q_RANK = 256
MLA_KV_RANK = 128
MLA_WIDTH = MLA_HEADS * MLA_V
ROPE_THETA = 10000.0
IN_SPLITS = (GLA_QK, GLA_QK, GLA_WIDTH, GLA_GATE_RANK, GLA_WIDTH, MLA_Q_RANK, MLA_KV_RANK, MLA_ROPE)

LANES = 128
SUBLANES = 8
HALF_ROPE = MLA_ROPE // 2
HEAD_PAD = MLA_HEADS * LANES
D_IN_PAD = 2048
VMEM_LIMIT_BYTES = 56 * 1024 * 1024

TM_PROJ = 512
TM_TAIL = 512
FF_CHUNK = 1024
TQ = 256
MLA_UNROLL = 4
ROPE_ROWS = 64
GLA_SEQS_PER_STEP = (4, 2, 1)
GLA_ROWS_PER_STEP = 512
GLA_FAST_MIN_LOG_DECAY = -80.0
MASKED_LOGIT = -1e30

ROW_FREQ16, ROW_MASK96, ROW_KRMASK, ROW_GQ, ROW_GQ_SW, ROW_GK, ROW_GK_SW = range(7)


def _const_spec(shape):
    return pl.BlockSpec(shape, lambda *_: (0,) * len(shape), pipeline_mode=pl.Buffered(1))


def _rms_scale(v, n):
    return lax.rsqrt(jnp.sum(v * v, axis=-1, keepdims=True) * (1.0 / n) + EPS)


def _proj_kernel(x_ref, pos_ref, g_attn_ref, w_in_ref, w2_ref, b2_ref, gqn_ref, w_uq_ref,
                 gkvn_ref, w_uk_ref, w_uv_ref, tab_ref,
                 qg_ref, kg_ref, vg_ref, la_ref, gr_ref, q_ref, k_ref, v_ref,
                 qs_ref, ks_ref, slot_ref):
    x = x_ref[...]
    xn = (x * _rms_scale(x, D_MODEL) * g_attn_ref[...]).astype(BF16)

    def proj(lo, hi):
        return jnp.dot(xn, w_in_ref[:, lo:hi], preferred_element_type=F32)

    zqk = proj(0, 2 * GLA_QK)
    qg_ref[...] = (zqk[:, :GLA_QK] * (GLA_DK ** -0.5)).astype(BF16)
    kg_ref[...] = zqk[:, GLA_QK:].astype(BF16)
    vg_ref[...] = proj(512, 1024).astype(BF16)
    gr = proj(1024, 1536)
    gr_ref[...] = (gr * jax.nn.sigmoid(gr)).astype(BF16)
    zc = proj(1536, 2048)
    cq = zc[:, :MLA_Q_RANK]
    ckv = zc[:, MLA_Q_RANK:MLA_Q_RANK + MLA_KV_RANK]
    slot = zc[:, MLA_Q_RANK + MLA_KV_RANK:]
    slot_ref[...] = slot

    xg = jnp.dot(slot.astype(BF16), w2_ref[...], preferred_element_type=F32) + b2_ref[...]
    la_ref[...] = (jnp.minimum(xg, 0.0) - jnp.log(1.0 + jnp.exp(-jnp.abs(xg)))) * (1.0 / GLA_TAU)

    cqn = (cq * _rms_scale(cq, MLA_Q_RANK) * gqn_ref[...]).astype(BF16)
    qs_ref[...] = jnp.dot(cqn, w_uq_ref[...], preferred_element_type=F32)
    ckvn = (ckv * _rms_scale(ckv, MLA_KV_RANK) * gkvn_ref[...]).astype(BF16)
    ks_ref[...] = jnp.dot(ckvn, w_uk_ref[...], preferred_element_type=F32)
    v_ref[...] = jnp.dot(ckvn, w_uv_ref[...], preferred_element_type=F32).astype(BF16)

    tab = lambda r: tab_ref[r:r + 1, :]
    mask96, krmask = tab(ROW_MASK96), tab(ROW_KRMASK)
    gq, gq_sw = tab(ROW_GQ), tab(ROW_GQ_SW)
    gk, gk_sw = tab(ROW_GK) * math.sqrt(MLA_QK), tab(ROW_GK_SW) * math.sqrt(MLA_QK)
    half = LANES // 2

    lane = lax.broadcasted_iota(jnp.int32, (1, LANES), 1)
    groups = ROPE_ROWS // SUBLANES
    assert groups * HALF_ROPE == LANES
    x1_lanes = (lane >= 32) & (lane < 32 + HALF_ROPE)
    x2_lanes = (lane >= 32 + HALF_ROPE) & (lane < 64)

    def rope_tables(rows):
        pos = jnp.broadcast_to(pos_ref[rows, :].astype(F32), (ROPE_ROWS, LANES))
        packed = jnp.zeros((SUBLANES, LANES), F32)
        for a in range(groups):
            packed = jnp.where(lane // HALF_ROPE == a, pos[a * SUBLANES:(a + 1) * SUBLANES], packed)
        ang = packed * tab(ROW_FREQ16)
        cos_p, sin_p = jnp.cos(ang), jnp.sin(ang)
        cos, sin = [], []
        for a in range(groups):
            s1 = (32 - HALF_ROPE * a) % LANES
            s2 = (32 + HALF_ROPE - HALF_ROPE * a) % LANES
            c1, c2 = pltpu.roll(cos_p, s1, 1), pltpu.roll(cos_p, s2, 1)
            n1, n2 = pltpu.roll(sin_p, s1, 1), pltpu.roll(sin_p, s2, 1)
            cos.append(jnp.where(x1_lanes, c1, jnp.where(x2_lanes, c2, mask96)))
            sin.append(jnp.where(x1_lanes, -n1, jnp.where(x2_lanes, n2, 0.0)))
        return jnp.concatenate(cos, axis=0), jnp.concatenate(sin, axis=0)

    sum96 = (lax.broadcasted_iota(jnp.int32, (LANES, LANES), 0) < MLA_QK).astype(BF16)

    def head_rms(v):
        ssq = jnp.dot((v * v).astype(BF16), sum96, preferred_element_type=F32)
        return lax.rsqrt(ssq + MLA_QK * EPS)

    def head_rows(rb, carry):
        rows = pl.ds(pl.multiple_of(rb * ROPE_ROWS, ROPE_ROWS), ROPE_ROWS)
        cos, sin = rope_tables(rows)
        kr = slot_ref[rows, :] * krmask
        kr_rot = pltpu.roll(kr, half, 1) * (sin * gk_sw)
        cos_q, sin_q, cos_k = cos * gq, sin * gq_sw, cos * gk
        for h in range(MLA_HEADS):
            sl = slice(h * LANES, (h + 1) * LANES)
            qh = qs_ref[rows, sl]
            q_ref[rows, sl] = ((qh * cos_q + pltpu.roll(qh, half, 1) * sin_q) * head_rms(qh)).astype(BF16)
            kh = ks_ref[rows, sl] + kr
            k_ref[rows, sl] = ((kh * cos_k + kr_rot) * head_rms(kh)).astype(BF16)
        return carry

    lax.fori_loop(0, x_ref.shape[0] // ROPE_ROWS, head_rows, 0, unroll=2)


def _head_layout(nope, rope):
    x1, x2 = rope[..., :HALF_ROPE], rope[..., HALF_ROPE:]
    return jnp.concatenate([nope[..., :32], x1, x2, nope[..., 32:], x2, x1], axis=-1)


def _projection(x2, pos2, attn_norm, w_in, gate_w2, gate_b, q_norm, w_uq, kv_norm, w_ukv,
                qk_norm_q, qk_norm_k):
    t = x2.shape[0]
    tm = TM_PROJ
    assert t % tm == 0

    gq, gk, gv, g_low, g_r, c_q, c_kv, k_r = jnp.split(w_in, list(_cumsum(IN_SPLITS))[:-1], axis=1)
    zeros = lambda n: jnp.zeros((D_MODEL, n), w_in.dtype)
    kr_sw = jnp.concatenate([k_r[:, HALF_ROPE:], k_r[:, :HALF_ROPE]], axis=1)
    slot = jnp.concatenate([g_low, zeros(32 - GLA_GATE_RANK), k_r, zeros(32), kr_sw], axis=1)
    w_in_l = jnp.concatenate([gq, gk, gv, g_r, c_q, c_kv, slot], axis=1).astype(BF16)
    assert w_in_l.shape == (D_MODEL, D_IN_PAD)

    w2_l = jnp.zeros((LANES, GLA_QK), F32).at[:GLA_GATE_RANK].set(gate_w2).astype(BF16)

    w_uq_h = w_uq.reshape(MLA_Q_RANK, MLA_HEADS, MLA_QK)
    w_uq_l = _head_layout(w_uq_h[..., :MLA_NOPE], w_uq_h[..., MLA_NOPE:]).reshape(MLA_Q_RANK, HEAD_PAD)
    w_ukv_h = w_ukv.reshape(MLA_KV_RANK, MLA_HEADS, MLA_NOPE + MLA_V)
    w_uk_l = _head_layout(w_ukv_h[..., :MLA_NOPE],
                          jnp.zeros((MLA_KV_RANK, MLA_HEADS, MLA_ROPE), F32)).reshape(MLA_KV_RANK, HEAD_PAD)
    w_uv = w_ukv_h[..., MLA_NOPE:].reshape(MLA_KV_RANK, MLA_WIDTH)
    gq_l = _head_layout(qk_norm_q[:MLA_NOPE], qk_norm_q[MLA_NOPE:])
    gk_l = _head_layout(qk_norm_k[:MLA_NOPE], qk_norm_k[MLA_NOPE:])

    inv_freq = ROPE_THETA ** (-jnp.arange(0, MLA_ROPE, 2, dtype=F32) / MLA_ROPE)
    z32 = jnp.zeros((32,), F32)
    tab = jnp.zeros((SUBLANES, LANES), F32)
    tab = tab.at[ROW_FREQ16].set(jnp.tile(inv_freq, LANES // HALF_ROPE))
    tab = tab.at[ROW_MASK96].set(jnp.concatenate([jnp.ones((96,), F32), z32]))
    tab = tab.at[ROW_KRMASK].set(jnp.concatenate([z32, jnp.ones((32,), F32), z32, jnp.ones((32,), F32)]))
    tab = tab.at[ROW_GQ].set(gq_l).at[ROW_GQ_SW].set(jnp.roll(gq_l, LANES // 2))
    tab = tab.at[ROW_GK].set(gk_l).at[ROW_GK_SW].set(jnp.roll(gk_l, LANES // 2))

    row = lambda w: pl.BlockSpec((tm, w), lambda i: (i, 0))
    out_widths = (GLA_QK, GLA_QK, GLA_WIDTH, GLA_QK, GLA_WIDTH, HEAD_PAD, HEAD_PAD, MLA_WIDTH)
    out_dtypes = (BF16, BF16, BF16, F32, BF16, BF16, BF16, BF16)
    return pl.pallas_call(
        _proj_kernel,
        grid=(t // tm,),
        in_specs=[
            row(D_MODEL), row(1),
            _const_spec((1, D_MODEL)), _const_spec((D_MODEL, D_IN_PAD)),
            _const_spec((LANES, GLA_QK)), _const_spec((1, GLA_QK)),
            _const_spec((1, MLA_Q_RANK)), _const_spec((MLA_Q_RANK, HEAD_PAD)),
            _const_spec((1, MLA_KV_RANK)), _const_spec((MLA_KV_RANK, HEAD_PAD)),
            _const_spec((MLA_KV_RANK, MLA_WIDTH)),
            _const_spec((8, LANES)),
        ],
        out_specs=[row(w) for w in out_widths],
        out_shape=[jax.ShapeDtypeStruct((t, w), d) for w, d in zip(out_widths, out_dtypes)],
        scratch_shapes=[pltpu.VMEM((tm, HEAD_PAD), F32), pltpu.VMEM((tm, HEAD_PAD), F32),
                        pltpu.VMEM((tm, LANES), F32)],
        compiler_params=pltpu.CompilerParams(
            dimension_semantics=("arbitrary",), vmem_limit_bytes=VMEM_LIMIT_BYTES),
        name="proj",
    )(x2, pos2, attn_norm.reshape(1, D_MODEL), w_in_l, w2_l, gate_b.reshape(1, GLA_QK),
      q_norm.reshape(1, MLA_Q_RANK), w_uq_l.astype(BF16), kv_norm.reshape(1, MLA_KV_RANK),
      w_uk_l.astype(BF16), w_uv.astype(BF16), tab)


def _cumsum(sizes):
    total = 0
    for s in sizes:
        total += s
        yield total


def _gla_kernel(q_ref, k_ref, v_ref, la_ref, gr_ref, gon_ref, o_ref, st_ref, kf_ref, b_ref):
    nseq, rows_per_step = q_ref.shape[0], q_ref.shape[1]

    @pl.when(pl.program_id(1) == 0)
    def _():
        st_ref[...] = jnp.zeros_like(st_ref)

    row = lax.broadcasted_iota(jnp.int32, (CHUNK, CHUNK), 0)
    col = lax.broadcasted_iota(jnp.int32, (CHUNK, CHUNK), 1)
    causal = col <= row
    tril = causal.astype(BF16)
    lane = lax.broadcasted_iota(jnp.int32, (1, LANES), 1)
    lane_q = lax.broadcasted_iota(jnp.int32, (1, GLA_QK), 1)
    head_mask = [(lane // GLA_DK) == e for e in range(2)]
    gon = gon_ref[...]
    nt = (((1,), (1,)), ((), ()))
    tn = (((0,), (0,)), ((), ()))

    def exact_scores(q, k, b):
        kf_ref[...] = k
        b_ref[...] = b

        def key_row(s, sc):
            k_s = kf_ref[pl.ds(s, 1), :]
            b_s = b_ref[pl.ds(s, 1), :]
            prod = q * k_s * jnp.exp(jnp.minimum(b - b_s, 0.0))
            out = []
            for h in range(GLA_HEADS):
                hm = (lane_q // GLA_DK) == h
                col_h = jnp.sum(jnp.where(hm, prod, 0.0), axis=-1, keepdims=True)
                out.append(jnp.where(col == s, col_h, sc[h]))
            return tuple(out)

        zero = jnp.zeros((CHUNK, CHUNK), F32)
        return lax.fori_loop(0, CHUNK, key_row, (zero,) * GLA_HEADS)

    pair = lambda h: slice((h // 2) * LANES, (h // 2 + 1) * LANES)
    vsl = lambda h: slice(h * GLA_DV, (h + 1) * GLA_DV)
    heads = range(GLA_HEADS)

    def chunk(seqs, c, factorised):
        rows = pl.ds(pl.multiple_of(c * CHUNK, CHUNK), CHUNK)
        pieces = []
        for n in seqs:
            la = la_ref[n, rows, :]
            hi = la.astype(BF16)
            r1 = la - hi.astype(F32)
            mid = r1.astype(BF16)
            pieces += [hi, mid, (r1 - mid.astype(F32)).astype(BF16)]
        cum = jnp.dot(tril, jnp.concatenate(pieces, axis=1), preferred_element_type=F32)
        q, k, b, a_h, kd, eb_last, scores = {}, {}, {}, {}, {}, {}, {}
        for i, n in enumerate(seqs):
            parts = [cum[:, (3 * i + j) * GLA_QK:(3 * i + j + 1) * GLA_QK] for j in range(3)]
            b[n] = parts[0] + parts[1] + parts[2]
            b_last = b[n][CHUNK - 1:CHUNK, :]
            q[n] = q_ref[n, rows, :].astype(F32)
            k[n] = k_ref[n, rows, :].astype(F32)
            qt = q[n] * jnp.exp(b[n])
            kd[n] = k[n] * jnp.exp(b_last - b[n])
            eb_last[n] = jnp.exp(b_last)
            a_h[n] = [jnp.where(head_mask[h % 2], qt[:, pair(h)], 0.0).astype(BF16) for h in heads]
        for n in seqs:
            if factorised:
                kt = (k[n] * jnp.exp(-b[n])).astype(BF16)
                scores[n] = [lax.dot_general(a_h[n][h], kt[:, pair(h)], nt, preferred_element_type=F32)
                             for h in heads]
            else:
                scores[n] = exact_scores(q[n], k[n], b[n])
        v, st, o = {}, {}, {}
        for n in seqs:
            for h in heads:
                sc = jnp.where(causal, scores[n][h], 0.0).astype(BF16)
                v[n, h] = v_ref[n, rows, vsl(h)]
                st[n, h] = st_ref[n, h]
                o[n, h] = (lax.dot_general(a_h[n][h], st[n, h].astype(BF16), nt, preferred_element_type=F32)
                           + jnp.dot(sc, v[n, h], preferred_element_type=F32))
        for n in seqs:
            for h in heads:
                kd_h = jnp.where(head_mask[h % 2], kd[n][:, pair(h)], 0.0).astype(BF16)
                upd = lax.dot_general(v[n, h], kd_h, tn, preferred_element_type=F32)
                st_ref[n, h] = st[n, h] * eb_last[n][:, pair(h)] + upd
        for n in seqs:
            for h in heads:
                on = o[n, h] * _rms_scale(o[n, h], GLA_DV) * gon
                o_ref[n, rows, vsl(h)] = (on * gr_ref[n, rows, vsl(h)].astype(F32)).astype(BF16)

    def factorised_chunks():
        def body(c, carry):
            chunk(range(nseq), c, True)
            return carry
        lax.fori_loop(0, rows_per_step // CHUNK, body, 0)

    def exact_chunks():
        def body(c, carry):
            for n in range(nseq):
                chunk([n], c, False)
            return carry
        lax.fori_loop(0, rows_per_step // CHUNK, body, 0)

    factorisable = jnp.min(la_ref[...]) >= GLA_FAST_MIN_LOG_DECAY / CHUNK
    lax.cond(factorisable, factorised_chunks, exact_chunks)


def _gla(qg, kg, vg, la, gr, out_norm):
    b, s, _ = qg.shape
    nseq = max(n for n in GLA_SEQS_PER_STEP if b % n == 0)
    rows = min(GLA_ROWS_PER_STEP, s)
    assert s % rows == 0 and rows % CHUNK == 0
    blk = lambda w: pl.BlockSpec((nseq, rows, w), lambda i, j: (i, j, 0))
    return pl.pallas_call(
        _gla_kernel,
        grid=(b // nseq, s // rows),
        in_specs=[blk(GLA_QK), blk(GLA_QK), blk(GLA_WIDTH), blk(GLA_QK), blk(GLA_WIDTH),
                  _const_spec((1, GLA_DV))],
        out_specs=blk(GLA_WIDTH),
        out_shape=jax.ShapeDtypeStruct((b, s, GLA_WIDTH), BF16),
        scratch_shapes=[pltpu.VMEM((nseq, GLA_HEADS, GLA_DV, LANES), F32),
                        pltpu.VMEM((CHUNK, GLA_QK), F32),
                        pltpu.VMEM((CHUNK, GLA_QK), F32)],
        compiler_params=pltpu.CompilerParams(
            dimension_semantics=("arbitrary", "arbitrary"), vmem_limit_bytes=VMEM_LIMIT_BYTES),
        name="gla",
    )(qg, kg, vg, la, gr, out_norm.reshape(1, GLA_DV))


def _mla_tiles(n):
    pairs = [(i, i) for i in range(n)] + [(i, j) for i in range(1, n) for j in range(i)]
    return [i for i, _ in pairs], [j for _, j in pairs]


def _mla_kernel(ti_ref, tj_ref, q_ref, k_ref, v_ref, o_ref, s_ref, m_ref, acc_ref, v1_ref):
    seq = q_ref.shape[1]
    tq = TQ
    n = seq // tq
    n_tiles = n * (n + 1) // 2
    nt = (((1,), (1,)), ((), ()))
    heads = range(2)
    lane = lax.broadcasted_iota(jnp.int32, (1, LANES), 1)
    own = [(lane // MLA_V) == e for e in heads]

    def rows(t):
        return pl.ds(pl.multiple_of(t * tq, tq), tq)

    def logits(e, i, j):
        hsl = slice(e * LANES, (e + 1) * LANES)
        return lax.dot_general(q_ref[0, rows(i), hsl], k_ref[0, rows(j), hsl], nt,
                               preferred_element_type=F32)

    def lane_fold_max(a):
        out = a[:, :LANES]
        for c in range(1, tq // LANES):
            out = jnp.maximum(out, a[:, c * LANES:(c + 1) * LANES])
        return out

    v = v_ref[0]
    for e in heads:
        v1_ref[e] = jnp.where(own[e], v, 1.0).astype(BF16)

    def diag(i, carry):
        row_chunk = lax.broadcasted_iota(jnp.int32, (tq, tq), 0) // CHUNK
        col_chunk = lax.broadcasted_iota(jnp.int32, (tq, tq), 1) // CHUNK
        for e in heads:
            s = jnp.where(col_chunk <= row_chunk, logits(e, i, i), MASKED_LOGIT)
            s_ref[e, i] = s
            m_ref[e, i] = lane_fold_max(s)
            acc_ref[e, i] = jnp.zeros((tq, LANES), F32)
        return carry

    lax.fori_loop(0, n, diag, 0, unroll=MLA_UNROLL)

    def pass1(t, carry):
        i, j = ti_ref[t], tj_ref[t]
        for e in heads:
            s = logits(e, i, j)
            s_ref[e, t] = s
            m_ref[e, i] = jnp.maximum(m_ref[e, i], lane_fold_max(s))
        return carry

    lax.fori_loop(n, n_tiles, pass1, 0, unroll=MLA_UNROLL)

    def row_max(i, carry):
        for e in heads:
            m_ref[e, i] = jnp.broadcast_to(jnp.max(m_ref[e, i], axis=-1, keepdims=True), (tq, LANES))
        return carry

    lax.fori_loop(0, n, row_max, 0, unroll=2)

    def pass2(t, carry):
        i, j = ti_ref[t], tj_ref[t]
        for e in heads:
            s = s_ref[e, t]
            m = m_ref[e, i]
            p = jnp.concatenate(
                [jnp.exp(s[:, c * LANES:(c + 1) * LANES] - m) for c in range(tq // LANES)], axis=1)
            acc_ref[e, i] += jnp.dot(p.astype(BF16), v1_ref[e, rows(j), :], preferred_element_type=F32)
        return carry

    lax.fori_loop(0, n_tiles, pass2, 0, unroll=MLA_UNROLL)

    def finish(i, carry):
        acc0, acc1 = acc_ref[0, i], acc_ref[1, i]
        num = jnp.where(own[0], acc0, acc1)
        den = pltpu.roll(jnp.where(own[0], acc1, acc0), LANES // 2, 1)
        o_ref[0, rows(i), :] = (num / den).astype(BF16)
        return carry

    lax.fori_loop(0, n, finish, 0, unroll=2)


def _mla(q, k, v):
    b, s, _ = q.shape
    tq = TQ
    assert s % (2 * tq) == 0
    n = s // tq
    ti, tj = _mla_tiles(n)
    blk = lambda w: pl.BlockSpec((1, s, w), lambda i, p, *_: (i, 0, p))
    return pl.pallas_call(
        _mla_kernel,
        grid_spec=pltpu.PrefetchScalarGridSpec(
            num_scalar_prefetch=2,
            grid=(b, MLA_HEADS // 2),
            in_specs=[blk(2 * LANES), blk(2 * LANES), blk(LANES)],
            out_specs=blk(LANES),
            scratch_shapes=[pltpu.VMEM((2, len(ti), tq, tq), F32),
                            pltpu.VMEM((2, n, tq, LANES), F32),
                            pltpu.VMEM((2, n, tq, LANES), F32),
                            pltpu.VMEM((2, s, LANES), BF16)]),
        out_shape=jax.ShapeDtypeStruct((b, s, MLA_WIDTH), BF16),
        compiler_params=pltpu.CompilerParams(
            dimension_semantics=("arbitrary", "arbitrary"),
            vmem_limit_bytes=VMEM_LIMIT_BYTES),
        name="mla",
    )(jnp.asarray(ti, jnp.int32), jnp.asarray(tj, jnp.int32), q, k, v)


def _tail_kernel(x_ref, og_ref, om_ref, p_ref, w_out_ref, g_mlp_ref, w_up_ref, w_down_ref,
                 g_ple_ref, w_gate_ref, b_gate_ref, w_pp_ref, o_ref):
    h = (x_ref[...]
         + jnp.dot(og_ref[...], w_out_ref[:GLA_WIDTH, :], preferred_element_type=F32)
         + jnp.dot(om_ref[...], w_out_ref[GLA_WIDTH:, :], preferred_element_type=F32))
    m = (h * _rms_scale(h, D_MODEL) * g_mlp_ref[...]).astype(BF16)
    mlp = None
    for c in range(D_FF // FF_CHUNK):
        cols = slice(c * FF_CHUNK, (c + 1) * FF_CHUNK)
        u = jnp.maximum(jnp.dot(m, w_up_ref[:, cols], preferred_element_type=F32), 0.0)
        d = jnp.dot((u * u).astype(BF16), w_down_ref[cols, :], preferred_element_type=F32)
        mlp = d if mlp is None else mlp + d
    h = h + mlp
    g = (h * _rms_scale(h, D_MODEL) * g_ple_ref[...]).astype(BF16)
    gate = jax.nn.sigmoid(jnp.dot(g, w_gate_ref[...], preferred_element_type=F32) + b_gate_ref[...])
    pp = jnp.dot(p_ref[...].astype(BF16), w_pp_ref[...], preferred_element_type=F32)
    o_ref[...] = h + pp * gate


def _tail(x2, og, om, p2, w_out, mlp_norm, w_up, w_down, ple_norm, w_gate, b_gate, w_pp):
    t = x2.shape[0]
    tm = TM_TAIL
    assert t % tm == 0
    row = lambda w: pl.BlockSpec((tm, w), lambda i: (i, 0))
    return pl.pallas_call(
        _tail_kernel,
        grid=(t // tm,),
        in_specs=[row(D_MODEL), row(GLA_WIDTH), row(MLA_WIDTH), row(PLE_DIM),
                  _const_spec((D_MODEL, D_MODEL)), _const_spec((1, D_MODEL)),
                  _const_spec((D_MODEL, D_FF)), _const_spec((D_FF, D_MODEL)),
                  _const_spec((1, D_MODEL)), _const_spec((D_MODEL, D_MODEL)),
                  _const_spec((1, D_MODEL)), _const_spec((PLE_DIM, D_MODEL))],
        out_specs=row(D_MODEL),
        out_shape=jax.ShapeDtypeStruct((t, D_MODEL), F32),
        compiler_params=pltpu.CompilerParams(
            dimension_semantics=("arbitrary",), vmem_limit_bytes=VMEM_LIMIT_BYTES),
        name="tail",
    )(x2, og, om, p2, w_out.astype(BF16), mlp_norm.reshape(1, D_MODEL), w_up.astype(BF16),
      w_down.astype(BF16), ple_norm.reshape(1, D_MODEL), w_gate.astype(BF16),
      b_gate.reshape(1, D_MODEL), w_pp.astype(BF16))


def kernel(x, p, positions, attn_norm, w_in, gla_gate_w2, gla_gate_b, gla_out_norm, mla_q_norm,
           mla_w_uq, mla_kv_norm, mla_w_ukv, qk_norm_q, qk_norm_k, w_out, mlp_norm, w_mlp_up,
           w_mlp_down, ple_norm, w_ple_gate, b_ple_gate, w_ple_proj):
    b, s, d = x.shape
    depth = w_in.shape[0]
    t = b * s
    pos2 = positions.reshape(t, 1)
    h = x.reshape(t, d)
    for i in range(depth):
        qg, kg, vg, la, gr, q, k, v = _projection(
            h, pos2, attn_norm[i], w_in[i], gla_gate_w2[i], gla_gate_b[i], mla_q_norm[i],
            mla_w_uq[i], mla_kv_norm[i], mla_w_ukv[i], qk_norm_q[i], qk_norm_k[i])
        seq = lambda a: a.reshape(b, s, a.shape[-1])
        og = _gla(seq(qg), seq(kg), seq(vg), seq(la), seq(gr), gla_out_norm[i])
        om = _mla(seq(q), seq(k), seq(v))
        h = _tail(h, og.reshape(t, GLA_WIDTH), om.reshape(t, MLA_WIDTH), p[i].reshape(t, PLE_DIM),
                  w_out[i], mlp_norm[i], w_mlp_up[i], w_mlp_down[i], ple_norm[i], w_ple_gate[i],
                  b_ple_gate[i], w_ple_proj[i])
    return h.reshape(b, s, d)
```

```python
import functools
import math

import jax
import jax.numpy as jnp
from jax import lax
from jax.experimental import pallas as pl
from jax.experimental.pallas import tpu as pltpu

F32 = jnp.float32
BF16 = jnp.bfloat16

D_MODEL = 1024
CHUNK = 64
PLE_DIM = 256
D_FF = 4 * D_MODEL
EPS = 1e-6
GLA_HEADS = 4
GLA_DK = 64
GLA_DV = 128
GLA_GATE_RANK = 16
GLA_TAU = 16.0
GLA_QK = GLA_HEADS * GLA_DK
GLA_WIDTH = GLA_HEADS * GLA_DV
MLA_HEADS = 8
MLA_NOPE = 64
MLA_ROPE = 32
MLA_V = 64
MLA_QK = MLA_NOPE + MLA_ROPE
MLA_Q_RANK = 256
MLA_KV_RANK = 128
MLA_WIDTH = MLA_HEADS * MLA_V
ROPE_THETA = 10000.0
LOG2E = math.log2(math.e)
IN_SPLITS = (GLA_QK, GLA_QK, GLA_WIDTH, GLA_GATE_RANK, GLA_WIDTH, MLA_Q_RANK, MLA_KV_RANK, MLA_ROPE)

LANES = 128
SUBLANES = 8
HALF_ROPE = MLA_ROPE // 2
HEAD_PAD = MLA_HEADS * LANES
D_IN_PAD = 2048
VMEM_LIMIT_BYTES = 56 * 1024 * 1024

TM_PROJ = 512
TM_TAIL = 512
FF_CHUNK = 1024
TQ = 256
MLA_UNROLL = 4
ROPE_ROWS = 64
GLA_SEQS_PER_STEP = (8, 4, 2, 1)
GLA_ROWS_PER_STEP = 256
GLA_FAST_MIN_LOG_DECAY = -80.0
MASKED_LOGIT = -1e30

ROW_FREQ16, ROW_MASK96, ROW_KRMASK, ROW_GQ, ROW_GQ_SW, ROW_GK, ROW_GK_SW = range(7)


def _const_spec(shape):
    return pl.BlockSpec(shape, lambda *_: (0,) * len(shape), pipeline_mode=pl.Buffered(1))


def _rms_scale(v, n):
    return lax.rsqrt(jnp.sum(v * v, axis=-1, keepdims=True) * (1.0 / n) + EPS)


def _proj_kernel(x_ref, pos_ref, g_attn_ref, w_in_ref, w2_ref, b2_ref, gqn_ref, w_uq_ref,
                 gkvn_ref, w_uk_ref, w_uv_ref, tab_ref,
                 qg_ref, kg_ref, vg_ref, la_ref, gr_ref, q_ref, k_ref, v_ref,
                 qs_ref, ks_ref, slot_ref):
    x = x_ref[...]
    xn = (x * _rms_scale(x, D_MODEL) * g_attn_ref[...]).astype(BF16)

    def proj(lo, hi):
        return jnp.dot(xn, w_in_ref[:, lo:hi], preferred_element_type=F32)

    zqk = proj(0, 2 * GLA_QK)
    qg_ref[...] = (zqk[:, :GLA_QK] * (GLA_DK ** -0.5)).astype(BF16)
    kg_ref[...] = zqk[:, GLA_QK:].astype(BF16)
    vg_ref[...] = proj(512, 1024).astype(BF16)
    gr = proj(1024, 1536)
    gr_ref[...] = (gr * jax.nn.sigmoid(gr)).astype(BF16)
    zc = proj(1536, 2048)
    cq = zc[:, :MLA_Q_RANK]
    ckv = zc[:, MLA_Q_RANK:MLA_Q_RANK + MLA_KV_RANK]
    slot = zc[:, MLA_Q_RANK + MLA_KV_RANK:]
    slot_ref[...] = slot

    xg = jnp.dot(slot.astype(BF16), w2_ref[...], preferred_element_type=F32) + b2_ref[...]
    la_ref[...] = (jnp.minimum(xg, 0.0) - jnp.log(1.0 + jnp.exp(-jnp.abs(xg)))) * (1.0 / GLA_TAU)

    cqn = (cq * _rms_scale(cq, MLA_Q_RANK) * gqn_ref[...]).astype(BF16)
    qs_ref[...] = jnp.dot(cqn, w_uq_ref[...], preferred_element_type=F32)
    ckvn = (ckv * _rms_scale(ckv, MLA_KV_RANK) * gkvn_ref[...]).astype(BF16)
    ks_ref[...] = jnp.dot(ckvn, w_uk_ref[...], preferred_element_type=F32)
    v_ref[...] = jnp.dot(ckvn, w_uv_ref[...], preferred_element_type=F32).astype(BF16)

    tab = lambda r: tab_ref[r:r + 1, :]
    mask96, krmask = tab(ROW_MASK96), tab(ROW_KRMASK)
    gq, gq_sw = tab(ROW_GQ) * LOG2E, tab(ROW_GQ_SW) * LOG2E
    gk, gk_sw = tab(ROW_GK) * math.sqrt(MLA_QK), tab(ROW_GK_SW) * math.sqrt(MLA_QK)
    half = LANES // 2

    lane = lax.broadcasted_iota(jnp.int32, (1, LANES), 1)
    groups = ROPE_ROWS // SUBLANES
    assert groups * HALF_ROPE == LANES
    x1_lanes = (lane >= 32) & (lane < 32 + HALF_ROPE)
    x2_lanes = (lane >= 32 + HALF_ROPE) & (lane < 64)

    def rope_tables(rows):
        pos = jnp.broadcast_to(pos_ref[rows, :].astype(F32), (ROPE_ROWS, LANES))
        packed = jnp.zeros((SUBLANES, LANES), F32)
        for a in range(groups):
            packed = jnp.where(lane // HALF_ROPE == a, pos[a * SUBLANES:(a + 1) * SUBLANES], packed)
        ang = packed * tab(ROW_FREQ16)
        cos_p, sin_p = jnp.cos(ang), jnp.sin(ang)
        cos, sin = [], []
        for a in range(groups):
            s1 = (32 - HALF_ROPE * a) % LANES
            s2 = (32 + HALF_ROPE - HALF_ROPE * a) % LANES
            c1, c2 = pltpu.roll(cos_p, s1, 1), pltpu.roll(cos_p, s2, 1)
            n1, n2 = pltpu.roll(sin_p, s1, 1), pltpu.roll(sin_p, s2, 1)
            cos.append(jnp.where(x1_lanes, c1, jnp.where(x2_lanes, c2, mask96)))
            sin.append(jnp.where(x1_lanes, -n1, jnp.where(x2_lanes, n2, 0.0)))
        return jnp.concatenate(cos, axis=0), jnp.concatenate(sin, axis=0)

    sum96 = (lax.broadcasted_iota(jnp.int32, (LANES, LANES), 0) < MLA_QK).astype(BF16)

    def head_rms(v):
        ssq = jnp.dot((v * v).astype(BF16), sum96, preferred_element_type=F32)
        return lax.rsqrt(ssq + MLA_QK * EPS)

    def head_rows(rb, carry):
        rows = pl.ds(pl.multiple_of(rb * ROPE_ROWS, ROPE_ROWS), ROPE_ROWS)
        cos, sin = rope_tables(rows)
        kr = slot_ref[rows, :] * krmask
        kr_rot = pltpu.roll(kr, half, 1) * (sin * gk_sw)
        cos_q, sin_q, cos_k = cos * gq, sin * gq_sw, cos * gk
        for h in range(MLA_HEADS):
            sl = slice(h * LANES, (h + 1) * LANES)
            qh = qs_ref[rows, sl]
            q_ref[rows, sl] = ((qh * cos_q + pltpu.roll(qh, half, 1) * sin_q) * head_rms(qh)).astype(BF16)
            kh = ks_ref[rows, sl] + kr
            k_ref[rows, sl] = ((kh * cos_k + kr_rot) * head_rms(kh)).astype(BF16)
        return carry

    lax.fori_loop(0, x_ref.shape[0] // ROPE_ROWS, head_rows, 0, unroll=2)


def _head_layout(nope, rope):
    x1, x2 = rope[..., :HALF_ROPE], rope[..., HALF_ROPE:]
    return jnp.concatenate([nope[..., :32], x1, x2, nope[..., 32:], x2, x1], axis=-1)


def _projection(x2, pos2, attn_norm, w_in, gate_w2, gate_b, q_norm, w_uq, kv_norm, w_ukv,
                qk_norm_q, qk_norm_k):
    t = x2.shape[0]
    tm = TM_PROJ
    assert t % tm == 0

    gq, gk, gv, g_low, g_r, c_q, c_kv, k_r = jnp.split(w_in, list(_cumsum(IN_SPLITS))[:-1], axis=1)
    zeros = lambda n: jnp.zeros((D_MODEL, n), w_in.dtype)
    kr_sw = jnp.concatenate([k_r[:, HALF_ROPE:], k_r[:, :HALF_ROPE]], axis=1)
    slot = jnp.concatenate([g_low, zeros(32 - GLA_GATE_RANK), k_r, zeros(32), kr_sw], axis=1)
    w_in_l = jnp.concatenate([gq, gk, gv, g_r, c_q, c_kv, slot], axis=1).astype(BF16)
    assert w_in_l.shape == (D_MODEL, D_IN_PAD)

    w2_l = jnp.zeros((LANES, GLA_QK), F32).at[:GLA_GATE_RANK].set(gate_w2).astype(BF16)

    w_uq_h = w_uq.reshape(MLA_Q_RANK, MLA_HEADS, MLA_QK)
    w_uq_l = _head_layout(w_uq_h[..., :MLA_NOPE], w_uq_h[..., MLA_NOPE:]).reshape(MLA_Q_RANK, HEAD_PAD)
    w_ukv_h = w_ukv.reshape(MLA_KV_RANK, MLA_HEADS, MLA_NOPE + MLA_V)
    w_uk_l = _head_layout(w_ukv_h[..., :MLA_NOPE],
                          jnp.zeros((MLA_KV_RANK, MLA_HEADS, MLA_ROPE), F32)).reshape(MLA_KV_RANK, HEAD_PAD)
    w_uv = w_ukv_h[..., MLA_NOPE:].reshape(MLA_KV_RANK, MLA_WIDTH)
    gq_l = _head_layout(qk_norm_q[:MLA_NOPE], qk_norm_q[MLA_NOPE:])
    gk_l = _head_layout(qk_norm_k[:MLA_NOPE], qk_norm_k[MLA_NOPE:])

    inv_freq = ROPE_THETA ** (-jnp.arange(0, MLA_ROPE, 2, dtype=F32) / MLA_ROPE)
    z32 = jnp.zeros((32,), F32)
    tab = jnp.zeros((SUBLANES, LANES), F32)
    tab = tab.at[ROW_FREQ16].set(jnp.tile(inv_freq, LANES // HALF_ROPE))
    tab = tab.at[ROW_MASK96].set(jnp.concatenate([jnp.ones((96,), F32), z32]))
    tab = tab.at[ROW_KRMASK].set(jnp.concatenate([z32, jnp.ones((32,), F32), z32, jnp.ones((32,), F32)]))
    tab = tab.at[ROW_GQ].set(gq_l).at[ROW_GQ_SW].set(jnp.roll(gq_l, LANES // 2))
    tab = tab.at[ROW_GK].set(gk_l).at[ROW_GK_SW].set(jnp.roll(gk_l, LANES // 2))

    row = lambda w: pl.BlockSpec((tm, w), lambda i: (i, 0))
    out_widths = (GLA_QK, GLA_QK, GLA_WIDTH, GLA_QK, GLA_WIDTH, HEAD_PAD, HEAD_PAD, MLA_WIDTH)
    out_dtypes = (BF16, BF16, BF16, F32, BF16, BF16, BF16, BF16)
    return pl.pallas_call(
        _proj_kernel,
        grid=(t // tm,),
        in_specs=[
            row(D_MODEL), row(1),
            _const_spec((1, D_MODEL)), _const_spec((D_MODEL, D_IN_PAD)),
            _const_spec((LANES, GLA_QK)), _const_spec((1, GLA_QK)),
            _const_spec((1, MLA_Q_RANK)), _const_spec((MLA_Q_RANK, HEAD_PAD)),
            _const_spec((1, MLA_KV_RANK)), _const_spec((MLA_KV_RANK, HEAD_PAD)),
            _const_spec((MLA_KV_RANK, MLA_WIDTH)),
            _const_spec((8, LANES)),
        ],
        out_specs=[row(w) for w in out_widths],
        out_shape=[jax.ShapeDtypeStruct((t, w), d) for w, d in zip(out_widths, out_dtypes)],
        scratch_shapes=[pltpu.VMEM((tm, HEAD_PAD), F32), pltpu.VMEM((tm, HEAD_PAD), F32),
                        pltpu.VMEM((tm, LANES), F32)],
        compiler_params=pltpu.CompilerParams(
            dimension_semantics=("arbitrary",), vmem_limit_bytes=VMEM_LIMIT_BYTES),
        name="proj",
    )(x2, pos2, attn_norm.reshape(1, D_MODEL), w_in_l, w2_l, gate_b.reshape(1, GLA_QK),
      q_norm.reshape(1, MLA_Q_RANK), w_uq_l.astype(BF16), kv_norm.reshape(1, MLA_KV_RANK),
      w_uk_l.astype(BF16), w_uv.astype(BF16), tab)


def _cumsum(sizes):
    total = 0
    for s in sizes:
        total += s
        yield total


def _gla_kernel(q_ref, k_ref, v_ref, la_ref, gr_ref, gon_ref, o_ref, st_ref, kf_ref, b_ref):
    nseq, rows_per_step = q_ref.shape[0], q_ref.shape[1]

    @pl.when(pl.program_id(1) == 0)
    def _():
        st_ref[...] = jnp.zeros_like(st_ref)

    row = lax.broadcasted_iota(jnp.int32, (CHUNK, CHUNK), 0)
    col = lax.broadcasted_iota(jnp.int32, (CHUNK, CHUNK), 1)
    causal = col <= row
    tril = causal.astype(BF16)
    lane = lax.broadcasted_iota(jnp.int32, (1, LANES), 1)
    lane_q = lax.broadcasted_iota(jnp.int32, (1, GLA_QK), 1)
    head_mask = [(lane // GLA_DK) == e for e in range(2)]
    gon = gon_ref[...]
    nt = (((1,), (1,)), ((), ()))
    tn = (((0,), (0,)), ((), ()))

    def exact_scores(q, k, b):
        kf_ref[...] = k
        b_ref[...] = b

        def key_row(s, sc):
            k_s = kf_ref[pl.ds(s, 1), :]
            b_s = b_ref[pl.ds(s, 1), :]
            prod = q * k_s * jnp.exp(jnp.minimum(b - b_s, 0.0))
            out = []
            for h in range(GLA_HEADS):
                hm = (lane_q // GLA_DK) == h
                col_h = jnp.sum(jnp.where(hm, prod, 0.0), axis=-1, keepdims=True)
                out.append(jnp.where(col == s, col_h, sc[h]))
            return tuple(out)

        zero = jnp.zeros((CHUNK, CHUNK), F32)
        return lax.fori_loop(0, CHUNK, key_row, (zero,) * GLA_HEADS)

    pair = lambda h: slice((h // 2) * LANES, (h // 2 + 1) * LANES)
    vsl = lambda h: slice(h * GLA_DV, (h + 1) * GLA_DV)
    heads = range(GLA_HEADS)

    def chunk(seqs, c, factorised):
        rows = pl.ds(pl.multiple_of(c * CHUNK, CHUNK), CHUNK)
        pieces = []
        for n in seqs:
            la = la_ref[n, rows, :]
            hi = la.astype(BF16)
            r1 = la - hi.astype(F32)
            mid = r1.astype(BF16)
            pieces += [hi, mid, (r1 - mid.astype(F32)).astype(BF16)]
        cum = jnp.dot(tril, jnp.concatenate(pieces, axis=1), preferred_element_type=F32)
        q, k, b, a_h, kd, eb_last, scores = {}, {}, {}, {}, {}, {}, {}
        for i, n in enumerate(seqs):
            parts = [cum[:, (3 * i + j) * GLA_QK:(3 * i + j + 1) * GLA_QK] for j in range(3)]
            b[n] = parts[0] + parts[1] + parts[2]
            b_last = b[n][CHUNK - 1:CHUNK, :]
            q[n] = q_ref[n, rows, :].astype(F32)
            k[n] = k_ref[n, rows, :].astype(F32)
            qt = q[n] * jnp.exp(b[n])
            kd[n] = k[n] * jnp.exp(b_last - b[n])
            eb_last[n] = jnp.exp(b_last)
            a_h[n] = [jnp.where(head_mask[h % 2], qt[:, pair(h)], 0.0).astype(BF16) for h in heads]
        for n in seqs:
            if factorised:
                kt = (k[n] * jnp.exp(-b[n])).astype(BF16)
                scores[n] = [lax.dot_general(a_h[n][h], kt[:, pair(h)], nt, preferred_element_type=F32)
                             for h in heads]
            else:
                scores[n] = exact_scores(q[n], k[n], b[n])
        v, st, o = {}, {}, {}
        for n in seqs:
            for h in heads:
                sc = jnp.where(causal, scores[n][h], 0.0).astype(BF16)
                v[n, h] = v_ref[n, rows, vsl(h)]
                st[n, h] = st_ref[n, h]
                o[n, h] = (lax.dot_general(a_h[n][h], st[n, h].astype(BF16), nt, preferred_element_type=F32)
                           + jnp.dot(sc, v[n, h], preferred_element_type=F32))
        for n in seqs:
            for h in heads:
                kd_h = jnp.where(head_mask[h % 2], kd[n][:, pair(h)], 0.0).astype(BF16)
                upd = lax.dot_general(v[n, h], kd_h, tn, preferred_element_type=F32)
                st_ref[n, h] = st[n, h] * eb_last[n][:, pair(h)] + upd
        for n in seqs:
            for h in heads:
                on = o[n, h] * _rms_scale(o[n, h], GLA_DV) * gon
                o_ref[n, rows, vsl(h)] = (on * gr_ref[n, rows, vsl(h)].astype(F32)).astype(BF16)

    def factorised_chunks():
        def body(c, carry):
            chunk(range(nseq), c, True)
            return carry
        lax.fori_loop(0, rows_per_step // CHUNK, body, 0)

    def exact_chunks():
        def body(c, carry):
            for n in range(nseq):
                chunk([n], c, False)
            return carry
        lax.fori_loop(0, rows_per_step // CHUNK, body, 0)

    factorisable = jnp.min(la_ref[...]) >= GLA_FAST_MIN_LOG_DECAY / CHUNK
    lax.cond(factorisable, factorised_chunks, exact_chunks)


def _gla(qg, kg, vg, la, gr, out_norm):
    b, s, _ = qg.shape
    nseq = max(n for n in GLA_SEQS_PER_STEP if b % n == 0)
    rows = min(GLA_ROWS_PER_STEP, s)
    assert s % rows == 0 and rows % CHUNK == 0
    blk = lambda w: pl.BlockSpec((nseq, rows, w), lambda i, j: (i, j, 0))
    return pl.pallas_call(
        _gla_kernel,
        grid=(b // nseq, s // rows),
        in_specs=[blk(GLA_QK), blk(GLA_QK), blk(GLA_WIDTH), blk(GLA_QK), blk(GLA_WIDTH),
                  _const_spec((1, GLA_DV))],
        out_specs=blk(GLA_WIDTH),
        out_shape=jax.ShapeDtypeStruct((b, s, GLA_WIDTH), BF16),
        scratch_shapes=[pltpu.VMEM((nseq, GLA_HEADS, GLA_DV, LANES), F32),
                        pltpu.VMEM((CHUNK, GLA_QK), F32),
                        pltpu.VMEM((CHUNK, GLA_QK), F32)],
        compiler_params=pltpu.CompilerParams(
            dimension_semantics=("arbitrary", "arbitrary"), vmem_limit_bytes=VMEM_LIMIT_BYTES),
        name="gla",
    )(qg, kg, vg, la, gr, out_norm.reshape(1, GLA_DV))


def _mla_tiles(n):
    pairs = [(i, i) for i in range(n)] + [(i, j) for i in range(1, n) for j in range(i)]
    return [i for i, _ in pairs], [j for _, j in pairs]


def _mla_kernel(ti_ref, tj_ref, q_ref, k_ref, v_ref, o_ref, s_ref, m_ref, acc_ref, v1_ref):
    seq = q_ref.shape[1]
    tq = TQ
    n = seq // tq
    n_tiles = n * (n + 1) // 2
    nt = (((1,), (1,)), ((), ()))
    heads = range(2)
    lane = lax.broadcasted_iota(jnp.int32, (1, LANES), 1)
    own = [(lane // MLA_V) == e for e in heads]

    def rows(t):
        return pl.ds(pl.multiple_of(t * tq, tq), tq)

    def logits(e, i, j):
        hsl = slice(e * LANES, (e + 1) * LANES)
        return lax.dot_general(q_ref[0, rows(i), hsl], k_ref[0, rows(j), hsl], nt,
                               preferred_element_type=F32)

    def lane_fold_max(a):
        out = a[:, :LANES]
        for c in range(1, tq // LANES):
            out = jnp.maximum(out, a[:, c * LANES:(c + 1) * LANES])
        return out

    v = v_ref[0]
    for e in heads:
        v1_ref[e] = jnp.where(own[e], v, 1.0).astype(BF16)

    def diag(i, carry):
        row_chunk = lax.broadcasted_iota(jnp.int32, (tq, tq), 0) // CHUNK
        col_chunk = lax.broadcasted_iota(jnp.int32, (tq, tq), 1) // CHUNK
        for e in heads:
            s = jnp.where(col_chunk <= row_chunk, logits(e, i, i), MASKED_LOGIT)
            s_ref[e, i] = s
            m_ref[e, i] = lane_fold_max(s)
            acc_ref[e, i] = jnp.zeros((tq, LANES), F32)
        return carry

    lax.fori_loop(0, n, diag, 0, unroll=MLA_UNROLL)

    def pass1(t, carry):
        i, j = ti_ref[t], tj_ref[t]
        for e in heads:
            s = logits(e, i, j)
            s_ref[e, t] = s
            m_ref[e, i] = jnp.maximum(m_ref[e, i], lane_fold_max(s))
        return carry

    lax.fori_loop(n, n_tiles, pass1, 0, unroll=MLA_UNROLL)

    def row_max(i, carry):
        for e in heads:
            m_ref[e, i] = jnp.broadcast_to(jnp.max(m_ref[e, i], axis=-1, keepdims=True), (tq, LANES))
        return carry

    lax.fori_loop(0, n, row_max, 0, unroll=2)

    def pass2(t, carry):
        i, j = ti_ref[t], tj_ref[t]
        for e in heads:
            s = s_ref[e, t]
            m = m_ref[e, i]
            p = jnp.concatenate(
                [jnp.exp2(s[:, c * LANES:(c + 1) * LANES] - m) for c in range(tq // LANES)], axis=1)
            acc_ref[e, i] += jnp.dot(p.astype(BF16), v1_ref[e, rows(j), :], preferred_element_type=F32)
        return carry

    lax.fori_loop(0, n_tiles, pass2, 0, unroll=MLA_UNROLL)

    def finish(i, carry):
        acc0, acc1 = acc_ref[0, i], acc_ref[1, i]
        num = jnp.where(own[0], acc0, acc1)
        den = pltpu.roll(jnp.where(own[0], acc1, acc0), LANES // 2, 1)
        o_ref[0, rows(i), :] = (num / den).astype(BF16)
        return carry

    lax.fori_loop(0, n, finish, 0, unroll=2)


def _mla(q, k, v):
    b, s, _ = q.shape
    tq = TQ
    assert s % (2 * tq) == 0
    n = s // tq
    ti, tj = _mla_tiles(n)
    blk = lambda w: pl.BlockSpec((1, s, w), lambda i, p, *_: (i, 0, p))
    return pl.pallas_call(
        _mla_kernel,
        grid_spec=pltpu.PrefetchScalarGridSpec(
            num_scalar_prefetch=2,
            grid=(b, MLA_HEADS // 2),
            in_specs=[blk(2 * LANES), blk(2 * LANES), blk(LANES)],
            out_specs=blk(LANES),
            scratch_shapes=[pltpu.VMEM((2, len(ti), tq, tq), F32),
                            pltpu.VMEM((2, n, tq, LANES), F32),
                            pltpu.VMEM((2, n, tq, LANES), F32),
                            pltpu.VMEM((2, s, LANES), BF16)]),
        out_shape=jax.ShapeDtypeStruct((b, s, MLA_WIDTH), BF16),
        compiler_params=pltpu.CompilerParams(
            dimension_semantics=("arbitrary", "arbitrary"),
            vmem_limit_bytes=VMEM_LIMIT_BYTES),
        name="mla",
    )(jnp.asarray(ti, jnp.int32), jnp.asarray(tj, jnp.int32), q, k, v)


def _tail_kernel(x_ref, og_ref, om_ref, p_ref, w_out_ref, g_mlp_ref, w_up_ref, w_down_ref,
                 g_ple_ref, w_gate_ref, b_gate_ref, w_pp_ref, o_ref):
    h = (x_ref[...]
         + jnp.dot(og_ref[...], w_out_ref[:GLA_WIDTH, :], preferred_element_type=F32)
         + jnp.dot(om_ref[...], w_out_ref[GLA_WIDTH:, :], preferred_element_type=F32))
    m = (h * _rms_scale(h, D_MODEL) * g_mlp_ref[...]).astype(BF16)
    mlp = None
    for c in range(D_FF // FF_CHUNK):
        cols = slice(c * FF_CHUNK, (c + 1) * FF_CHUNK)
        u = jnp.maximum(jnp.dot(m, w_up_ref[:, cols], preferred_element_type=F32), 0.0)
        d = jnp.dot((u * u).astype(BF16), w_down_ref[cols, :], preferred_element_type=F32)
        mlp = d if mlp is None else mlp + d
    h = h + mlp
    g = (h * _rms_scale(h, D_MODEL) * g_ple_ref[...]).astype(BF16)
    gate = jax.nn.sigmoid(jnp.dot(g, w_gate_ref[...], preferred_element_type=F32) + b_gate_ref[...])
    pp = jnp.dot(p_ref[...].astype(BF16), w_pp_ref[...], preferred_element_type=F32)
    o_ref[...] = h + pp * gate


def _tail(x2, og, om, p2, w_out, mlp_norm, w_up, w_down, ple_norm, w_gate, b_gate, w_pp):
    t = x2.shape[0]
    tm = TM_TAIL
    assert t % tm == 0
    row = lambda w: pl.BlockSpec((tm, w), lambda i: (i, 0))
    return pl.pallas_call(
        _tail_kernel,
        grid=(t // tm,),
        in_specs=[row(D_MODEL), row(GLA_WIDTH), row(MLA_WIDTH), row(PLE_DIM),
                  _const_spec((D_MODEL, D_MODEL)), _const_spec((1, D_MODEL)),
                  _const_spec((D_MODEL, D_FF)), _const_spec((D_FF, D_MODEL)),
                  _const_spec((1, D_MODEL)), _const_spec((D_MODEL, D_MODEL)),
                  _const_spec((1, D_MODEL)), _const_spec((PLE_DIM, D_MODEL))],
        out_specs=row(D_MODEL),
        out_shape=jax.ShapeDtypeStruct((t, D_MODEL), F32),
        compiler_params=pltpu.CompilerParams(
            dimension_semantics=("arbitrary",), vmem_limit_bytes=VMEM_LIMIT_BYTES),
        name="tail",
    )(x2, og, om, p2, w_out.astype(BF16), mlp_norm.reshape(1, D_MODEL), w_up.astype(BF16),
      w_down.astype(BF16), ple_norm.reshape(1, D_MODEL), w_gate.astype(BF16),
      b_gate.reshape(1, D_MODEL), w_pp.astype(BF16))


def kernel(x, p, positions, attn_norm, w_in, gla_gate_w2, gla_gate_b, gla_out_norm, mla_q_norm,
           mla_w_uq, mla_kv_norm, mla_w_ukv, qk_norm_q, qk_norm_k, w_out, mlp_norm, w_mlp_up,
           w_mlp_down, ple_norm, w_ple_gate, b_ple_gate, w_ple_proj):
    b, s, d = x.shape
    depth = w_in.shape[0]
    t = b * s
    pos2 = positions.reshape(t, 1)
    h = x.reshape(t, d)
    for i in range(depth):
        qg, kg, vg, la, gr, q, k, v = _projection(
            h, pos2, attn_norm[i], w_in[i], gla_gate_w2[i], gla_gate_b[i], mla_q_norm[i],
            mla_w_uq[i], mla_kv_norm[i], mla_w_ukv[i], qk_norm_q[i], qk_norm_k[i])
        seq = lambda a: a.reshape(b, s, a.shape[-1])
        og = _gla(seq(qg), seq(kg), seq(vg), seq(la), seq(gr), gla_out_norm[i])
        om = _mla(seq(q), seq(k), seq(v))
        h = _tail(h, og.reshape(t, GLA_WIDTH), om.reshape(t, MLA_WIDTH), p[i].reshape(t, PLE_DIM),
                  w_out[i], mlp_norm[i], w_mlp_up[i], w_mlp_down[i], ple_norm[i], w_ple_gate[i],
                  b_ple_gate[i], w_ple_proj[i])
    return h.reshape(b, s, d)
```

```python
import functools
import math

import jax
import jax.numpy as jnp
from jax import lax
from jax.experimental import pallas as pl
from jax.experimental.pallas import tpu as pltpu

F32 = jnp.float32
BF16 = jnp.bfloat16

D_MODEL = 1024
CHUNK = 64
PLE_DIM = 256
D_FF = 4 * D_MODEL
EPS = 1e-6
GLA_HEADS = 4
GLA_DK = 64
GLA_DV = 128
GLA_GATE_RANK = 16
GLA_TAU = 16.0
GLA_QK = GLA_HEADS * GLA_DK
GLA_WIDTH = GLA_HEADS * GLA_DV
MLA_HEADS = 8
MLA_NOPE = 64
MLA_ROPE = 32
MLA_V = 64
MLA_QK = MLA_NOPE + MLA_ROPE
MLA_Q_RANK = 256
MLA_KV_RANK = 128
MLA_WIDTH = MLA_HEADS * MLA_V
ROPE_THETA = 10000.0
LOG2E = math.log2(math.e)
IN_SPLITS = (GLA_QK, GLA_QK, GLA_WIDTH, GLA_GATE_RANK, GLA_WIDTH, MLA_Q_RANK, MLA_KV_RANK, MLA_ROPE)

LANES = 128
SUBLANES = 8
HALF_ROPE = MLA_ROPE // 2
HEAD_PAD = MLA_HEADS * LANES
D_IN_PAD = 2048
VMEM_LIMIT_BYTES = 56 * 1024 * 1024

TM_PROJ = 512
TM_TAIL = 512
FF_CHUNK = 1024
TQ = 512
TK = 256
MLA_UNROLL = 4
ROPE_ROWS = 64
GLA_SEQS_PER_STEP = (8, 4, 2, 1)
GLA_ROWS_PER_STEP = 256
GLA_FAST_MIN_LOG_DECAY = -80.0
MASKED_LOGIT = -1e30
MLA_MAX_SHIFT = 60

ROW_FREQ16, ROW_MASK96, ROW_KRMASK, ROW_GQ, ROW_GQ_SW, ROW_GK, ROW_GK_SW = range(7)


def _const_spec(shape):
    return pl.BlockSpec(shape, lambda *_: (0,) * len(shape), pipeline_mode=pl.Buffered(1))


def _rms_scale(v, n):
    return lax.rsqrt(jnp.sum(v * v, axis=-1, keepdims=True) * (1.0 / n) + EPS)


def _proj_kernel(x_ref, pos_ref, g_attn_ref, w_in_ref, w2_ref, b2_ref, gqn_ref, w_uq_ref,
                 gkvn_ref, w_uk_ref, w_uv_ref, tab_ref,
                 qg_ref, kg_ref, vg_ref, la_ref, gr_ref, q_ref, k_ref, v_ref,
                 qs_ref, ks_ref, slot_ref):
    x = x_ref[...]
    xn = (x * _rms_scale(x, D_MODEL) * g_attn_ref[...]).astype(BF16)

    def proj(lo, hi):
        return jnp.dot(xn, w_in_ref[:, lo:hi], preferred_element_type=F32)

    zqk = proj(0, 2 * GLA_QK)
    qg_ref[...] = (zqk[:, :GLA_QK] * (GLA_DK ** -0.5)).astype(BF16)
    kg_ref[...] = zqk[:, GLA_QK:].astype(BF16)
    vg_ref[...] = proj(512, 1024).astype(BF16)
    gr = proj(1024, 1536)
    gr_ref[...] = (gr * jax.nn.sigmoid(gr)).astype(BF16)
    zc = proj(1536, 2048)
    cq = zc[:, :MLA_Q_RANK]
    ckv = zc[:, MLA_Q_RANK:MLA_Q_RANK + MLA_KV_RANK]
    slot = zc[:, MLA_Q_RANK + MLA_KV_RANK:]
    slot_ref[...] = slot

    xg = jnp.dot(slot.astype(BF16), w2_ref[...], preferred_element_type=F32) + b2_ref[...]
    la_ref[...] = (jnp.minimum(xg, 0.0) - jnp.log(1.0 + jnp.exp(-jnp.abs(xg)))) * (1.0 / GLA_TAU)

    cqn = (cq * _rms_scale(cq, MLA_Q_RANK) * gqn_ref[...]).astype(BF16)
    qs_ref[...] = jnp.dot(cqn, w_uq_ref[...], preferred_element_type=F32)
    ckvn = (ckv * _rms_scale(ckv, MLA_KV_RANK) * gkvn_ref[...]).astype(BF16)
    ks_ref[...] = jnp.dot(ckvn, w_uk_ref[...], preferred_element_type=F32)
    v_ref[...] = jnp.dot(ckvn, w_uv_ref[...], preferred_element_type=F32).astype(BF16)

    tab = lambda r: tab_ref[r:r + 1, :]
    mask96, krmask = tab(ROW_MASK96), tab(ROW_KRMASK)
    gq, gq_sw = tab(ROW_GQ) * LOG2E, tab(ROW_GQ_SW) * LOG2E
    gk, gk_sw = tab(ROW_GK) * math.sqrt(MLA_QK), tab(ROW_GK_SW) * math.sqrt(MLA_QK)
    half = LANES // 2

    lane = lax.broadcasted_iota(jnp.int32, (1, LANES), 1)
    groups = ROPE_ROWS // SUBLANES
    assert groups * HALF_ROPE == LANES
    x1_lanes = (lane >= 32) & (lane < 32 + HALF_ROPE)
    x2_lanes = (lane >= 32 + HALF_ROPE) & (lane < 64)

    def rope_tables(rows):
        pos = jnp.broadcast_to(pos_ref[rows, :].astype(F32), (ROPE_ROWS, LANES))
        packed = jnp.zeros((SUBLANES, LANES), F32)
        for a in range(groups):
            packed = jnp.where(lane // HALF_ROPE == a, pos[a * SUBLANES:(a + 1) * SUBLANES], packed)
        ang = packed * tab(ROW_FREQ16)
        cos_p, sin_p = jnp.cos(ang), jnp.sin(ang)
        cos, sin = [], []
        for a in range(groups):
            s1 = (32 - HALF_ROPE * a) % LANES
            s2 = (32 + HALF_ROPE - HALF_ROPE * a) % LANES
            c1, c2 = pltpu.roll(cos_p, s1, 1), pltpu.roll(cos_p, s2, 1)
            n1, n2 = pltpu.roll(sin_p, s1, 1), pltpu.roll(sin_p, s2, 1)
            cos.append(jnp.where(x1_lanes, c1, jnp.where(x2_lanes, c2, mask96)))
            sin.append(jnp.where(x1_lanes, -n1, jnp.where(x2_lanes, n2, 0.0)))
        return jnp.concatenate(cos, axis=0), jnp.concatenate(sin, axis=0)

    sum96 = (lax.broadcasted_iota(jnp.int32, (LANES, LANES), 0) < MLA_QK).astype(BF16)

    def head_rms(v):
        ssq = jnp.dot((v * v).astype(BF16), sum96, preferred_element_type=F32)
        return lax.rsqrt(ssq + MLA_QK * EPS)

    def head_rows(rb, carry):
        rows = pl.ds(pl.multiple_of(rb * ROPE_ROWS, ROPE_ROWS), ROPE_ROWS)
        cos, sin = rope_tables(rows)
        kr = slot_ref[rows, :] * krmask
        kr_rot = pltpu.roll(kr, half, 1) * (sin * gk_sw)
        cos_q, sin_q, cos_k = cos * gq, sin * gq_sw, cos * gk
        for h in range(MLA_HEADS):
            sl = slice(h * LANES, (h + 1) * LANES)
            qh = qs_ref[rows, sl]
            q_ref[rows, sl] = ((qh * cos_q + pltpu.roll(qh, half, 1) * sin_q) * head_rms(qh)).astype(BF16)
            kh = ks_ref[rows, sl] + kr
            k_ref[rows, sl] = ((kh * cos_k + kr_rot) * head_rms(kh)).astype(BF16)
        return carry

    lax.fori_loop(0, x_ref.shape[0] // ROPE_ROWS, head_rows, 0, unroll=2)


def _head_layout(nope, rope):
    x1, x2 = rope[..., :HALF_ROPE], rope[..., HALF_ROPE:]
    return jnp.concatenate([nope[..., :32], x1, x2, nope[..., 32:], x2, x1], axis=-1)


def _projection(x2, pos2, attn_norm, w_in, gate_w2, gate_b, q_norm, w_uq, kv_norm, w_ukv,
                qk_norm_q, qk_norm_k):
    t = x2.shape[0]
    tm = TM_PROJ
    assert t % tm == 0

    gq, gk, gv, g_low, g_r, c_q, c_kv, k_r = jnp.split(w_in, list(_cumsum(IN_SPLITS))[:-1], axis=1)
    zeros = lambda n: jnp.zeros((D_MODEL, n), w_in.dtype)
    kr_sw = jnp.concatenate([k_r[:, HALF_ROPE:], k_r[:, :HALF_ROPE]], axis=1)
    slot = jnp.concatenate([g_low, zeros(32 - GLA_GATE_RANK), k_r, zeros(32), kr_sw], axis=1)
    w_in_l = jnp.concatenate([gq, gk, gv, g_r, c_q, c_kv, slot], axis=1).astype(BF16)
    assert w_in_l.shape == (D_MODEL, D_IN_PAD)

    w2_l = jnp.zeros((LANES, GLA_QK), F32).at[:GLA_GATE_RANK].set(gate_w2).astype(BF16)

    w_uq_h = w_uq.reshape(MLA_Q_RANK, MLA_HEADS, MLA_QK)
    w_uq_l = _head_layout(w_uq_h[..., :MLA_NOPE], w_uq_h[..., MLA_NOPE:]).reshape(MLA_Q_RANK, HEAD_PAD)
    w_ukv_h = w_ukv.reshape(MLA_KV_RANK, MLA_HEADS, MLA_NOPE + MLA_V)
    w_uk_l = _head_layout(w_ukv_h[..., :MLA_NOPE],
                          jnp.zeros((MLA_KV_RANK, MLA_HEADS, MLA_ROPE), F32)).reshape(MLA_KV_RANK, HEAD_PAD)
    w_uv = w_ukv_h[..., MLA_NOPE:].reshape(MLA_KV_RANK, MLA_WIDTH)
    gq_l = _head_layout(qk_norm_q[:MLA_NOPE], qk_norm_q[MLA_NOPE:])
    gk_l = _head_layout(qk_norm_k[:MLA_NOPE], qk_norm_k[MLA_NOPE:])

    inv_freq = ROPE_THETA ** (-jnp.arange(0, MLA_ROPE, 2, dtype=F32) / MLA_ROPE)
    z32 = jnp.zeros((32,), F32)
    tab = jnp.zeros((SUBLANES, LANES), F32)
    tab = tab.at[ROW_FREQ16].set(jnp.tile(inv_freq, LANES // HALF_ROPE))
    tab = tab.at[ROW_MASK96].set(jnp.concatenate([jnp.ones((96,), F32), z32]))
    tab = tab.at[ROW_KRMASK].set(jnp.concatenate([z32, jnp.ones((32,), F32), z32, jnp.ones((32,), F32)]))
    tab = tab.at[ROW_GQ].set(gq_l).at[ROW_GQ_SW].set(jnp.roll(gq_l, LANES // 2))
    tab = tab.at[ROW_GK].set(gk_l).at[ROW_GK_SW].set(jnp.roll(gk_l, LANES // 2))

    row = lambda w: pl.BlockSpec((tm, w), lambda i: (i, 0))
    out_widths = (GLA_QK, GLA_QK, GLA_WIDTH, GLA_QK, GLA_WIDTH, HEAD_PAD, HEAD_PAD, MLA_WIDTH)
    out_dtypes = (BF16, BF16, BF16, F32, BF16, BF16, BF16, BF16)
    return pl.pallas_call(
        _proj_kernel,
        grid=(t // tm,),
        in_specs=[
            row(D_MODEL), row(1),
            _const_spec((1, D_MODEL)), _const_spec((D_MODEL, D_IN_PAD)),
            _const_spec((LANES, GLA_QK)), _const_spec((1, GLA_QK)),
            _const_spec((1, MLA_Q_RANK)), _const_spec((MLA_Q_RANK, HEAD_PAD)),
            _const_spec((1, MLA_KV_RANK)), _const_spec((MLA_KV_RANK, HEAD_PAD)),
            _const_spec((MLA_KV_RANK, MLA_WIDTH)),
            _const_spec((8, LANES)),
        ],
        out_specs=[row(w) for w in out_widths],
        out_shape=[jax.ShapeDtypeStruct((t, w), d) for w, d in zip(out_widths, out_dtypes)],
        scratch_shapes=[pltpu.VMEM((tm, HEAD_PAD), F32), pltpu.VMEM((tm, HEAD_PAD), F32),
                        pltpu.VMEM((tm, LANES), F32)],
        compiler_params=pltpu.CompilerParams(
            dimension_semantics=("arbitrary",), vmem_limit_bytes=VMEM_LIMIT_BYTES),
        name="proj",
    )(x2, pos2, attn_norm.reshape(1, D_MODEL), w_in_l, w2_l, gate_b.reshape(1, GLA_QK),
      q_norm.reshape(1, MLA_Q_RANK), w_uq_l.astype(BF16), kv_norm.reshape(1, MLA_KV_RANK),
      w_uk_l.astype(BF16), w_uv.astype(BF16), tab)


def _cumsum(sizes):
    total = 0
    for s in sizes:
        total += s
        yield total


def _gla_kernel(q_ref, k_ref, v_ref, la_ref, gr_ref, gon_ref, o_ref, st_ref, kf_ref, b_ref):
    nseq, rows_per_step = q_ref.shape[0], q_ref.shape[1]

    @pl.when(pl.program_id(1) == 0)
    def _():
        st_ref[...] = jnp.zeros_like(st_ref)

    row = lax.broadcasted_iota(jnp.int32, (CHUNK, CHUNK), 0)
    col = lax.broadcasted_iota(jnp.int32, (CHUNK, CHUNK), 1)
    causal = col <= row
    tril = causal.astype(BF16)
    lane = lax.broadcasted_iota(jnp.int32, (1, LANES), 1)
    lane_q = lax.broadcasted_iota(jnp.int32, (1, GLA_QK), 1)
    head_mask = [(lane // GLA_DK) == e for e in range(2)]
    gon = gon_ref[...]
    nt = (((1,), (1,)), ((), ()))
    tn = (((0,), (0,)), ((), ()))

    def exact_scores(q, k, b):
        kf_ref[...] = k
        b_ref[...] = b

        def key_row(s, sc):
            k_s = kf_ref[pl.ds(s, 1), :]
            b_s = b_ref[pl.ds(s, 1), :]
            prod = q * k_s * jnp.exp(jnp.minimum(b - b_s, 0.0))
            out = []
            for h in range(GLA_HEADS):
                hm = (lane_q // GLA_DK) == h
                col_h = jnp.sum(jnp.where(hm, prod, 0.0), axis=-1, keepdims=True)
                out.append(jnp.where(col == s, col_h, sc[h]))
            return tuple(out)

        zero = jnp.zeros((CHUNK, CHUNK), F32)
        return lax.fori_loop(0, CHUNK, key_row, (zero,) * GLA_HEADS)

    pair = lambda h: slice((h // 2) * LANES, (h // 2 + 1) * LANES)
    vsl = lambda h: slice(h * GLA_DV, (h + 1) * GLA_DV)
    heads = range(GLA_HEADS)

    def chunk(seqs, c, factorised):
        rows = pl.ds(pl.multiple_of(c * CHUNK, CHUNK), CHUNK)
        pieces = []
        for n in seqs:
            la = la_ref[n, rows, :]
            hi = la.astype(BF16)
            r1 = la - hi.astype(F32)
            mid = r1.astype(BF16)
            pieces += [hi, mid, (r1 - mid.astype(F32)).astype(BF16)]
        cum = jnp.dot(tril, jnp.concatenate(pieces, axis=1), preferred_element_type=F32)
        q, k, b, a_h, kd, eb_last, scores = {}, {}, {}, {}, {}, {}, {}
        for i, n in enumerate(seqs):
            parts = [cum[:, (3 * i + j) * GLA_QK:(3 * i + j + 1) * GLA_QK] for j in range(3)]
            b[n] = parts[0] + parts[1] + parts[2]
            b_last = b[n][CHUNK - 1:CHUNK, :]
            q[n] = q_ref[n, rows, :].astype(F32)
            k[n] = k_ref[n, rows, :].astype(F32)
            qt = q[n] * jnp.exp(b[n])
            kd[n] = k[n] * jnp.exp(b_last - b[n])
            eb_last[n] = jnp.exp(b_last)
            a_h[n] = [jnp.where(head_mask[h % 2], qt[:, pair(h)], 0.0).astype(BF16) for h in heads]
        for n in seqs:
            if factorised:
                kt = (k[n] * jnp.exp(-b[n])).astype(BF16)
                scores[n] = [lax.dot_general(a_h[n][h], kt[:, pair(h)], nt, preferred_element_type=F32)
                             for h in heads]
            else:
                scores[n] = exact_scores(q[n], k[n], b[n])
        v, st, o = {}, {}, {}
        for n in seqs:
            for h in heads:
                sc = jnp.where(causal, scores[n][h], 0.0).astype(BF16)
                v[n, h] = v_ref[n, rows, vsl(h)]
                st[n, h] = st_ref[n, h]
                o[n, h] = (lax.dot_general(a_h[n][h], st[n, h].astype(BF16), nt, preferred_element_type=F32)
                           + jnp.dot(sc, v[n, h], preferred_element_type=F32))
        for n in seqs:
            for h in heads:
                kd_h = jnp.where(head_mask[h % 2], kd[n][:, pair(h)], 0.0).astype(BF16)
                upd = lax.dot_general(v[n, h], kd_h, tn, preferred_element_type=F32)
                st_ref[n, h] = st[n, h] * eb_last[n][:, pair(h)] + upd
        for n in seqs:
            for h in heads:
                on = o[n, h] * _rms_scale(o[n, h], GLA_DV) * gon
                o_ref[n, rows, vsl(h)] = (on * gr_ref[n, rows, vsl(h)].astype(F32)).astype(BF16)

    def factorised_chunks():
        def body(c, carry):
            chunk(range(nseq), c, True)
            return carry
        lax.fori_loop(0, rows_per_step // CHUNK, body, 0)

    def exact_chunks():
        def body(c, carry):
            for n in range(nseq):
                chunk([n], c, False)
            return carry
        lax.fori_loop(0, rows_per_step // CHUNK, body, 0)

    factorisable = jnp.min(la_ref[...]) >= GLA_FAST_MIN_LOG_DECAY / CHUNK
    lax.cond(factorisable, factorised_chunks, exact_chunks)


def _gla(qg, kg, vg, la, gr, out_norm):
    b, s, _ = qg.shape
    nseq = max(n for n in GLA_SEQS_PER_STEP if b % n == 0)
    rows = min(GLA_ROWS_PER_STEP, s)
    assert s % rows == 0 and rows % CHUNK == 0
    blk = lambda w: pl.BlockSpec((nseq, rows, w), lambda i, j: (i, j, 0))
    return pl.pallas_call(
        _gla_kernel,
        grid=(b // nseq, s // rows),
        in_specs=[blk(GLA_QK), blk(GLA_QK), blk(GLA_WIDTH), blk(GLA_QK), blk(GLA_WIDTH),
                  _const_spec((1, GLA_DV))],
        out_specs=blk(GLA_WIDTH),
        out_shape=jax.ShapeDtypeStruct((b, s, GLA_WIDTH), BF16),
        scratch_shapes=[pltpu.VMEM((nseq, GLA_HEADS, GLA_DV, LANES), F32),
                        pltpu.VMEM((CHUNK, GLA_QK), F32),
                        pltpu.VMEM((CHUNK, GLA_QK), F32)],
        compiler_params=pltpu.CompilerParams(
            dimension_semantics=("arbitrary", "arbitrary"), vmem_limit_bytes=VMEM_LIMIT_BYTES),
        name="gla",
    )(qg, kg, vg, la, gr, out_norm.reshape(1, GLA_DV))


def _mla_tiles(n_q):
    ratio = TQ // TK
    first = [(a, a * ratio, 0) for a in range(n_q)]
    rest = []
    for a in range(n_q):
        rest += [(a, j, (a * ratio - j) * (TK // CHUNK)) for j in range(a * ratio + 1, (a + 1) * ratio)]
        rest += [(a, j, TQ // CHUNK) for j in range(a * ratio)]
    return tuple(zip(*(first + rest)))


def _mla_kernel(ti_ref, tj_ref, toff_ref, bound_ref, q_ref, k_ref, v_ref, o_ref,
                s_ref, m_ref, acc_ref, v1_ref):
    seq = q_ref.shape[1]
    n_q = seq // TQ
    n_tiles = s_ref.shape[1]
    nt = (((1,), (1,)), ((), ()))
    heads = range(2)
    lane = lax.broadcasted_iota(jnp.int32, (1, LANES), 1)
    own = [(lane // MLA_V) == e for e in heads]
    unroll = lambda count: math.gcd(count, MLA_UNROLL)

    def q_rows(i):
        return pl.ds(pl.multiple_of(i * TQ, TQ), TQ)

    def k_rows(j):
        return pl.ds(pl.multiple_of(j * TK, TK), TK)

    def masked_logits(e, t):
        hsl = slice(e * LANES, (e + 1) * LANES)
        s = lax.dot_general(q_ref[0, q_rows(ti_ref[t]), hsl], k_ref[0, k_rows(tj_ref[t]), hsl], nt,
                            preferred_element_type=F32)
        row_chunk = lax.broadcasted_iota(jnp.int32, (TQ, TK), 0) // CHUNK
        col_chunk = lax.broadcasted_iota(jnp.int32, (TQ, TK), 1) // CHUNK
        return jnp.where(col_chunk <= row_chunk + toff_ref[t], s, MASKED_LOGIT)

    def lane_fold_max(a):
        out = a[:, :LANES]
        for c in range(1, TK // LANES):
            out = jnp.maximum(out, a[:, c * LANES:(c + 1) * LANES])
        return out

    v = v_ref[0]
    for e in heads:
        v1_ref[e] = jnp.where(own[e], v, 1.0).astype(BF16)

    def accumulate(e, t, s, shift):
        p = jnp.concatenate(
            [jnp.exp2(s[:, c * LANES:(c + 1) * LANES] - shift) for c in range(TK // LANES)], axis=1)
        acc_ref[e, ti_ref[t]] += jnp.dot(p.astype(BF16), v1_ref[e, k_rows(tj_ref[t]), :],
                                         preferred_element_type=F32)

    def clear(i, carry):
        for e in heads:
            acc_ref[e, i] = jnp.zeros((TQ, LANES), F32)
        return carry

    def single_pass():
        lax.fori_loop(0, n_q, clear, 0)
        shift = bound_ref[0].astype(F32)

        def tile(t, carry):
            for e in heads:
                accumulate(e, t, masked_logits(e, t), shift)
            return carry

        lax.fori_loop(0, n_tiles, tile, 0, unroll=unroll(n_tiles))

    def two_pass():
        def first_tile(t, carry):
            for e in heads:
                s = masked_logits(e, t)
                s_ref[e, t] = s
                m_ref[e, t] = lane_fold_max(s)
            return clear(t, carry)

        lax.fori_loop(0, n_q, first_tile, 0, unroll=unroll(n_q))

        def pass1(t, carry):
            i = ti_ref[t]
            for e in heads:
                s = masked_logits(e, t)
                s_ref[e, t] = s
                m_ref[e, i] = jnp.maximum(m_ref[e, i], lane_fold_max(s))
            return carry

        lax.fori_loop(n_q, n_tiles, pass1, 0, unroll=unroll(n_tiles - n_q))

        def row_max(i, carry):
            for e in heads:
                m_ref[e, i] = jnp.broadcast_to(jnp.max(m_ref[e, i], axis=-1, keepdims=True), (TQ, LANES))
            return carry

        lax.fori_loop(0, n_q, row_max, 0)

        def pass2(t, carry):
            for e in heads:
                accumulate(e, t, s_ref[e, t], m_ref[e, ti_ref[t]])
            return carry

        lax.fori_loop(0, n_tiles, pass2, 0, unroll=unroll(n_tiles))

    lax.cond(bound_ref[0] <= MLA_MAX_SHIFT, single_pass, two_pass)

    def finish(i, carry):
        acc0, acc1 = acc_ref[0, i], acc_ref[1, i]
        num = jnp.where(own[0], acc0, acc1)
        den = pltpu.roll(jnp.where(own[0], acc1, acc0), LANES // 2, 1)
        o_ref[0, q_rows(i), :] = (num / den).astype(BF16)
        return carry

    lax.fori_loop(0, n_q, finish, 0)


def _mla(q, k, v, gain_q, gain_k):
    b, s, _ = q.shape
    assert s % TQ == 0 and TQ % TK == 0
    n_q = s // TQ
    ti, tj, toff = _mla_tiles(n_q)
    bound = jnp.ceil(1.02 * LOG2E * math.sqrt(MLA_QK) * jnp.max(jnp.abs(gain_q)) * jnp.max(jnp.abs(gain_k)))
    bound = jnp.minimum(bound, 2.0 * MLA_MAX_SHIFT).astype(jnp.int32).reshape(1)
    blk = lambda w: pl.BlockSpec((1, s, w), lambda i, p, *_: (i, 0, p))
    return pl.pallas_call(
        _mla_kernel,
        grid_spec=pltpu.PrefetchScalarGridSpec(
            num_scalar_prefetch=4,
            grid=(b, MLA_HEADS // 2),
            in_specs=[blk(2 * LANES), blk(2 * LANES), blk(LANES)],
            out_specs=blk(LANES),
            scratch_shapes=[pltpu.VMEM((2, len(ti), TQ, TK), F32),
                            pltpu.VMEM((2, n_q, TQ, LANES), F32),
                            pltpu.VMEM((2, n_q, TQ, LANES), F32),
                            pltpu.VMEM((2, s, LANES), BF16)]),
        out_shape=jax.ShapeDtypeStruct((b, s, MLA_WIDTH), BF16),
        compiler_params=pltpu.CompilerParams(
            dimension_semantics=("arbitrary", "arbitrary"),
            vmem_limit_bytes=VMEM_LIMIT_BYTES),
        name="mla",
    )(jnp.asarray(ti, jnp.int32), jnp.asarray(tj, jnp.int32), jnp.asarray(toff, jnp.int32), bound,
      q, k, v)


def _tail_kernel(x_ref, og_ref, om_ref, p_ref, w_out_ref, g_mlp_ref, w_up_ref, w_down_ref,
                 g_ple_ref, w_gate_ref, b_gate_ref, w_pp_ref, o_ref):
    h = (x_ref[...]
         + jnp.dot(og_ref[...], w_out_ref[:GLA_WIDTH, :], preferred_element_type=F32)
         + jnp.dot(om_ref[...], w_out_ref[GLA_WIDTH:, :], preferred_element_type=F32))
    m = (h * _rms_scale(h, D_MODEL) * g_mlp_ref[...]).astype(BF16)
    mlp = None
    for c in range(D_FF // FF_CHUNK):
        cols = slice(c * FF_CHUNK, (c + 1) * FF_CHUNK)
        u = jnp.maximum(jnp.dot(m, w_up_ref[:, cols], preferred_element_type=F32), 0.0)
        d = jnp.dot((u * u).astype(BF16), w_down_ref[cols, :], preferred_element_type=F32)
        mlp = d if mlp is None else mlp + d
    h = h + mlp
    g = (h * _rms_scale(h, D_MODEL) * g_ple_ref[...]).astype(BF16)
    gate = jax.nn.sigmoid(jnp.dot(g, w_gate_ref[...], preferred_element_type=F32) + b_gate_ref[...])
    pp = jnp.dot(p_ref[...].astype(BF16), w_pp_ref[...], preferred_element_type=F32)
    o_ref[...] = h + pp * gate


def _tail(x2, og, om, p2, w_out, mlp_norm, w_up, w_down, ple_norm, w_gate, b_gate, w_pp):
    t = x2.shape[0]
    tm = TM_TAIL
    assert t % tm == 0
    row = lambda w: pl.BlockSpec((tm, w), lambda i: (i, 0))
    return pl.pallas_call(
        _tail_kernel,
        grid=(t // tm,),
        in_specs=[row(D_MODEL), row(GLA_WIDTH), row(MLA_WIDTH), row(PLE_DIM),
                  _const_spec((D_MODEL, D_MODEL)), _const_spec((1, D_MODEL)),
                  _const_spec((D_MODEL, D_FF)), _const_spec((D_FF, D_MODEL)),
                  _const_spec((1, D_MODEL)), _const_spec((D_MODEL, D_MODEL)),
                  _const_spec((1, D_MODEL)), _const_spec((PLE_DIM, D_MODEL))],
        out_specs=row(D_MODEL),
        out_shape=jax.ShapeDtypeStruct((t, D_MODEL), F32),
        compiler_params=pltpu.CompilerParams(
            dimension_semantics=("arbitrary",), vmem_limit_bytes=VMEM_LIMIT_BYTES),
        name="tail",
    )(x2, og, om, p2, w_out.astype(BF16), mlp_norm.reshape(1, D_MODEL), w_up.astype(BF16),
      w_down.astype(BF16), ple_norm.reshape(1, D_MODEL), w_gate.astype(BF16),
      b_gate.reshape(1, D_MODEL), w_pp.astype(BF16))


def kernel(x, p, positions, attn_norm, w_in, gla_gate_w2, gla_gate_b, gla_out_norm, mla_q_norm,
           mla_w_uq, mla_kv_norm, mla_w_ukv, qk_norm_q, qk_norm_k, w_out, mlp_norm, w_mlp_up,
           w_mlp_down, ple_norm, w_ple_gate, b_ple_gate, w_ple_proj):
    b, s, d = x.shape
    depth = w_in.shape[0]
    t = b * s
    pos2 = positions.reshape(t, 1)
    h = x.reshape(t, d)
    for i in range(depth):
        qg, kg, vg, la, gr, q, k, v = _projection(
            h, pos2, attn_norm[i], w_in[i], gla_gate_w2[i], gla_gate_b[i], mla_q_norm[i],
            mla_w_uq[i], mla_kv_norm[i], mla_w_ukv[i], qk_norm_q[i], qk_norm_k[i])
        seq = lambda a: a.reshape(b, s, a.shape[-1])
        og = _gla(seq(qg), seq(kg), seq(vg), seq(la), seq(gr), gla_out_norm[i])
        om = _mla(seq(q), seq(k), seq(v), qk_norm_q[i], qk_norm_k[i])
        h = _tail(h, og.reshape(t, GLA_WIDTH), om.reshape(t, MLA_WIDTH), p[i].reshape(t, PLE_DIM),
                  w_out[i], mlp_norm[i], w_mlp_up[i], w_mlp_down[i], ple_norm[i], w_ple_gate[i],
                  b_ple_gate[i], w_ple_proj[i])
    return h.reshape(b, s, d)
```

```python
import functools
import math

import jax
import jax.numpy as jnp
from jax import lax
from jax.experimental import pallas as pl
from jax.experimental.pallas import tpu as pltpu

F32 = jnp.float32
BF16 = jnp.bfloat16

D_MODEL = 1024
CHUNK = 64
PLE_DIM = 256
D_FF = 4 * D_MODEL
EPS = 1e-6
GLA_HEADS = 4
GLA_DK = 64
GLA_DV = 128
GLA_GATE_RANK = 16
GLA_TAU = 16.0
GLA_QK = GLA_HEADS * GLA_DK
GLA_WIDTH = GLA_HEADS * GLA_DV
MLA_HEADS = 8
MLA_NOPE = 64
MLA_ROPE = 32
MLA_V = 64
MLA_QK = MLA_NOPE + MLA_ROPE
MLA_Q_RANK = 256
MLA_KV_RANK = 128
MLA_WIDTH = MLA_HEADS * MLA_V
ROPE_THETA = 10000.0
LOG2E = math.log2(math.e)
IN_SPLITS = (GLA_QK, GLA_QK, GLA_WIDTH, GLA_GATE_RANK, GLA_WIDTH, MLA_Q_RANK, MLA_KV_RANK, MLA_ROPE)

LANES = 128
SUBLANES = 8
HALF_ROPE = MLA_ROPE // 2
HEAD_PAD = MLA_HEADS * LANES
D_IN_PAD = 2048
VMEM_LIMIT_BYTES = 56 * 1024 * 1024

TM_PROJ = 512
TM_TAIL = 512
FF_CHUNK = 1024
TQ = 512
TK = 256
MLA_UNROLL = 8
ROPE_ROWS = 64
GLA_SEQS_PER_STEP = (8, 4, 2, 1)
GLA_ROWS_PER_STEP = 256
GLA_FAST_MIN_LOG_DECAY = -80.0
MASKED_LOGIT = -1e30
MLA_MAX_SHIFT = 60

ROW_FREQ16, ROW_MASK96, ROW_KRMASK, ROW_GQ, ROW_GQ_SW, ROW_GK, ROW_GK_SW = range(7)


def _const_spec(shape):
    return pl.BlockSpec(shape, lambda *_: (0,) * len(shape), pipeline_mode=pl.Buffered(1))


def _rms_scale(v, n):
    return lax.rsqrt(jnp.sum(v * v, axis=-1, keepdims=True) * (1.0 / n) + EPS)


def _proj_kernel(x_ref, pos_ref, g_attn_ref, w_in_ref, w2_ref, b2_ref, gqn_ref, w_uq_ref,
                 gkvn_ref, w_uk_ref, w_uv_ref, tab_ref,
                 qg_ref, kg_ref, vg_ref, la_ref, gr_ref, q_ref, k_ref, v_ref,
                 qs_ref, ks_ref, slot_ref):
    x = x_ref[...]
    xn = (x * _rms_scale(x, D_MODEL) * g_attn_ref[...]).astype(BF16)

    def proj(lo, hi):
        return jnp.dot(xn, w_in_ref[:, lo:hi], preferred_element_type=F32)

    zqk = proj(0, 2 * GLA_QK)
    qg_ref[...] = (zqk[:, :GLA_QK] * (GLA_DK ** -0.5)).astype(BF16)
    kg_ref[...] = zqk[:, GLA_QK:].astype(BF16)
    vg_ref[...] = proj(512, 1024).astype(BF16)
    gr = proj(1024, 1536)
    gr_ref[...] = (gr * jax.nn.sigmoid(gr)).astype(BF16)
    zc = proj(1536, 2048)
    cq = zc[:, :MLA_Q_RANK]
    ckv = zc[:, MLA_Q_RANK:MLA_Q_RANK + MLA_KV_RANK]
    slot = zc[:, MLA_Q_RANK + MLA_KV_RANK:]
    slot_ref[...] = slot

    xg = jnp.dot(slot.astype(BF16), w2_ref[...], preferred_element_type=F32) + b2_ref[...]
    la_ref[...] = (jnp.minimum(xg, 0.0) - jnp.log(1.0 + jnp.exp(-jnp.abs(xg)))) * (1.0 / GLA_TAU)

    cqn = (cq * _rms_scale(cq, MLA_Q_RANK) * gqn_ref[...]).astype(BF16)
    qs_ref[...] = jnp.dot(cqn, w_uq_ref[...], preferred_element_type=F32)
    ckvn = (ckv * _rms_scale(ckv, MLA_KV_RANK) * gkvn_ref[...]).astype(BF16)
    ks_ref[...] = jnp.dot(ckvn, w_uk_ref[...], preferred_element_type=F32)
    v_ref[...] = jnp.dot(ckvn, w_uv_ref[...], preferred_element_type=F32).astype(BF16)

    tab = lambda r: tab_ref[r:r + 1, :]
    mask96, krmask = tab(ROW_MASK96), tab(ROW_KRMASK)
    gq, gq_sw = tab(ROW_GQ) * LOG2E, tab(ROW_GQ_SW) * LOG2E
    gk, gk_sw = tab(ROW_GK) * math.sqrt(MLA_QK), tab(ROW_GK_SW) * math.sqrt(MLA_QK)
    half = LANES // 2

    lane = lax.broadcasted_iota(jnp.int32, (1, LANES), 1)
    groups = ROPE_ROWS // SUBLANES
    assert groups * HALF_ROPE == LANES
    x1_lanes = (lane >= 32) & (lane < 32 + HALF_ROPE)
    x2_lanes = (lane >= 32 + HALF_ROPE) & (lane < 64)

    def rope_tables(rows):
        pos = jnp.broadcast_to(pos_ref[rows, :].astype(F32), (ROPE_ROWS, LANES))
        packed = jnp.zeros((SUBLANES, LANES), F32)
        for a in range(groups):
            packed = jnp.where(lane // HALF_ROPE == a, pos[a * SUBLANES:(a + 1) * SUBLANES], packed)
        ang = packed * tab(ROW_FREQ16)
        cos_p, sin_p = jnp.cos(ang), jnp.sin(ang)
        cos, sin = [], []
        for a in range(groups):
            s1 = (32 - HALF_ROPE * a) % LANES
            s2 = (32 + HALF_ROPE - HALF_ROPE * a) % LANES
            c1, c2 = pltpu.roll(cos_p, s1, 1), pltpu.roll(cos_p, s2, 1)
            n1, n2 = pltpu.roll(sin_p, s1, 1), pltpu.roll(sin_p, s2, 1)
            cos.append(jnp.where(x1_lanes, c1, jnp.where(x2_lanes, c2, mask96)))
            sin.append(jnp.where(x1_lanes, -n1, jnp.where(x2_lanes, n2, 0.0)))
        return jnp.concatenate(cos, axis=0), jnp.concatenate(sin, axis=0)

    sum96 = (lax.broadcasted_iota(jnp.int32, (LANES, LANES), 0) < MLA_QK).astype(BF16)

    def head_rms(v):
        ssq = jnp.dot((v * v).astype(BF16), sum96, preferred_element_type=F32)
        return lax.rsqrt(ssq + MLA_QK * EPS)

    def head_rows(rb, carry):
        rows = pl.ds(pl.multiple_of(rb * ROPE_ROWS, ROPE_ROWS), ROPE_ROWS)
        cos, sin = rope_tables(rows)
        kr = slot_ref[rows, :] * krmask
        kr_rot = pltpu.roll(kr, half, 1) * (sin * gk_sw)
        cos_q, sin_q, cos_k = cos * gq, sin * gq_sw, cos * gk
        for h in range(MLA_HEADS):
            sl = slice(h * LANES, (h + 1) * LANES)
            qh = qs_ref[rows, sl]
            q_ref[rows, sl] = ((qh * cos_q + pltpu.roll(qh, half, 1) * sin_q) * head_rms(qh)).astype(BF16)
            kh = ks_ref[rows, sl] + kr
            k_ref[rows, sl] = ((kh * cos_k + kr_rot) * head_rms(kh)).astype(BF16)
        return carry

    lax.fori_loop(0, x_ref.shape[0] // ROPE_ROWS, head_rows, 0, unroll=2)


def _head_layout(nope, rope):
    x1, x2 = rope[..., :HALF_ROPE], rope[..., HALF_ROPE:]
    return jnp.concatenate([nope[..., :32], x1, x2, nope[..., 32:], x2, x1], axis=-1)


def _projection(x2, pos2, attn_norm, w_in, gate_w2, gate_b, q_norm, w_uq, kv_norm, w_ukv,
                qk_norm_q, qk_norm_k):
    t = x2.shape[0]
    tm = TM_PROJ
    assert t % tm == 0

    gq, gk, gv, g_low, g_r, c_q, c_kv, k_r = jnp.split(w_in, list(_cumsum(IN_SPLITS))[:-1], axis=1)
    zeros = lambda n: jnp.zeros((D_MODEL, n), w_in.dtype)
    kr_sw = jnp.concatenate([k_r[:, HALF_ROPE:], k_r[:, :HALF_ROPE]], axis=1)
    slot = jnp.concatenate([g_low, zeros(32 - GLA_GATE_RANK), k_r, zeros(32), kr_sw], axis=1)
    w_in_l = jnp.concatenate([gq, gk, gv, g_r, c_q, c_kv, slot], axis=1).astype(BF16)
    assert w_in_l.shape == (D_MODEL, D_IN_PAD)

    w2_l = jnp.zeros((LANES, GLA_QK), F32).at[:GLA_GATE_RANK].set(gate_w2).astype(BF16)

    w_uq_h = w_uq.reshape(MLA_Q_RANK, MLA_HEADS, MLA_QK)
    w_uq_l = _head_layout(w_uq_h[..., :MLA_NOPE], w_uq_h[..., MLA_NOPE:]).reshape(MLA_Q_RANK, HEAD_PAD)
    w_ukv_h = w_ukv.reshape(MLA_KV_RANK, MLA_HEADS, MLA_NOPE + MLA_V)
    w_uk_l = _head_layout(w_ukv_h[..., :MLA_NOPE],
                          jnp.zeros((MLA_KV_RANK, MLA_HEADS, MLA_ROPE), F32)).reshape(MLA_KV_RANK, HEAD_PAD)
    w_uv = w_ukv_h[..., MLA_NOPE:].reshape(MLA_KV_RANK, MLA_WIDTH)
    gq_l = _head_layout(qk_norm_q[:MLA_NOPE], qk_norm_q[MLA_NOPE:])
    gk_l = _head_layout(qk_norm_k[:MLA_NOPE], qk_norm_k[MLA_NOPE:])

    inv_freq = ROPE_THETA ** (-jnp.arange(0, MLA_ROPE, 2, dtype=F32) / MLA_ROPE)
    z32 = jnp.zeros((32,), F32)
    tab = jnp.zeros((SUBLANES, LANES), F32)
    tab = tab.at[ROW_FREQ16].set(jnp.tile(inv_freq, LANES // HALF_ROPE))
    tab = tab.at[ROW_MASK96].set(jnp.concatenate([jnp.ones((96,), F32), z32]))
    tab = tab.at[ROW_KRMASK].set(jnp.concatenate([z32, jnp.ones((32,), F32), z32, jnp.ones((32,), F32)]))
    tab = tab.at[ROW_GQ].set(gq_l).at[ROW_GQ_SW].set(jnp.roll(gq_l, LANES // 2))
    tab = tab.at[ROW_GK].set(gk_l).at[ROW_GK_SW].set(jnp.roll(gk_l, LANES // 2))

    row = lambda w: pl.BlockSpec((tm, w), lambda i: (i, 0))
    out_widths = (GLA_QK, GLA_QK, GLA_WIDTH, GLA_QK, GLA_WIDTH, HEAD_PAD, HEAD_PAD, MLA_WIDTH)
    out_dtypes = (BF16, BF16, BF16, F32, BF16, BF16, BF16, BF16)
    return pl.pallas_call(
        _proj_kernel,
        grid=(t // tm,),
        in_specs=[
            row(D_MODEL), row(1),
            _const_spec((1, D_MODEL)), _const_spec((D_MODEL, D_IN_PAD)),
            _const_spec((LANES, GLA_QK)), _const_spec((1, GLA_QK)),
            _const_spec((1, MLA_Q_RANK)), _const_spec((MLA_Q_RANK, HEAD_PAD)),
            _const_spec((1, MLA_KV_RANK)), _const_spec((MLA_KV_RANK, HEAD_PAD)),
            _const_spec((MLA_KV_RANK, MLA_WIDTH)),
            _const_spec((8, LANES)),
        ],
        out_specs=[row(w) for w in out_widths],
        out_shape=[jax.ShapeDtypeStruct((t, w), d) for w, d in zip(out_widths, out_dtypes)],
        scratch_shapes=[pltpu.VMEM((tm, HEAD_PAD), F32), pltpu.VMEM((tm, HEAD_PAD), F32),
                        pltpu.VMEM((tm, LANES), F32)],
        compiler_params=pltpu.CompilerParams(
            dimension_semantics=("arbitrary",), vmem_limit_bytes=VMEM_LIMIT_BYTES),
        name="proj",
    )(x2, pos2, attn_norm.reshape(1, D_MODEL), w_in_l, w2_l, gate_b.reshape(1, GLA_QK),
      q_norm.reshape(1, MLA_Q_RANK), w_uq_l.astype(BF16), kv_norm.reshape(1, MLA_KV_RANK),
      w_uk_l.astype(BF16), w_uv.astype(BF16), tab)


def _cumsum(sizes):
    total = 0
    for s in sizes:
        total += s
        yield total


def _gla_kernel(q_ref, k_ref, v_ref, la_ref, gr_ref, gon_ref, o_ref, st_ref, kf_ref, b_ref):
    nseq, rows_per_step = q_ref.shape[0], q_ref.shape[1]

    @pl.when(pl.program_id(1) == 0)
    def _():
        st_ref[...] = jnp.zeros_like(st_ref)

    row = lax.broadcasted_iota(jnp.int32, (CHUNK, CHUNK), 0)
    col = lax.broadcasted_iota(jnp.int32, (CHUNK, CHUNK), 1)
    causal = col <= row
    tril = causal.astype(BF16)
    lane = lax.broadcasted_iota(jnp.int32, (1, LANES), 1)
    lane_q = lax.broadcasted_iota(jnp.int32, (1, GLA_QK), 1)
    head_mask = [(lane // GLA_DK) == e for e in range(2)]
    gon = gon_ref[...]
    nt = (((1,), (1,)), ((), ()))
    tn = (((0,), (0,)), ((), ()))

    def exact_scores(q, k, b):
        kf_ref[...] = k
        b_ref[...] = b

        def key_row(s, sc):
            k_s = kf_ref[pl.ds(s, 1), :]
            b_s = b_ref[pl.ds(s, 1), :]
            prod = q * k_s * jnp.exp(jnp.minimum(b - b_s, 0.0))
            out = []
            for h in range(GLA_HEADS):
                hm = (lane_q // GLA_DK) == h
                col_h = jnp.sum(jnp.where(hm, prod, 0.0), axis=-1, keepdims=True)
                out.append(jnp.where(col == s, col_h, sc[h]))
            return tuple(out)

        zero = jnp.zeros((CHUNK, CHUNK), F32)
        return lax.fori_loop(0, CHUNK, key_row, (zero,) * GLA_HEADS)

    pair = lambda h: slice((h // 2) * LANES, (h // 2 + 1) * LANES)
    vsl = lambda h: slice(h * GLA_DV, (h + 1) * GLA_DV)
    heads = range(GLA_HEADS)

    def chunk(seqs, c, factorised):
        rows = pl.ds(pl.multiple_of(c * CHUNK, CHUNK), CHUNK)
        pieces = []
        for n in seqs:
            la = la_ref[n, rows, :]
            hi = la.astype(BF16)
            r1 = la - hi.astype(F32)
            mid = r1.astype(BF16)
            pieces += [hi, mid, (r1 - mid.astype(F32)).astype(BF16)]
        cum = jnp.dot(tril, jnp.concatenate(pieces, axis=1), preferred_element_type=F32)
        q, k, b, a_h, kd, eb_last, scores = {}, {}, {}, {}, {}, {}, {}
        for i, n in enumerate(seqs):
            parts = [cum[:, (3 * i + j) * GLA_QK:(3 * i + j + 1) * GLA_QK] for j in range(3)]
            b[n] = parts[0] + parts[1] + parts[2]
            b_last = b[n][CHUNK - 1:CHUNK, :]
            q[n] = q_ref[n, rows, :].astype(F32)
            k[n] = k_ref[n, rows, :].astype(F32)
            qt = q[n] * jnp.exp(b[n])
            kd[n] = k[n] * jnp.exp(b_last - b[n])
            eb_last[n] = jnp.exp(b_last)
            a_h[n] = [jnp.where(head_mask[h % 2], qt[:, pair(h)], 0.0).astype(BF16) for h in heads]
        for n in seqs:
            if factorised:
                kt = (k[n] * jnp.exp(-b[n])).astype(BF16)
                scores[n] = [lax.dot_general(a_h[n][h], kt[:, pair(h)], nt, preferred_element_type=F32)
                             for h in heads]
            else:
                scores[n] = exact_scores(q[n], k[n], b[n])
        v, st, o = {}, {}, {}
        for n in seqs:
            for h in heads:
                sc = jnp.where(causal, scores[n][h], 0.0).astype(BF16)
                v[n, h] = v_ref[n, rows, vsl(h)]
                st[n, h] = st_ref[n, h]
                o[n, h] = (lax.dot_general(a_h[n][h], st[n, h].astype(BF16), nt, preferred_element_type=F32)
                           + jnp.dot(sc, v[n, h], preferred_element_type=F32))
        for n in seqs:
            for h in heads:
                kd_h = jnp.where(head_mask[h % 2], kd[n][:, pair(h)], 0.0).astype(BF16)
                upd = lax.dot_general(v[n, h], kd_h, tn, preferred_element_type=F32)
                st_ref[n, h] = st[n, h] * eb_last[n][:, pair(h)] + upd
        for n in seqs:
            for h in heads:
                on = o[n, h] * _rms_scale(o[n, h], GLA_DV) * gon
                o_ref[n, rows, vsl(h)] = (on * gr_ref[n, rows, vsl(h)].astype(F32)).astype(BF16)

    def factorised_chunks():
        def body(c, carry):
            chunk(range(nseq), c, True)
            return carry
        lax.fori_loop(0, rows_per_step // CHUNK, body, 0)

    def exact_chunks():
        def body(c, carry):
            for n in range(nseq):
                chunk([n], c, False)
            return carry
        lax.fori_loop(0, rows_per_step // CHUNK, body, 0)

    factorisable = jnp.min(la_ref[...]) >= GLA_FAST_MIN_LOG_DECAY / CHUNK
    lax.cond(factorisable, factorised_chunks, exact_chunks)


def _gla(qg, kg, vg, la, gr, out_norm):
    b, s, _ = qg.shape
    nseq = max(n for n in GLA_SEQS_PER_STEP if b % n == 0)
    rows = min(GLA_ROWS_PER_STEP, s)
    assert s % rows == 0 and rows % CHUNK == 0
    blk = lambda w: pl.BlockSpec((nseq, rows, w), lambda i, j: (i, j, 0))
    return pl.pallas_call(
        _gla_kernel,
        grid=(b // nseq, s // rows),
        in_specs=[blk(GLA_QK), blk(GLA_QK), blk(GLA_WIDTH), blk(GLA_QK), blk(GLA_WIDTH),
                  _const_spec((1, GLA_DV))],
        out_specs=blk(GLA_WIDTH),
        out_shape=jax.ShapeDtypeStruct((b, s, GLA_WIDTH), BF16),
        scratch_shapes=[pltpu.VMEM((nseq, GLA_HEADS, GLA_DV, LANES), F32),
                        pltpu.VMEM((CHUNK, GLA_QK), F32),
                        pltpu.VMEM((CHUNK, GLA_QK), F32)],
        compiler_params=pltpu.CompilerParams(
            dimension_semantics=("arbitrary", "arbitrary"), vmem_limit_bytes=VMEM_LIMIT_BYTES),
        name="gla",
    )(qg, kg, vg, la, gr, out_norm.reshape(1, GLA_DV))


def _mla_tiles(n_q):
    assert TQ == 2 * TK
    first = [(a, 2 * a, 0) for a in range(n_q)]
    full = [(a, j, TQ // CHUNK) for a in range(n_q) for j in range(2 * a)]
    second = [(a, 2 * a + 1, -(TK // CHUNK)) for a in range(n_q)]
    return tuple(zip(*(first + full + second)))


def _mla_kernel(ti_ref, tj_ref, toff_ref, bound_ref, q_ref, k_ref, v_ref, o_ref,
                s_ref, m_ref, acc_ref, v1_ref):
    seq = q_ref.shape[1]
    n_q = seq // TQ
    n_tiles = s_ref.shape[1]
    nt = (((1,), (1,)), ((), ()))
    heads = range(2)
    lane = lax.broadcasted_iota(jnp.int32, (1, LANES), 1)
    own = [(lane // MLA_V) == e for e in heads]
    unroll = lambda count: math.gcd(count, MLA_UNROLL)

    def q_rows(i):
        return pl.ds(pl.multiple_of(i * TQ, TQ), TQ)

    def k_rows(j):
        return pl.ds(pl.multiple_of(j * TK, TK), TK)

    def masked_logits(e, t):
        hsl = slice(e * LANES, (e + 1) * LANES)
        s = lax.dot_general(q_ref[0, q_rows(ti_ref[t]), hsl], k_ref[0, k_rows(tj_ref[t]), hsl], nt,
                            preferred_element_type=F32)
        row_chunk = lax.broadcasted_iota(jnp.int32, (TQ, TK), 0) // CHUNK
        col_chunk = lax.broadcasted_iota(jnp.int32, (TQ, TK), 1) // CHUNK
        return jnp.where(col_chunk <= row_chunk + toff_ref[t], s, MASKED_LOGIT)

    def lane_fold_max(a):
        out = a[:, :LANES]
        for c in range(1, TK // LANES):
            out = jnp.maximum(out, a[:, c * LANES:(c + 1) * LANES])
        return out

    v = v_ref[0]
    for e in heads:
        v1_ref[e] = jnp.where(own[e], v, 1.0).astype(BF16)

    def exp_shifted(s, shift):
        return jnp.concatenate(
            [jnp.exp2(s[:, c * LANES:(c + 1) * LANES] - shift) for c in range(TK // LANES)],
            axis=1).astype(BF16)

    def accumulate(e, t, s, shift):
        acc_ref[e, ti_ref[t]] += jnp.dot(exp_shifted(s, shift), v1_ref[e, k_rows(tj_ref[t]), :],
                                         preferred_element_type=F32)

    def clear(i, carry):
        for e in heads:
            acc_ref[e, i] = jnp.zeros((TQ, LANES), F32)
        return carry

    def single_pass():
        lax.fori_loop(0, n_q, clear, 0)
        shift = bound_ref[0].astype(F32)

        def tile(t, carry):
            for e in heads:
                accumulate(e, t, masked_logits(e, t), shift)
            return carry

        lax.fori_loop(0, n_tiles - n_q, tile, 0, unroll=unroll(n_tiles - n_q))

        def lower_half_tile(t, carry):
            i = ti_ref[t]
            lower = pl.ds(pl.multiple_of(i * TQ + TK, TK), TK)
            row_chunk = lax.broadcasted_iota(jnp.int32, (TK, TK), 0) // CHUNK
            col_chunk = lax.broadcasted_iota(jnp.int32, (TK, TK), 1) // CHUNK
            for e in heads:
                hsl = slice(e * LANES, (e + 1) * LANES)
                s = lax.dot_general(q_ref[0, lower, hsl], k_ref[0, k_rows(tj_ref[t]), hsl], nt,
                                    preferred_element_type=F32)
                p = exp_shifted(jnp.where(col_chunk <= row_chunk, s, MASKED_LOGIT), shift)
                acc_ref[e, i, TK:, :] += jnp.dot(p, v1_ref[e, k_rows(tj_ref[t]), :],
                                                 preferred_element_type=F32)
            return carry

        lax.fori_loop(n_tiles - n_q, n_tiles, lower_half_tile, 0, unroll=unroll(n_q))

    def two_pass():
        def first_tile(t, carry):
            for e in heads:
                s = masked_logits(e, t)
                s_ref[e, t] = s
                m_ref[e, t] = lane_fold_max(s)
            return clear(t, carry)

        lax.fori_loop(0, n_q, first_tile, 0, unroll=unroll(n_q))

        def pass1(t, carry):
            i = ti_ref[t]
            for e in heads:
                s = masked_logits(e, t)
                s_ref[e, t] = s
                m_ref[e, i] = jnp.maximum(m_ref[e, i], lane_fold_max(s))
            return carry

        lax.fori_loop(n_q, n_tiles, pass1, 0, unroll=unroll(n_tiles - n_q))

        def row_max(i, carry):
            for e in heads:
                m_ref[e, i] = jnp.broadcast_to(jnp.max(m_ref[e, i], axis=-1, keepdims=True), (TQ, LANES))
            return carry

        lax.fori_loop(0, n_q, row_max, 0)

        def pass2(t, carry):
            for e in heads:
                accumulate(e, t, s_ref[e, t], m_ref[e, ti_ref[t]])
            return carry

        lax.fori_loop(0, n_tiles, pass2, 0, unroll=unroll(n_tiles))

    lax.cond(bound_ref[0] <= MLA_MAX_SHIFT, single_pass, two_pass)

    def finish(i, carry):
        acc0, acc1 = acc_ref[0, i], acc_ref[1, i]
        num = jnp.where(own[0], acc0, acc1)
        den = pltpu.roll(jnp.where(own[0], acc1, acc0), LANES // 2, 1)
        o_ref[0, q_rows(i), :] = (num / den).astype(BF16)
        return carry

    lax.fori_loop(0, n_q, finish, 0)


def _mla(q, k, v, gain_q, gain_k):
    b, s, _ = q.shape
    assert s % TQ == 0 and TQ % TK == 0
    n_q = s // TQ
    ti, tj, toff = _mla_tiles(n_q)
    bound = jnp.ceil(1.02 * LOG2E * math.sqrt(MLA_QK) * jnp.max(jnp.abs(gain_q)) * jnp.max(jnp.abs(gain_k)))
    bound = jnp.minimum(bound, 2.0 * MLA_MAX_SHIFT).astype(jnp.int32).reshape(1)
    blk = lambda w: pl.BlockSpec((1, s, w), lambda i, p, *_: (i, 0, p))
    return pl.pallas_call(
        _mla_kernel,
        grid_spec=pltpu.PrefetchScalarGridSpec(
            num_scalar_prefetch=4,
            grid=(b, MLA_HEADS // 2),
            in_specs=[blk(2 * LANES), blk(2 * LANES), blk(LANES)],
            out_specs=blk(LANES),
            scratch_shapes=[pltpu.VMEM((2, len(ti), TQ, TK), F32),
                            pltpu.VMEM((2, n_q, TQ, LANES), F32),
                            pltpu.VMEM((2, n_q, TQ, LANES), F32),
                            pltpu.VMEM((2, s, LANES), BF16)]),
        out_shape=jax.ShapeDtypeStruct((b, s, MLA_WIDTH), BF16),
        compiler_params=pltpu.CompilerParams(
            dimension_semantics=("arbitrary", "arbitrary"),
            vmem_limit_bytes=VMEM_LIMIT_BYTES),
        name="mla",
    )(jnp.asarray(ti, jnp.int32), jnp.asarray(tj, jnp.int32), jnp.asarray(toff, jnp.int32), bound,
      q, k, v)


def _tail_kernel(x_ref, og_ref, om_ref, p_ref, w_out_ref, g_mlp_ref, w_up_ref, w_down_ref,
                 g_ple_ref, w_gate_ref, b_gate_ref, w_pp_ref, o_ref):
    h = (x_ref[...]
         + jnp.dot(og_ref[...], w_out_ref[:GLA_WIDTH, :], preferred_element_type=F32)
         + jnp.dot(om_ref[...], w_out_ref[GLA_WIDTH:, :], preferred_element_type=F32))
    m = (h * _rms_scale(h, D_MODEL) * g_mlp_ref[...]).astype(BF16)
    mlp = None
    for c in range(D_FF // FF_CHUNK):
        cols = slice(c * FF_CHUNK, (c + 1) * FF_CHUNK)
        u = jnp.maximum(jnp.dot(m, w_up_ref[:, cols], preferred_element_type=F32), 0.0)
        d = jnp.dot((u * u).astype(BF16), w_down_ref[cols, :], preferred_element_type=F32)
        mlp = d if mlp is None else mlp + d
    h = h + mlp
    g = (h * _rms_scale(h, D_MODEL) * g_ple_ref[...]).astype(BF16)
    gate = jax.nn.sigmoid(jnp.dot(g, w_gate_ref[...], preferred_element_type=F32) + b_gate_ref[...])
    pp = jnp.dot(p_ref[...].astype(BF16), w_pp_ref[...], preferred_element_type=F32)
    o_ref[...] = h + pp * gate


def _tail(x2, og, om, p2, w_out, mlp_norm, w_up, w_down, ple_norm, w_gate, b_gate, w_pp):
    t = x2.shape[0]
    tm = TM_TAIL
    assert t % tm == 0
    row = lambda w: pl.BlockSpec((tm, w), lambda i: (i, 0))
    return pl.pallas_call(
        _tail_kernel,
        grid=(t // tm,),
        in_specs=[row(D_MODEL), row(GLA_WIDTH), row(MLA_WIDTH), row(PLE_DIM),
                  _const_spec((D_MODEL, D_MODEL)), _const_spec((1, D_MODEL)),
                  _const_spec((D_MODEL, D_FF)), _const_spec((D_FF, D_MODEL)),
                  _const_spec((1, D_MODEL)), _const_spec((D_MODEL, D_MODEL)),
                  _const_spec((1, D_MODEL)), _const_spec((PLE_DIM, D_MODEL))],
        out_specs=row(D_MODEL),
        out_shape=jax.ShapeDtypeStruct((t, D_MODEL), F32),
        compiler_params=pltpu.CompilerParams(
            dimension_semantics=("arbitrary",), vmem_limit_bytes=VMEM_LIMIT_BYTES),
        name="tail",
    )(x2, og, om, p2, w_out.astype(BF16), mlp_norm.reshape(1, D_MODEL), w_up.astype(BF16),
      w_down.astype(BF16), ple_norm.reshape(1, D_MODEL), w_gate.astype(BF16),
      b_gate.reshape(1, D_MODEL), w_pp.astype(BF16))


def kernel(x, p, positions, attn_norm, w_in, gla_gate_w2, gla_gate_b, gla_out_norm, mla_q_norm,
           mla_w_uq, mla_kv_norm, mla_w_ukv, qk_norm_q, qk_norm_k, w_out, mlp_norm, w_mlp_up,
           w_mlp_down, ple_norm, w_ple_gate, b_ple_gate, w_ple_proj):
    b, s, d = x.shape
    depth = w_in.shape[0]
    t = b * s
    pos2 = positions.reshape(t, 1)
    h = x.reshape(t, d)
    for i in range(depth):
        qg, kg, vg, la, gr, q, k, v = _projection(
            h, pos2, attn_norm[i], w_in[i], gla_gate_w2[i], gla_gate_b[i], mla_q_norm[i],
            mla_w_uq[i], mla_kv_norm[i], mla_w_ukv[i], qk_norm_q[i], qk_norm_k[i])
        seq = lambda a: a.reshape(b, s, a.shape[-1])
        og = _gla(seq(qg), seq(kg), seq(vg), seq(la), seq(gr), gla_out_norm[i])
        om = _mla(seq(q), seq(k), seq(v), qk_norm_q[i], qk_norm_k[i])
        h = _tail(h, og.reshape(t, GLA_WIDTH), om.reshape(t, MLA_WIDTH), p[i].reshape(t, PLE_DIM),
                  w_out[i], mlp_norm[i], w_mlp_up[i], w_mlp_down[i], ple_norm[i], w_ple_gate[i],
                  b_ple_gate[i], w_ple_proj[i])
    return h.reshape(b, s, d)
```

```python
import functools
import math

import jax
import jax.numpy as jnp
from jax import lax
from jax.experimental import pallas as pl
from jax.experimental.pallas import tpu as pltpu

F32 = jnp.float32
BF16 = jnp.bfloat16

D_MODEL = 1024
CHUNK = 64
PLE_DIM = 256
D_FF = 4 * D_MODEL
EPS = 1e-6
GLA_HEADS = 4
GLA_DK = 64
GLA_DV = 128
GLA_GATE_RANK = 16
GLA_TAU = 16.0
GLA_QK = GLA_HEADS * GLA_DK
GLA_WIDTH = GLA_HEADS * GLA_DV
MLA_HEADS = 8
MLA_NOPE = 64
MLA_ROPE = 32
MLA_V = 64
MLA_QK = MLA_NOPE + MLA_ROPE
MLA_Q_RANK = 256
MLA_KV_RANK = 128
MLA_WIDTH = MLA_HEADS * MLA_V
ROPE_THETA = 10000.0
LOG2E = math.log2(math.e)
IN_SPLITS = (GLA_QK, GLA_QK, GLA_WIDTH, GLA_GATE_RANK, GLA_WIDTH, MLA_Q_RANK, MLA_KV_RANK, MLA_ROPE)

LANES = 128
SUBLANES = 8
HALF_ROPE = MLA_ROPE // 2
HEAD_PAD = MLA_HEADS * LANES
D_IN_PAD = 2048
VMEM_LIMIT_BYTES = 56 * 1024 * 1024

TM_PROJ = 512
TM_TAIL = 512
FF_CHUNK = 1024
TQ = 512
TK = 256
MLA_UNROLL = 8
ROPE_ROWS = 64
GLA_SEQS_PER_STEP = (8, 4, 2, 1)
GLA_ROWS_PER_STEP = 256
GLA_FAST_MIN_LOG_DECAY = -80.0
MASKED_LOGIT = -1e30
MLA_MAX_SHIFT = 60

ROW_FREQ16, ROW_MASK96, ROW_KRMASK, ROW_GQ, ROW_GQ_SW, ROW_GK, ROW_GK_SW = range(7)


def _const_spec(shape):
    return pl.BlockSpec(shape, lambda *_: (0,) * len(shape), pipeline_mode=pl.Buffered(1))


def _rms_scale(v, n):
    return lax.rsqrt(jnp.sum(v * v, axis=-1, keepdims=True) * (1.0 / n) + EPS)


def _proj_kernel(x_ref, pos_ref, g_attn_ref, w_in_ref, w2_ref, b2_ref, gqn_ref, w_uq_ref,
                 gkvn_ref, w_uk_ref, w_uv_ref, tab_ref,
                 qg_ref, kg_ref, vg_ref, la_ref, gr_ref, q_ref, k_ref, v_ref,
                 qs_ref, ks_ref, slot_ref):
    x = x_ref[...]
    xn = (x * _rms_scale(x, D_MODEL) * g_attn_ref[...]).astype(BF16)

    def proj(lo, hi):
        return jnp.dot(xn, w_in_ref[:, lo:hi], preferred_element_type=F32)

    zqk = proj(0, 2 * GLA_QK)
    qg_ref[...] = (zqk[:, :GLA_QK] * (GLA_DK ** -0.5)).astype(BF16)
    kg_ref[...] = zqk[:, GLA_QK:].astype(BF16)
    vg_ref[...] = proj(512, 1024).astype(BF16)
    gr = proj(1024, 1536)
    gr_ref[...] = (gr * jax.nn.sigmoid(gr)).astype(BF16)
    zc = proj(1536, 2048)
    cq = zc[:, :MLA_Q_RANK]
    ckv = zc[:, MLA_Q_RANK:MLA_Q_RANK + MLA_KV_RANK]
    slot = zc[:, MLA_Q_RANK + MLA_KV_RANK:]
    slot_ref[...] = slot

    xg = jnp.dot(slot.astype(BF16), w2_ref[...], preferred_element_type=F32) + b2_ref[...]
    la_ref[...] = (jnp.minimum(xg, 0.0) - jnp.log(1.0 + jnp.exp(-jnp.abs(xg)))) * (1.0 / GLA_TAU)

    cqn = (cq * _rms_scale(cq, MLA_Q_RANK) * gqn_ref[...]).astype(BF16)
    qs_ref[...] = jnp.dot(cqn, w_uq_ref[...], preferred_element_type=F32)
    ckvn = (ckv * _rms_scale(ckv, MLA_KV_RANK) * gkvn_ref[...]).astype(BF16)
    ks_ref[...] = jnp.dot(ckvn, w_uk_ref[...], preferred_element_type=F32)
    v_ref[...] = jnp.dot(ckvn, w_uv_ref[...], preferred_element_type=F32).astype(BF16)

    tab = lambda r: tab_ref[r:r + 1, :]
    mask96, krmask = tab(ROW_MASK96), tab(ROW_KRMASK)
    gq, gq_sw = tab(ROW_GQ) * LOG2E, tab(ROW_GQ_SW) * LOG2E
    gk, gk_sw = tab(ROW_GK) * math.sqrt(MLA_QK), tab(ROW_GK_SW) * math.sqrt(MLA_QK)
    half = LANES // 2

    lane = lax.broadcasted_iota(jnp.int32, (1, LANES), 1)
    groups = ROPE_ROWS // SUBLANES
    assert groups * HALF_ROPE == LANES
    x1_lanes = (lane >= 32) & (lane < 32 + HALF_ROPE)
    x2_lanes = (lane >= 32 + HALF_ROPE) & (lane < 64)

    def rope_tables(rows):
        pos = jnp.broadcast_to(pos_ref[rows, :].astype(F32), (ROPE_ROWS, LANES))
        packed = jnp.zeros((SUBLANES, LANES), F32)
        for a in range(groups):
            packed = jnp.where(lane // HALF_ROPE == a, pos[a * SUBLANES:(a + 1) * SUBLANES], packed)
        ang = packed * tab(ROW_FREQ16)
        cos_p, sin_p = jnp.cos(ang), jnp.sin(ang)
        cos, sin = [], []
        for a in range(groups):
            s1 = (32 - HALF_ROPE * a) % LANES
            s2 = (32 + HALF_ROPE - HALF_ROPE * a) % LANES
            c1, c2 = pltpu.roll(cos_p, s1, 1), pltpu.roll(cos_p, s2, 1)
            n1, n2 = pltpu.roll(sin_p, s1, 1), pltpu.roll(sin_p, s2, 1)
            cos.append(jnp.where(x1_lanes, c1, jnp.where(x2_lanes, c2, mask96)))
            sin.append(jnp.where(x1_lanes, -n1, jnp.where(x2_lanes, n2, 0.0)))
        return jnp.concatenate(cos, axis=0), jnp.concatenate(sin, axis=0)

    sum96 = (lax.broadcasted_iota(jnp.int32, (LANES, LANES), 0) < MLA_QK).astype(BF16)

    def head_rms(v):
        ssq = jnp.dot((v * v).astype(BF16), sum96, preferred_element_type=F32)
        return lax.rsqrt(ssq + MLA_QK * EPS)

    def head_rows(rb, carry):
        rows = pl.ds(pl.multiple_of(rb * ROPE_ROWS, ROPE_ROWS), ROPE_ROWS)
        cos, sin = rope_tables(rows)
        kr = slot_ref[rows, :] * krmask
        kr_rot = pltpu.roll(kr, half, 1) * (sin * gk_sw)
        cos_q, sin_q, cos_k = cos * gq, sin * gq_sw, cos * gk
        for h in range(MLA_HEADS):
            sl = slice(h * LANES, (h + 1) * LANES)
            qh = qs_ref[rows, sl]
            q_ref[rows, sl] = ((qh * cos_q + pltpu.roll(qh, half, 1) * sin_q) * head_rms(qh)).astype(BF16)
            kh = ks_ref[rows, sl] + kr
            k_ref[rows, sl] = ((kh * cos_k + kr_rot) * head_rms(kh)).astype(BF16)
        return carry

    lax.fori_loop(0, x_ref.shape[0] // ROPE_ROWS, head_rows, 0, unroll=2)


def _head_layout(nope, rope):
    x1, x2 = rope[..., :HALF_ROPE], rope[..., HALF_ROPE:]
    return jnp.concatenate([nope[..., :32], x1, x2, nope[..., 32:], x2, x1], axis=-1)


def _projection(x2, pos2, attn_norm, w_in, gate_w2, gate_b, q_norm, w_uq, kv_norm, w_ukv,
                qk_norm_q, qk_norm_k):
    t = x2.shape[0]
    tm = TM_PROJ
    assert t % tm == 0

    gq, gk, gv, g_low, g_r, c_q, c_kv, k_r = jnp.split(w_in, list(_cumsum(IN_SPLITS))[:-1], axis=1)
    zeros = lambda n: jnp.zeros((D_MODEL, n), w_in.dtype)
    kr_sw = jnp.concatenate([k_r[:, HALF_ROPE:], k_r[:, :HALF_ROPE]], axis=1)
    slot = jnp.concatenate([g_low, zeros(32 - GLA_GATE_RANK), k_r, zeros(32), kr_sw], axis=1)
    w_in_l = jnp.concatenate([gq, gk, gv, g_r, c_q, c_kv, slot], axis=1).astype(BF16)
    assert w_in_l.shape == (D_MODEL, D_IN_PAD)

    w2_l = jnp.zeros((LANES, GLA_QK), F32).at[:GLA_GATE_RANK].set(gate_w2).astype(BF16)

    w_uq_h = w_uq.reshape(MLA_Q_RANK, MLA_HEADS, MLA_QK)
    w_uq_l = _head_layout(w_uq_h[..., :MLA_NOPE], w_uq_h[..., MLA_NOPE:]).reshape(MLA_Q_RANK, HEAD_PAD)
    w_ukv_h = w_ukv.reshape(MLA_KV_RANK, MLA_HEADS, MLA_NOPE + MLA_V)
    w_uk_l = _head_layout(w_ukv_h[..., :MLA_NOPE],
                          jnp.zeros((MLA_KV_RANK, MLA_HEADS, MLA_ROPE), F32)).reshape(MLA_KV_RANK, HEAD_PAD)
    w_uv = w_ukv_h[..., MLA_NOPE:].reshape(MLA_KV_RANK, MLA_WIDTH)
    gq_l = _head_layout(qk_norm_q[:MLA_NOPE], qk_norm_q[MLA_NOPE:])
    gk_l = _head_layout(qk_norm_k[:MLA_NOPE], qk_norm_k[MLA_NOPE:])

    inv_freq = ROPE_THETA ** (-jnp.arange(0, MLA_ROPE, 2, dtype=F32) / MLA_ROPE)
    z32 = jnp.zeros((32,), F32)
    tab = jnp.zeros((SUBLANES, LANES), F32)
    tab = tab.at[ROW_FREQ16].set(jnp.tile(inv_freq, LANES // HALF_ROPE))
    tab = tab.at[ROW_MASK96].set(jnp.concatenate([jnp.ones((96,), F32), z32]))
    tab = tab.at[ROW_KRMASK].set(jnp.concatenate([z32, jnp.ones((32,), F32), z32, jnp.ones((32,), F32)]))
    tab = tab.at[ROW_GQ].set(gq_l).at[ROW_GQ_SW].set(jnp.roll(gq_l, LANES // 2))
    tab = tab.at[ROW_GK].set(gk_l).at[ROW_GK_SW].set(jnp.roll(gk_l, LANES // 2))

    row = lambda w: pl.BlockSpec((tm, w), lambda i: (i, 0))
    out_widths = (GLA_QK, GLA_QK, GLA_WIDTH, GLA_QK, GLA_WIDTH, HEAD_PAD, HEAD_PAD, MLA_WIDTH)
    out_dtypes = (BF16, BF16, BF16, F32, BF16, BF16, BF16, BF16)
    return pl.pallas_call(
        _proj_kernel,
        grid=(t // tm,),
        in_specs=[
            row(D_MODEL), row(1),
            _const_spec((1, D_MODEL)), _const_spec((D_MODEL, D_IN_PAD)),
            _const_spec((LANES, GLA_QK)), _const_spec((1, GLA_QK)),
            _const_spec((1, MLA_Q_RANK)), _const_spec((MLA_Q_RANK, HEAD_PAD)),
            _const_spec((1, MLA_KV_RANK)), _const_spec((MLA_KV_RANK, HEAD_PAD)),
            _const_spec((MLA_KV_RANK, MLA_WIDTH)),
            _const_spec((8, LANES)),
        ],
        out_specs=[row(w) for w in out_widths],
        out_shape=[jax.ShapeDtypeStruct((t, w), d) for w, d in zip(out_widths, out_dtypes)],
        scratch_shapes=[pltpu.VMEM((tm, HEAD_PAD), F32), pltpu.VMEM((tm, HEAD_PAD), F32),
                        pltpu.VMEM((tm, LANES), F32)],
        compiler_params=pltpu.CompilerParams(
            dimension_semantics=("arbitrary",), vmem_limit_bytes=VMEM_LIMIT_BYTES),
        name="proj",
    )(x2, pos2, attn_norm.reshape(1, D_MODEL), w_in_l, w2_l, gate_b.reshape(1, GLA_QK),
      q_norm.reshape(1, MLA_Q_RANK), w_uq_l.astype(BF16), kv_norm.reshape(1, MLA_KV_RANK),
      w_uk_l.astype(BF16), w_uv.astype(BF16), tab)


def _cumsum(sizes):
    total = 0
    for s in sizes:
        total += s
        yield total


def _gla_kernel(q_ref, k_ref, v_ref, la_ref, gr_ref, gon_ref, o_ref, st_ref, kf_ref, b_ref):
    nseq, rows_per_step = q_ref.shape[0], q_ref.shape[1]

    @pl.when(pl.program_id(1) == 0)
    def _():
        st_ref[...] = jnp.zeros_like(st_ref)

    row = lax.broadcasted_iota(jnp.int32, (CHUNK, CHUNK), 0)
    col = lax.broadcasted_iota(jnp.int32, (CHUNK, CHUNK), 1)
    causal = col <= row
    lane = lax.broadcasted_iota(jnp.int32, (1, LANES), 1)
    lane_q = lax.broadcasted_iota(jnp.int32, (1, GLA_QK), 1)
    head_mask = [(lane // GLA_DK) == e for e in range(2)]
    gon = gon_ref[...]
    nt = (((1,), (1,)), ((), ()))
    tn = (((0,), (0,)), ((), ()))

    def exact_scores(q, k, b):
        kf_ref[...] = k
        b_ref[...] = b

        def key_row(s, sc):
            k_s = kf_ref[pl.ds(s, 1), :]
            b_s = b_ref[pl.ds(s, 1), :]
            prod = q * k_s * jnp.exp(jnp.minimum(b - b_s, 0.0))
            out = []
            for h in range(GLA_HEADS):
                hm = (lane_q // GLA_DK) == h
                col_h = jnp.sum(jnp.where(hm, prod, 0.0), axis=-1, keepdims=True)
                out.append(jnp.where(col == s, col_h, sc[h]))
            return tuple(out)

        zero = jnp.zeros((CHUNK, CHUNK), F32)
        return lax.fori_loop(0, CHUNK, key_row, (zero,) * GLA_HEADS)

    pair = lambda h: slice((h // 2) * LANES, (h // 2 + 1) * LANES)
    vsl = lambda h: slice(h * GLA_DV, (h + 1) * GLA_DV)
    heads = range(GLA_HEADS)

    def row_cumsum(x):
        row_id = lax.broadcasted_iota(jnp.int32, x.shape, 0)
        step = 1
        while step < CHUNK:
            if step < SUBLANES:
                moved = jnp.where(row_id >= step, pltpu.roll(x, step, 0), 0.0)
            else:
                moved = jnp.concatenate([jnp.zeros((step, x.shape[1]), x.dtype), x[:-step]], axis=0)
            x = x + moved
            step *= 2
        return x

    def chunk(seqs, c, factorised):
        rows = pl.ds(pl.multiple_of(c * CHUNK, CHUNK), CHUNK)
        q, k, b, a_h, kt, kd, eb_last, scores, o_inter = {}, {}, {}, {}, {}, {}, {}, {}, {}
        for n in seqs:
            b[n] = row_cumsum(la_ref[n, rows, :])
            b_last = b[n][CHUNK - 1:CHUNK, :]
            q[n] = q_ref[n, rows, :].astype(F32)
            k[n] = k_ref[n, rows, :].astype(F32)
            qt = q[n] * jnp.exp(b[n])
            eb_last[n] = jnp.exp(b_last)
            if factorised:
                k_grown = k[n] * jnp.exp(-b[n])
                kt[n] = k_grown.astype(BF16)
                kd[n] = k_grown * eb_last[n]
            else:
                kd[n] = k[n] * jnp.exp(b_last - b[n])
            a_h[n] = [jnp.where(head_mask[h % 2], qt[:, pair(h)], 0.0).astype(BF16) for h in heads]
        v, st, o = {}, {}, {}
        for n in seqs:
            for h in heads:
                st[n, h] = st_ref[n, h]
                if factorised:
                    both = lax.dot_general(
                        a_h[n][h], jnp.concatenate([st[n, h].astype(BF16), kt[n][:, pair(h)]], axis=0), nt,
                        preferred_element_type=F32)
                    o_inter[n, h], scores[n, h] = both[:, :GLA_DV], both[:, GLA_DV:]
                else:
                    o_inter[n, h] = lax.dot_general(a_h[n][h], st[n, h].astype(BF16), nt,
                                                    preferred_element_type=F32)
            if not factorised:
                for h, sc in enumerate(exact_scores(q[n], k[n], b[n])):
                    scores[n, h] = sc
        for n in seqs:
            for h in heads:
                sc = jnp.where(causal, scores[n, h], 0.0).astype(BF16)
                v[n, h] = v_ref[n, rows, vsl(h)]
                o[n, h] = o_inter[n, h] + jnp.dot(sc, v[n, h], preferred_element_type=F32)
        for n in seqs:
            for h in heads:
                kd_h = jnp.where(head_mask[h % 2], kd[n][:, pair(h)], 0.0).astype(BF16)
                upd = lax.dot_general(v[n, h], kd_h, tn, preferred_element_type=F32)
                st_ref[n, h] = st[n, h] * eb_last[n][:, pair(h)] + upd
        for n in seqs:
            for h in heads:
                on = o[n, h] * _rms_scale(o[n, h], GLA_DV) * gon
                o_ref[n, rows, vsl(h)] = (on * gr_ref[n, rows, vsl(h)].astype(F32)).astype(BF16)

    def factorised_chunks():
        def body(c, carry):
            chunk(range(nseq), c, True)
            return carry
        lax.fori_loop(0, rows_per_step // CHUNK, body, 0)

    def exact_chunks():
        def body(c, carry):
            for n in range(nseq):
                chunk([n], c, False)
            return carry
        lax.fori_loop(0, rows_per_step // CHUNK, body, 0)

    factorisable = jnp.min(la_ref[...]) >= GLA_FAST_MIN_LOG_DECAY / CHUNK
    lax.cond(factorisable, factorised_chunks, exact_chunks)


def _gla(qg, kg, vg, la, gr, out_norm):
    b, s, _ = qg.shape
    nseq = max(n for n in GLA_SEQS_PER_STEP if b % n == 0)
    rows = min(GLA_ROWS_PER_STEP, s)
    assert s % rows == 0 and rows % CHUNK == 0
    blk = lambda w: pl.BlockSpec((nseq, rows, w), lambda i, j: (i, j, 0))
    return pl.pallas_call(
        _gla_kernel,
        grid=(b // nseq, s // rows),
        in_specs=[blk(GLA_QK), blk(GLA_QK), blk(GLA_WIDTH), blk(GLA_QK), blk(GLA_WIDTH),
                  _const_spec((1, GLA_DV))],
        out_specs=blk(GLA_WIDTH),
        out_shape=jax.ShapeDtypeStruct((b, s, GLA_WIDTH), BF16),
        scratch_shapes=[pltpu.VMEM((nseq, GLA_HEADS, GLA_DV, LANES), F32),
                        pltpu.VMEM((CHUNK, GLA_QK), F32),
                        pltpu.VMEM((CHUNK, GLA_QK), F32)],
        compiler_params=pltpu.CompilerParams(
            dimension_semantics=("arbitrary", "arbitrary"), vmem_limit_bytes=VMEM_LIMIT_BYTES),
        name="gla",
    )(qg, kg, vg, la, gr, out_norm.reshape(1, GLA_DV))


def _mla_tiles(n_q):
    assert TQ == 2 * TK
    first = [(a, 2 * a, 0) for a in range(n_q)]
    full = [(a, j, TQ // CHUNK) for a in range(n_q) for j in range(2 * a)]
    second = [(a, 2 * a + 1, -(TK // CHUNK)) for a in range(n_q)]
    return tuple(zip(*(first + full + second)))


def _mla_kernel(ti_ref, tj_ref, toff_ref, bound_ref, q_ref, k_ref, v_ref, o_ref,
                s_ref, m_ref, acc_ref, v1_ref):
    seq = q_ref.shape[1]
    n_q = seq // TQ
    n_tiles = s_ref.shape[1]
    nt = (((1,), (1,)), ((), ()))
    heads = range(2)
    lane = lax.broadcasted_iota(jnp.int32, (1, LANES), 1)
    own = [(lane // MLA_V) == e for e in heads]
    unroll = lambda count: math.gcd(count, MLA_UNROLL)

    def q_rows(i):
        return pl.ds(pl.multiple_of(i * TQ, TQ), TQ)

    def k_rows(j):
        return pl.ds(pl.multiple_of(j * TK, TK), TK)

    def masked_logits(e, t):
        hsl = slice(e * LANES, (e + 1) * LANES)
        s = lax.dot_general(q_ref[0, q_rows(ti_ref[t]), hsl], k_ref[0, k_rows(tj_ref[t]), hsl], nt,
                            preferred_element_type=F32)
        row_chunk = lax.broadcasted_iota(jnp.int32, (TQ, TK), 0) // CHUNK
        col_chunk = lax.broadcasted_iota(jnp.int32, (TQ, TK), 1) // CHUNK
        return jnp.where(col_chunk <= row_chunk + toff_ref[t], s, MASKED_LOGIT)

    def lane_fold_max(a):
        out = a[:, :LANES]
        for c in range(1, TK // LANES):
            out = jnp.maximum(out, a[:, c * LANES:(c + 1) * LANES])
        return out

    v = v_ref[0]
    for e in heads:
        v1_ref[e] = jnp.where(own[e], v, 1.0).astype(BF16)

    def exp_shifted(s, shift):
        return jnp.concatenate(
            [jnp.exp2(s[:, c * LANES:(c + 1) * LANES] - shift) for c in range(TK // LANES)],
            axis=1).astype(BF16)

    def accumulate(e, t, s, shift):
        acc_ref[e, ti_ref[t]] += jnp.dot(exp_shifted(s, shift), v1_ref[e, k_rows(tj_ref[t]), :],
                                         preferred_element_type=F32)

    def clear(i, carry):
        for e in heads:
            acc_ref[e, i] = jnp.zeros((TQ, LANES), F32)
        return carry

    def single_pass():
        lax.fori_loop(0, n_q, clear, 0)
        shift = bound_ref[0].astype(F32)

        def tile(t, carry):
            for e in heads:
                accumulate(e, t, masked_logits(e, t), shift)
            return carry

        lax.fori_loop(0, n_tiles - n_q, tile, 0, unroll=unroll(n_tiles - n_q))

        def lower_half_tile(t, carry):
            i = ti_ref[t]
            lower = pl.ds(pl.multiple_of(i * TQ + TK, TK), TK)
            row_chunk = lax.broadcasted_iota(jnp.int32, (TK, TK), 0) // CHUNK
            col_chunk = lax.broadcasted_iota(jnp.int32, (TK, TK), 1) // CHUNK
            for e in heads:
                hsl = slice(e * LANES, (e + 1) * LANES)
                s = lax.dot_general(q_ref[0, lower, hsl], k_ref[0, k_rows(tj_ref[t]), hsl], nt,
                                    preferred_element_type=F32)
                p = exp_shifted(jnp.where(col_chunk <= row_chunk, s, MASKED_LOGIT), shift)
                acc_ref[e, i, TK:, :] += jnp.dot(p, v1_ref[e, k_rows(tj_ref[t]), :],
                                                 preferred_element_type=F32)
            return carry

        lax.fori_loop(n_tiles - n_q, n_tiles, lower_half_tile, 0, unroll=unroll(n_q))

    def two_pass():
        def first_tile(t, carry):
            for e in heads:
                s = masked_logits(e, t)
                s_ref[e, t] = s
                m_ref[e, t] = lane_fold_max(s)
            return clear(t, carry)

        lax.fori_loop(0, n_q, first_tile, 0, unroll=unroll(n_q))

        def pass1(t, carry):
            i = ti_ref[t]
            for e in heads:
                s = masked_logits(e, t)
                s_ref[e, t] = s
                m_ref[e, i] = jnp.maximum(m_ref[e, i], lane_fold_max(s))
            return carry

        lax.fori_loop(n_q, n_tiles, pass1, 0, unroll=unroll(n_tiles - n_q))

        def row_max(i, carry):
            for e in heads:
                m_ref[e, i] = jnp.broadcast_to(jnp.max(m_ref[e, i], axis=-1, keepdims=True), (TQ, LANES))
            return carry

        lax.fori_loop(0, n_q, row_max, 0)

        def pass2(t, carry):
            for e in heads:
                accumulate(e, t, s_ref[e, t], m_ref[e, ti_ref[t]])
            return carry

        lax.fori_loop(0, n_tiles, pass2, 0, unroll=unroll(n_tiles))

    lax.cond(bound_ref[0] <= MLA_MAX_SHIFT, single_pass, two_pass)

    def finish(i, carry):
        acc0, acc1 = acc_ref[0, i], acc_ref[1, i]
        num = jnp.where(own[0], acc0, acc1)
        den = pltpu.roll(jnp.where(own[0], acc1, acc0), LANES // 2, 1)
        o_ref[0, q_rows(i), :] = (num / den).astype(BF16)
        return carry

    lax.fori_loop(0, n_q, finish, 0)


def _mla(q, k, v, gain_q, gain_k):
    b, s, _ = q.shape
    assert s % TQ == 0 and TQ % TK == 0
    n_q = s // TQ
    ti, tj, toff = _mla_tiles(n_q)
    bound = jnp.ceil(1.02 * LOG2E * math.sqrt(MLA_QK) * jnp.max(jnp.abs(gain_q)) * jnp.max(jnp.abs(gain_k)))
    bound = jnp.minimum(bound, 2.0 * MLA_MAX_SHIFT).astype(jnp.int32).reshape(1)
    blk = lambda w: pl.BlockSpec((1, s, w), lambda i, p, *_: (i, 0, p))
    return pl.pallas_call(
        _mla_kernel,
        grid_spec=pltpu.PrefetchScalarGridSpec(
            num_scalar_prefetch=4,
            grid=(b, MLA_HEADS // 2),
            in_specs=[blk(2 * LANES), blk(2 * LANES), blk(LANES)],
            out_specs=blk(LANES),
            scratch_shapes=[pltpu.VMEM((2, len(ti), TQ, TK), F32),
                            pltpu.VMEM((2, n_q, TQ, LANES), F32),
                            pltpu.VMEM((2, n_q, TQ, LANES), F32),
                            pltpu.VMEM((2, s, LANES), BF16)]),
        out_shape=jax.ShapeDtypeStruct((b, s, MLA_WIDTH), BF16),
        compiler_params=pltpu.CompilerParams(
            dimension_semantics=("arbitrary", "arbitrary"),
            vmem_limit_bytes=VMEM_LIMIT_BYTES),
        name="mla",
    )(jnp.asarray(ti, jnp.int32), jnp.asarray(tj, jnp.int32), jnp.asarray(toff, jnp.int32), bound,
      q, k, v)


def _tail_kernel(x_ref, og_ref, om_ref, p_ref, w_out_ref, g_mlp_ref, w_up_ref, w_down_ref,
                 g_ple_ref, w_gate_ref, b_gate_ref, w_pp_ref, o_ref):
    h = (x_ref[...]
         + jnp.dot(og_ref[...], w_out_ref[:GLA_WIDTH, :], preferred_element_type=F32)
         + jnp.dot(om_ref[...], w_out_ref[GLA_WIDTH:, :], preferred_element_type=F32))
    m = (h * _rms_scale(h, D_MODEL) * g_mlp_ref[...]).astype(BF16)
    mlp = None
    for c in range(D_FF // FF_CHUNK):
        cols = slice(c * FF_CHUNK, (c + 1) * FF_CHUNK)
        u = jnp.maximum(jnp.dot(m, w_up_ref[:, cols], preferred_element_type=F32), 0.0)
        d = jnp.dot((u * u).astype(BF16), w_down_ref[cols, :], preferred_element_type=F32)
        mlp = d if mlp is None else mlp + d
    h = h + mlp
    g = (h * _rms_scale(h, D_MODEL) * g_ple_ref[...]).astype(BF16)
    gate = jax.nn.sigmoid(jnp.dot(g, w_gate_ref[...], preferred_element_type=F32) + b_gate_ref[...])
    pp = jnp.dot(p_ref[...].astype(BF16), w_pp_ref[...], preferred_element_type=F32)
    o_ref[...] = h + pp * gate


def _tail(x2, og, om, p2, w_out, mlp_norm, w_up, w_down, ple_norm, w_gate, b_gate, w_pp):
    t = x2.shape[0]
    tm = TM_TAIL
    assert t % tm == 0
    row = lambda w: pl.BlockSpec((tm, w), lambda i: (i, 0))
    return pl.pallas_call(
        _tail_kernel,
        grid=(t // tm,),
        in_specs=[row(D_MODEL), row(GLA_WIDTH), row(MLA_WIDTH), row(PLE_DIM),
                  _const_spec((D_MODEL, D_MODEL)), _const_spec((1, D_MODEL)),
                  _const_spec((D_MODEL, D_FF)), _const_spec((D_FF, D_MODEL)),
                  _const_spec((1, D_MODEL)), _const_spec((D_MODEL, D_MODEL)),
                  _const_spec((1, D_MODEL)), _const_spec((PLE_DIM, D_MODEL))],
        out_specs=row(D_MODEL),
        out_shape=jax.ShapeDtypeStruct((t, D_MODEL), F32),
        compiler_params=pltpu.CompilerParams(
            dimension_semantics=("arbitrary",), vmem_limit_bytes=VMEM_LIMIT_BYTES),
        name="tail",
    )(x2, og, om, p2, w_out.astype(BF16), mlp_norm.reshape(1, D_MODEL), w_up.astype(BF16),
      w_down.astype(BF16), ple_norm.reshape(1, D_MODEL), w_gate.astype(BF16),
      b_gate.reshape(1, D_MODEL), w_pp.astype(BF16))


def kernel(x, p, positions, attn_norm, w_in, gla_gate_w2, gla_gate_b, gla_out_norm, mla_q_norm,
           mla_w_uq, mla_kv_norm, mla_w_ukv, qk_norm_q, qk_norm_k, w_out, mlp_norm, w_mlp_up,
           w_mlp_down, ple_norm, w_ple_gate, b_ple_gate, w_ple_proj):
    b, s, d = x.shape
    depth = w_in.shape[0]
    t = b * s
    pos2 = positions.reshape(t, 1)
    h = x.reshape(t, d)
    for i in range(depth):
        qg, kg, vg, la, gr, q, k, v = _projection(
            h, pos2, attn_norm[i], w_in[i], gla_gate_w2[i], gla_gate_b[i], mla_q_norm[i],
            mla_w_uq[i], mla_kv_norm[i], mla_w_ukv[i], qk_norm_q[i], qk_norm_k[i])
        seq = lambda a: a.reshape(b, s, a.shape[-1])
        og = _gla(seq(qg), seq(kg), seq(vg), seq(la), seq(gr), gla_out_norm[i])
        om = _mla(seq(q), seq(k), seq(v), qk_norm_q[i], qk_norm_k[i])
        h = _tail(h, og.reshape(t, GLA_WIDTH), om.reshape(t, MLA_WIDTH), p[i].reshape(t, PLE_DIM),
                  w_out[i], mlp_norm[i], w_mlp_up[i], w_mlp_down[i], ple_norm[i], w_ple_gate[i],
                  b_ple_gate[i], w_ple_proj[i])
    return h.reshape(b, s, d)
```

```python
import functools
import math

import jax
import jax.numpy as jnp
from jax import lax
from jax.experimental import pallas as pl
from jax.experimental.pallas import tpu as pltpu

F32 = jnp.float32
BF16 = jnp.bfloat16

D_MODEL = 1024
CHUNK = 64
PLE_DIM = 256
D_FF = 4 * D_MODEL
EPS = 1e-6
GLA_HEADS = 4
GLA_DK = 64
GLA_DV = 128
GLA_GATE_RANK = 16
GLA_TAU = 16.0
GLA_QK = GLA_HEADS * GLA_DK
GLA_WIDTH = GLA_HEADS * GLA_DV
MLA_HEADS = 8
MLA_NOPE = 64
MLA_ROPE = 32
MLA_V = 64
MLA_QK = MLA_NOPE + MLA_ROPE
MLA_Q_RANK = 256
MLA_KV_RANK = 128
MLA_WIDTH = MLA_HEADS * MLA_V
ROPE_THETA = 10000.0
LOG2E = math.log2(math.e)
IN_SPLITS = (GLA_QK, GLA_QK, GLA_WIDTH, GLA_GATE_RANK, GLA_WIDTH, MLA_Q_RANK, MLA_KV_RANK, MLA_ROPE)

LANES = 128
SUBLANES = 8
HALF_ROPE = MLA_ROPE // 2
HEAD_PAD = MLA_HEADS * LANES
D_IN_PAD = 2048
W_IN_BLOCK = 256
VMEM_LIMIT_BYTES = 56 * 1024 * 1024

TM_PROJ = 512
TM_TAIL = 512
FF_CHUNK = 1024
TQ = 512
TK = 256
MLA_UNROLL = 8
ROPE_ROWS = 64
GLA_SEQS_PER_STEP = (8, 4, 2, 1)
GLA_ROWS_PER_STEP = 256
GLA_FAST_MIN_LOG_DECAY = -80.0
MASKED_LOGIT = -1e30
MLA_MAX_SHIFT = 60

ROW_FREQ16, ROW_MASK96, ROW_KRMASK, ROW_GQ, ROW_GQ_SW, ROW_GK, ROW_GK_SW = range(7)


def _const_spec(shape):
    return pl.BlockSpec(shape, lambda *_: (0,) * len(shape), pipeline_mode=pl.Buffered(1))


def _rms_scale(v, n):
    return lax.rsqrt(jnp.sum(v * v, axis=-1, keepdims=True) * (1.0 / n) + EPS)


def _proj_kernel(x_ref, pos_ref, g_attn_ref, w_in_ref, w2_ref, b2_ref, gqn_ref, w_uq_ref,
                 gkvn_ref, w_uk_ref, w_uv_ref, tab_ref,
                 qg_ref, kg_ref, vg_ref, la_ref, gr_ref, q_ref, k_ref, v_ref,
                 qs_ref, ks_ref, slot_ref):
    @pl.when(pl.program_id(0) == 0)
    def _():
        qs_ref[...] = jnp.zeros_like(qs_ref)
        ks_ref[...] = jnp.zeros_like(ks_ref)
        slot_ref[...] = jnp.zeros_like(slot_ref)

    x = x_ref[...]
    xn = (x * _rms_scale(x, D_MODEL) * g_attn_ref[...]).astype(BF16)

    tab = lambda r: tab_ref[r:r + 1, :]
    mask96, krmask = tab(ROW_MASK96), tab(ROW_KRMASK)
    gq, gq_sw = tab(ROW_GQ) * LOG2E, tab(ROW_GQ_SW) * LOG2E
    gk, gk_sw = tab(ROW_GK) * math.sqrt(MLA_QK), tab(ROW_GK_SW) * math.sqrt(MLA_QK)
    half = LANES // 2

    lane = lax.broadcasted_iota(jnp.int32, (1, LANES), 1)
    groups = ROPE_ROWS // SUBLANES
    assert groups * HALF_ROPE == LANES
    x1_lanes = (lane >= 32) & (lane < 32 + HALF_ROPE)
    x2_lanes = (lane >= 32 + HALF_ROPE) & (lane < 64)

    def rope_tables(rows):
        pos = jnp.broadcast_to(pos_ref[rows, :].astype(F32), (ROPE_ROWS, LANES))
        packed = jnp.zeros((SUBLANES, LANES), F32)
        for a in range(groups):
            packed = jnp.where(lane // HALF_ROPE == a, pos[a * SUBLANES:(a + 1) * SUBLANES], packed)
        ang = packed * tab(ROW_FREQ16)
        cos_p, sin_p = jnp.cos(ang), jnp.sin(ang)
        cos, sin = [], []
        for a in range(groups):
            s1 = (32 - HALF_ROPE * a) % LANES
            s2 = (32 + HALF_ROPE - HALF_ROPE * a) % LANES
            c1, c2 = pltpu.roll(cos_p, s1, 1), pltpu.roll(cos_p, s2, 1)
            n1, n2 = pltpu.roll(sin_p, s1, 1), pltpu.roll(sin_p, s2, 1)
            cos.append(jnp.where(x1_lanes, c1, jnp.where(x2_lanes, c2, mask96)))
            sin.append(jnp.where(x1_lanes, -n1, jnp.where(x2_lanes, n2, 0.0)))
        return jnp.concatenate(cos, axis=0), jnp.concatenate(sin, axis=0)

    sum96 = (lax.broadcasted_iota(jnp.int32, (LANES, LANES), 0) < MLA_QK).astype(BF16)

    def head_rows(rb):
        rows = pl.ds(rb * ROPE_ROWS, ROPE_ROWS)
        cos, sin = rope_tables(rows)
        kr = slot_ref[rows, :] * krmask
        kr_rot = pltpu.roll(kr, half, 1) * (sin * gk_sw)
        cos_q, sin_q, cos_k = cos * gq, sin * gq_sw, cos * gk
        slabs = []
        for h in range(MLA_HEADS):
            sl = slice(h * LANES, (h + 1) * LANES)
            slabs += [qs_ref[rows, sl], ks_ref[rows, sl] + kr]
        squares = jnp.concatenate([(v * v).astype(BF16) for v in slabs], axis=0)
        rms = lax.rsqrt(jnp.dot(squares, sum96, preferred_element_type=F32) + MLA_QK * EPS)
        for h in range(MLA_HEADS):
            sl = slice(h * LANES, (h + 1) * LANES)
            qh, kh = slabs[2 * h], slabs[2 * h + 1]
            rq = rms[(2 * h) * ROPE_ROWS:(2 * h + 1) * ROPE_ROWS]
            rk = rms[(2 * h + 1) * ROPE_ROWS:(2 * h + 2) * ROPE_ROWS]
            q_ref[rows, sl] = ((qh * cos_q + pltpu.roll(qh, half, 1) * sin_q) * rq).astype(BF16)
            k_ref[rows, sl] = ((kh * cos_k + kr_rot) * rk).astype(BF16)

    assert x_ref.shape[0] // ROPE_ROWS == w_in_ref.shape[0] == 8
    cq = ckv = slot = None
    for r in range(w_in_ref.shape[0]):
        z = jnp.dot(xn, w_in_ref[r], preferred_element_type=F32)
        head_rows(r)
        half_cols = slice((r % 2) * W_IN_BLOCK, (r % 2 + 1) * W_IN_BLOCK)
        if r == 0:
            qg_ref[...] = (z * (GLA_DK ** -0.5)).astype(BF16)
        elif r == 1:
            kg_ref[...] = z.astype(BF16)
        elif r in (2, 3):
            vg_ref[:, half_cols] = z.astype(BF16)
        elif r in (4, 5):
            gr_ref[:, half_cols] = (z * jax.nn.sigmoid(z)).astype(BF16)
        elif r == 6:
            cq = z
        else:
            ckv, slot = z[:, :MLA_KV_RANK], z[:, MLA_KV_RANK:]
    slot_ref[...] = slot

    xg = jnp.dot(slot.astype(BF16), w2_ref[...], preferred_element_type=F32) + b2_ref[...]
    la_ref[...] = (jnp.minimum(xg, 0.0) - jnp.log(1.0 + jnp.exp(-jnp.abs(xg)))) * (1.0 / GLA_TAU)

    cqn = (cq * _rms_scale(cq, MLA_Q_RANK) * gqn_ref[...]).astype(BF16)
    qs_ref[...] = jnp.dot(cqn, w_uq_ref[...], preferred_element_type=F32)
    ckvn = (ckv * _rms_scale(ckv, MLA_KV_RANK) * gkvn_ref[...]).astype(BF16)
    ks_ref[...] = jnp.dot(ckvn, w_uk_ref[...], preferred_element_type=F32)
    v_ref[...] = jnp.dot(ckvn, w_uv_ref[...], preferred_element_type=F32).astype(BF16)


def _head_layout(nope, rope):
    x1, x2 = rope[..., :HALF_ROPE], rope[..., HALF_ROPE:]
    return jnp.concatenate([nope[..., :32], x1, x2, nope[..., 32:], x2, x1], axis=-1)


def _projection(x2, pos2, attn_norm, w_in, gate_w2, gate_b, q_norm, w_uq, kv_norm, w_ukv,
                qk_norm_q, qk_norm_k):
    t = x2.shape[0]
    tm = TM_PROJ
    assert t % tm == 0

    gq, gk, gv, g_low, g_r, c_q, c_kv, k_r = jnp.split(w_in, list(_cumsum(IN_SPLITS))[:-1], axis=1)
    zeros = lambda n: jnp.zeros((D_MODEL, n), w_in.dtype)
    kr_sw = jnp.concatenate([k_r[:, HALF_ROPE:], k_r[:, :HALF_ROPE]], axis=1)
    slot = jnp.concatenate([g_low, zeros(32 - GLA_GATE_RANK), k_r, zeros(32), kr_sw], axis=1)
    w_in_l = jnp.concatenate([gq, gk, gv, g_r, c_q, c_kv, slot], axis=1).astype(BF16)
    assert w_in_l.shape == (D_MODEL, D_IN_PAD)

    w2_l = jnp.zeros((LANES, GLA_QK), F32).at[:GLA_GATE_RANK].set(gate_w2).astype(BF16)

    w_uq_h = w_uq.reshape(MLA_Q_RANK, MLA_HEADS, MLA_QK)
    w_uq_l = _head_layout(w_uq_h[..., :MLA_NOPE], w_uq_h[..., MLA_NOPE:]).reshape(MLA_Q_RANK, HEAD_PAD)
    w_ukv_h = w_ukv.reshape(MLA_KV_RANK, MLA_HEADS, MLA_NOPE + MLA_V)
    w_uk_l = _head_layout(w_ukv_h[..., :MLA_NOPE],
                          jnp.zeros((MLA_KV_RANK, MLA_HEADS, MLA_ROPE), F32)).reshape(MLA_KV_RANK, HEAD_PAD)
    w_uv = w_ukv_h[..., MLA_NOPE:].reshape(MLA_KV_RANK, MLA_WIDTH)
    gq_l = _head_layout(qk_norm_q[:MLA_NOPE], qk_norm_q[MLA_NOPE:])
    gk_l = _head_layout(qk_norm_k[:MLA_NOPE], qk_norm_k[MLA_NOPE:])

    inv_freq = ROPE_THETA ** (-jnp.arange(0, MLA_ROPE, 2, dtype=F32) / MLA_ROPE)
    z32 = jnp.zeros((32,), F32)
    tab = jnp.zeros((SUBLANES, LANES), F32)
    tab = tab.at[ROW_FREQ16].set(jnp.tile(inv_freq, LANES // HALF_ROPE))
    tab = tab.at[ROW_MASK96].set(jnp.concatenate([jnp.ones((96,), F32), z32]))
    tab = tab.at[ROW_KRMASK].set(jnp.concatenate([z32, jnp.ones((32,), F32), z32, jnp.ones((32,), F32)]))
    tab = tab.at[ROW_GQ].set(gq_l).at[ROW_GQ_SW].set(jnp.roll(gq_l, LANES // 2))
    tab = tab.at[ROW_GK].set(gk_l).at[ROW_GK_SW].set(jnp.roll(gk_l, LANES // 2))

    n_blocks = D_IN_PAD // W_IN_BLOCK
    w_in_l = w_in_l.reshape(D_MODEL, n_blocks, W_IN_BLOCK).transpose(1, 0, 2)

    n = t // tm
    tile = lambda w: pl.BlockSpec((tm, w), lambda i: (jnp.minimum(i, n - 1), 0))
    prev = lambda w: pl.BlockSpec((tm, w), lambda i: (jnp.maximum(i - 1, 0), 0))
    out_widths = (GLA_QK, GLA_QK, GLA_WIDTH, GLA_QK, GLA_WIDTH, HEAD_PAD, HEAD_PAD, MLA_WIDTH)
    out_dtypes = (BF16, BF16, BF16, F32, BF16, BF16, BF16, BF16)
    out_specs = [tile(w) for w in out_widths]
    out_specs[5], out_specs[6] = prev(HEAD_PAD), prev(HEAD_PAD)
    return pl.pallas_call(
        _proj_kernel,
        grid=(n + 1,),
        in_specs=[
            tile(D_MODEL), prev(1),
            _const_spec((1, D_MODEL)), _const_spec((n_blocks, D_MODEL, W_IN_BLOCK)),
            _const_spec((LANES, GLA_QK)), _const_spec((1, GLA_QK)),
            _const_spec((1, MLA_Q_RANK)), _const_spec((MLA_Q_RANK, HEAD_PAD)),
            _const_spec((1, MLA_KV_RANK)), _const_spec((MLA_KV_RANK, HEAD_PAD)),
            _const_spec((MLA_KV_RANK, MLA_WIDTH)),
            _const_spec((8, LANES)),
        ],
        out_specs=out_specs,
        out_shape=[jax.ShapeDtypeStruct((t, w), d) for w, d in zip(out_widths, out_dtypes)],
        scratch_shapes=[pltpu.VMEM((tm, HEAD_PAD), F32), pltpu.VMEM((tm, HEAD_PAD), F32),
                        pltpu.VMEM((tm, LANES), F32)],
        compiler_params=pltpu.CompilerParams(
            dimension_semantics=("arbitrary",), vmem_limit_bytes=VMEM_LIMIT_BYTES),
        name="proj",
    )(x2, pos2, attn_norm.reshape(1, D_MODEL), w_in_l, w2_l, gate_b.reshape(1, GLA_QK),
      q_norm.reshape(1, MLA_Q_RANK), w_uq_l.astype(BF16), kv_norm.reshape(1, MLA_KV_RANK),
      w_uk_l.astype(BF16), w_uv.astype(BF16), tab)


def _cumsum(sizes):
    total = 0
    for s in sizes:
        total += s
        yield total


def _gla_kernel(q_ref, k_ref, v_ref, la_ref, gr_ref, gon_ref, o_ref, st_ref, kf_ref, b_ref):
    nseq, rows_per_step = q_ref.shape[0], q_ref.shape[1]

    @pl.when(pl.program_id(1) == 0)
    def _():
        st_ref[...] = jnp.zeros_like(st_ref)

    row = lax.broadcasted_iota(jnp.int32, (CHUNK, CHUNK), 0)
    col = lax.broadcasted_iota(jnp.int32, (CHUNK, CHUNK), 1)
    causal = col <= row
    lane = lax.broadcasted_iota(jnp.int32, (1, LANES), 1)
    lane_q = lax.broadcasted_iota(jnp.int32, (1, GLA_QK), 1)
    head_mask = [(lane // GLA_DK) == e for e in range(2)]
    gon = gon_ref[...]
    nt = (((1,), (1,)), ((), ()))
    tn = (((0,), (0,)), ((), ()))

    def exact_scores(q, k, b):
        kf_ref[...] = k
        b_ref[...] = b

        def key_row(s, sc):
            k_s = kf_ref[pl.ds(s, 1), :]
            b_s = b_ref[pl.ds(s, 1), :]
            prod = q * k_s * jnp.exp(jnp.minimum(b - b_s, 0.0))
            out = []
            for h in range(GLA_HEADS):
                hm = (lane_q // GLA_DK) == h
                col_h = jnp.sum(jnp.where(hm, prod, 0.0), axis=-1, keepdims=True)
                out.append(jnp.where(col == s, col_h, sc[h]))
            return tuple(out)

        zero = jnp.zeros((CHUNK, CHUNK), F32)
        return lax.fori_loop(0, CHUNK, key_row, (zero,) * GLA_HEADS)

    pair = lambda h: slice((h // 2) * LANES, (h // 2 + 1) * LANES)
    vsl = lambda h: slice(h * GLA_DV, (h + 1) * GLA_DV)
    heads = range(GLA_HEADS)

    def row_cumsum(x):
        row_id = lax.broadcasted_iota(jnp.int32, x.shape, 0)
        step = 1
        while step < CHUNK:
            if step < SUBLANES:
                moved = jnp.where(row_id >= step, pltpu.roll(x, step, 0), 0.0)
            else:
                moved = jnp.concatenate([jnp.zeros((step, x.shape[1]), x.dtype), x[:-step]], axis=0)
            x = x + moved
            step *= 2
        return x

    def chunk(seqs, c, factorised):
        rows = pl.ds(pl.multiple_of(c * CHUNK, CHUNK), CHUNK)
        q, k, b, a_h, kt, kd, eb_last, scores, o_inter = {}, {}, {}, {}, {}, {}, {}, {}, {}
        for n in seqs:
            b[n] = row_cumsum(la_ref[n, rows, :])
            b_last = b[n][CHUNK - 1:CHUNK, :]
            q[n] = q_ref[n, rows, :].astype(F32)
            k[n] = k_ref[n, rows, :].astype(F32)
            qt = q[n] * jnp.exp(b[n])
            eb_last[n] = jnp.exp(b_last)
            if factorised:
                k_grown = k[n] * jnp.exp(-b[n])
                kt[n] = k_grown.astype(BF16)
                kd[n] = k_grown * eb_last[n]
            else:
                kd[n] = k[n] * jnp.exp(b_last - b[n])
            a_h[n] = [jnp.where(head_mask[h % 2], qt[:, pair(h)], 0.0).astype(BF16) for h in heads]
        v, st, o = {}, {}, {}
        for n in seqs:
            for h in heads:
                st[n, h] = st_ref[n, h]
                if factorised:
                    both = lax.dot_general(
                        a_h[n][h], jnp.concatenate([st[n, h].astype(BF16), kt[n][:, pair(h)]], axis=0), nt,
                        preferred_element_type=F32)
                    o_inter[n, h], scores[n, h] = both[:, :GLA_DV], both[:, GLA_DV:]
                else:
                    o_inter[n, h] = lax.dot_general(a_h[n][h], st[n, h].astype(BF16), nt,
                                                    preferred_element_type=F32)
            if not factorised:
                for h, sc in enumerate(exact_scores(q[n], k[n], b[n])):
                    scores[n, h] = sc
        for n in seqs:
            for h in heads:
                sc = jnp.where(causal, scores[n, h], 0.0).astype(BF16)
                v[n, h] = v_ref[n, rows, vsl(h)]
                o[n, h] = o_inter[n, h] + jnp.dot(sc, v[n, h], preferred_element_type=F32)
        for n in seqs:
            for h in heads:
                kd_h = jnp.where(head_mask[h % 2], kd[n][:, pair(h)], 0.0).astype(BF16)
                upd = lax.dot_general(v[n, h], kd_h, tn, preferred_element_type=F32)
                st_ref[n, h] = st[n, h] * eb_last[n][:, pair(h)] + upd
        for n in seqs:
            for h in heads:
                on = o[n, h] * _rms_scale(o[n, h], GLA_DV) * gon
                o_ref[n, rows, vsl(h)] = (on * gr_ref[n, rows, vsl(h)].astype(F32)).astype(BF16)

    def factorised_chunks():
        def body(c, carry):
            chunk(range(nseq), c, True)
            return carry
        lax.fori_loop(0, rows_per_step // CHUNK, body, 0)

    def exact_chunks():
        def body(c, carry):
            for n in range(nseq):
                chunk([n], c, False)
            return carry
        lax.fori_loop(0, rows_per_step // CHUNK, body, 0)

    factorisable = jnp.min(la_ref[...]) >= GLA_FAST_MIN_LOG_DECAY / CHUNK
    lax.cond(factorisable, factorised_chunks, exact_chunks)


def _gla(qg, kg, vg, la, gr, out_norm):
    b, s, _ = qg.shape
    nseq = max(n for n in GLA_SEQS_PER_STEP if b % n == 0)
    rows = min(GLA_ROWS_PER_STEP, s)
    assert s % rows == 0 and rows % CHUNK == 0
    blk = lambda w: pl.BlockSpec((nseq, rows, w), lambda i, j: (i, j, 0))
    return pl.pallas_call(
        _gla_kernel,
        grid=(b // nseq, s // rows),
        in_specs=[blk(GLA_QK), blk(GLA_QK), blk(GLA_WIDTH), blk(GLA_QK), blk(GLA_WIDTH),
                  _const_spec((1, GLA_DV))],
        out_specs=blk(GLA_WIDTH),
        out_shape=jax.ShapeDtypeStruct((b, s, GLA_WIDTH), BF16),
        scratch_shapes=[pltpu.VMEM((nseq, GLA_HEADS, GLA_DV, LANES), F32),
                        pltpu.VMEM((CHUNK, GLA_QK), F32),
                        pltpu.VMEM((CHUNK, GLA_QK), F32)],
        compiler_params=pltpu.CompilerParams(
            dimension_semantics=("arbitrary", "arbitrary"), vmem_limit_bytes=VMEM_LIMIT_BYTES),
        name="gla",
    )(qg, kg, vg, la, gr, out_norm.reshape(1, GLA_DV))


def _mla_tiles(n_q):
    assert TQ == 2 * TK
    first = [(a, 2 * a, 0) for a in range(n_q)]
    full = [(a, j, TQ // CHUNK) for a in range(n_q) for j in range(2 * a)]
    second = [(a, 2 * a + 1, -(TK // CHUNK)) for a in range(n_q)]
    return tuple(zip(*(first + full + second)))


def _mla_kernel(ti_ref, tj_ref, toff_ref, bound_ref, q_ref, k_ref, v_ref, o_ref,
                s_ref, m_ref, acc_ref, v1_ref):
    seq = q_ref.shape[1]
    n_q = seq // TQ
    n_tiles = s_ref.shape[1]
    nt = (((1,), (1,)), ((), ()))
    heads = range(2)
    lane = lax.broadcasted_iota(jnp.int32, (1, LANES), 1)
    own = [(lane // MLA_V) == e for e in heads]
    unroll = lambda count: math.gcd(count, MLA_UNROLL)

    def q_rows(i):
        return pl.ds(pl.multiple_of(i * TQ, TQ), TQ)

    def k_rows(j):
        return pl.ds(pl.multiple_of(j * TK, TK), TK)

    def masked_logits(e, t):
        hsl = slice(e * LANES, (e + 1) * LANES)
        s = lax.dot_general(q_ref[0, q_rows(ti_ref[t]), hsl], k_ref[0, k_rows(tj_ref[t]), hsl], nt,
                            preferred_element_type=F32)
        row_chunk = lax.broadcasted_iota(jnp.int32, (TQ, TK), 0) // CHUNK
        col_chunk = lax.broadcasted_iota(jnp.int32, (TQ, TK), 1) // CHUNK
        return jnp.where(col_chunk <= row_chunk + toff_ref[t], s, MASKED_LOGIT)

    def lane_fold_max(a):
        out = a[:, :LANES]
        for c in range(1, TK // LANES):
            out = jnp.maximum(out, a[:, c * LANES:(c + 1) * LANES])
        return out

    v = v_ref[0]
    for e in heads:
        v1_ref[e] = jnp.where(own[e], v, 1.0).astype(BF16)

    def exp_shifted(s, shift):
        return jnp.concatenate(
            [jnp.exp2(s[:, c * LANES:(c + 1) * LANES] - shift) for c in range(TK // LANES)],
            axis=1).astype(BF16)

    def accumulate(e, t, s, shift):
        acc_ref[e, ti_ref[t]] += jnp.dot(exp_shifted(s, shift), v1_ref[e, k_rows(tj_ref[t]), :],
                                         preferred_element_type=F32)

    def clear(i, carry):
        for e in heads:
            acc_ref[e, i] = jnp.zeros((TQ, LANES), F32)
        return carry

    def single_pass():
        lax.fori_loop(0, n_q, clear, 0)
        shift = bound_ref[0].astype(F32)

        def tile(t, carry):
            for e in heads:
                accumulate(e, t, masked_logits(e, t), shift)
            return carry

        lax.fori_loop(0, n_tiles - n_q, tile, 0, unroll=unroll(n_tiles - n_q))

        def lower_half_tile(t, carry):
            i = ti_ref[t]
            lower = pl.ds(pl.multiple_of(i * TQ + TK, TK), TK)
            row_chunk = lax.broadcasted_iota(jnp.int32, (TK, TK), 0) // CHUNK
            col_chunk = lax.broadcasted_iota(jnp.int32, (TK, TK), 1) // CHUNK
            for e in heads:
                hsl = slice(e * LANES, (e + 1) * LANES)
                s = lax.dot_general(q_ref[0, lower, hsl], k_ref[0, k_rows(tj_ref[t]), hsl], nt,
                                    preferred_element_type=F32)
                p = exp_shifted(jnp.where(col_chunk <= row_chunk, s, MASKED_LOGIT), shift)
                acc_ref[e, i, TK:, :] += jnp.dot(p, v1_ref[e, k_rows(tj_ref[t]), :],
                                                 preferred_element_type=F32)
            return carry

        lax.fori_loop(n_tiles - n_q, n_tiles, lower_half_tile, 0, unroll=unroll(n_q))

    def two_pass():
        def first_tile(t, carry):
            for e in heads:
                s = masked_logits(e, t)
                s_ref[e, t] = s
                m_ref[e, t] = lane_fold_max(s)
            return clear(t, carry)

        lax.fori_loop(0, n_q, first_tile, 0, unroll=unroll(n_q))

        def pass1(t, carry):
            i = ti_ref[t]
            for e in heads:
                s = masked_logits(e, t)
                s_ref[e, t] = s
                m_ref[e, i] = jnp.maximum(m_ref[e, i], lane_fold_max(s))
            return carry

        lax.fori_loop(n_q, n_tiles, pass1, 0, unroll=unroll(n_tiles - n_q))

        def row_max(i, carry):
            for e in heads:
                m_ref[e, i] = jnp.broadcast_to(jnp.max(m_ref[e, i], axis=-1, keepdims=True), (TQ, LANES))
            return carry

        lax.fori_loop(0, n_q, row_max, 0)

        def pass2(t, carry):
            for e in heads:
                accumulate(e, t, s_ref[e, t], m_ref[e, ti_ref[t]])
            return carry

        lax.fori_loop(0, n_tiles, pass2, 0, unroll=unroll(n_tiles))

    lax.cond(bound_ref[0] <= MLA_MAX_SHIFT, single_pass, two_pass)

    def finish(i, carry):
        acc0, acc1 = acc_ref[0, i], acc_ref[1, i]
        num = jnp.where(own[0], acc0, acc1)
        den = pltpu.roll(jnp.where(own[0], acc1, acc0), LANES // 2, 1)
        o_ref[0, q_rows(i), :] = (num / den).astype(BF16)
        return carry

    lax.fori_loop(0, n_q, finish, 0)


def _mla(q, k, v, gain_q, gain_k):
    b, s, _ = q.shape
    assert s % TQ == 0 and TQ % TK == 0
    n_q = s // TQ
    ti, tj, toff = _mla_tiles(n_q)
    bound = jnp.ceil(1.02 * LOG2E * math.sqrt(MLA_QK) * jnp.max(jnp.abs(gain_q)) * jnp.max(jnp.abs(gain_k)))
    bound = jnp.minimum(bound, 2.0 * MLA_MAX_SHIFT).astype(jnp.int32).reshape(1)
    blk = lambda w: pl.BlockSpec((1, s, w), lambda i, p, *_: (i, 0, p))
    return pl.pallas_call(
        _mla_kernel,
        grid_spec=pltpu.PrefetchScalarGridSpec(
            num_scalar_prefetch=4,
            grid=(b, MLA_HEADS // 2),
            in_specs=[blk(2 * LANES), blk(2 * LANES), blk(LANES)],
            out_specs=blk(LANES),
            scratch_shapes=[pltpu.VMEM((2, len(ti), TQ, TK), F32),
                            pltpu.VMEM((2, n_q, TQ, LANES), F32),
                            pltpu.VMEM((2, n_q, TQ, LANES), F32),
                            pltpu.VMEM((2, s, LANES), BF16)]),
        out_shape=jax.ShapeDtypeStruct((b, s, MLA_WIDTH), BF16),
        compiler_params=pltpu.CompilerParams(
            dimension_semantics=("arbitrary", "arbitrary"),
            vmem_limit_bytes=VMEM_LIMIT_BYTES),
        name="mla",
    )(jnp.asarray(ti, jnp.int32), jnp.asarray(tj, jnp.int32), jnp.asarray(toff, jnp.int32), bound,
      q, k, v)


def _tail_kernel(x_ref, og_ref, om_ref, p_ref, w_out_ref, g_mlp_ref, w_up_ref, w_down_ref,
                 g_ple_ref, w_gate_ref, b_gate_ref, w_pp_ref, o_ref):
    h = (x_ref[...]
         + jnp.dot(og_ref[...], w_out_ref[:GLA_WIDTH, :], preferred_element_type=F32)
         + jnp.dot(om_ref[...], w_out_ref[GLA_WIDTH:, :], preferred_element_type=F32))
    m = (h * _rms_scale(h, D_MODEL) * g_mlp_ref[...]).astype(BF16)
    mlp = None
    for c in range(D_FF // FF_CHUNK):
        cols = slice(c * FF_CHUNK, (c + 1) * FF_CHUNK)
        u = jnp.maximum(jnp.dot(m, w_up_ref[:, cols], preferred_element_type=F32), 0.0)
        d = jnp.dot((u * u).astype(BF16), w_down_ref[cols, :], preferred_element_type=F32)
        mlp = d if mlp is None else mlp + d
    h = h + mlp
    g = (h * _rms_scale(h, D_MODEL) * g_ple_ref[...]).astype(BF16)
    gate = jax.nn.sigmoid(jnp.dot(g, w_gate_ref[...], preferred_element_type=F32) + b_gate_ref[...])
    pp = jnp.dot(p_ref[...].astype(BF16), w_pp_ref[...], preferred_element_type=F32)
    o_ref[...] = h + pp * gate


def _tail(x2, og, om, p2, w_out, mlp_norm, w_up, w_down, ple_norm, w_gate, b_gate, w_pp):
    t = x2.shape[0]
    tm = TM_TAIL
    assert t % tm == 0
    row = lambda w: pl.BlockSpec((tm, w), lambda i: (i, 0))
    return pl.pallas_call(
        _tail_kernel,
        grid=(t // tm,),
        in_specs=[row(D_MODEL), row(GLA_WIDTH), row(MLA_WIDTH), row(PLE_DIM),
                  _const_spec((D_MODEL, D_MODEL)), _const_spec((1, D_MODEL)),
                  _const_spec((D_MODEL, D_FF)), _const_spec((D_FF, D_MODEL)),
                  _const_spec((1, D_MODEL)), _const_spec((D_MODEL, D_MODEL)),
                  _const_spec((1, D_MODEL)), _const_spec((PLE_DIM, D_MODEL))],
        out_specs=row(D_MODEL),
        out_shape=jax.ShapeDtypeStruct((t, D_MODEL), F32),
        compiler_params=pltpu.CompilerParams(
            dimension_semantics=("arbitrary",), vmem_limit_bytes=VMEM_LIMIT_BYTES),
        name="tail",
    )(x2, og, om, p2, w_out.astype(BF16), mlp_norm.reshape(1, D_MODEL), w_up.astype(BF16),
      w_down.astype(BF16), ple_norm.reshape(1, D_MODEL), w_gate.astype(BF16),
      b_gate.reshape(1, D_MODEL), w_pp.astype(BF16))


def kernel(x, p, positions, attn_norm, w_in, gla_gate_w2, gla_gate_b, gla_out_norm, mla_q_norm,
           mla_w_uq, mla_kv_norm, mla_w_ukv, qk_norm_q, qk_norm_k, w_out, mlp_norm, w_mlp_up,
           w_mlp_down, ple_norm, w_ple_gate, b_ple_gate, w_ple_proj):
    b, s, d = x.shape
    depth = w_in.shape[0]
    t = b * s
    pos2 = positions.reshape(t, 1)
    h = x.reshape(t, d)
    for i in range(depth):
        qg, kg, vg, la, gr, q, k, v = _projection(
            h, pos2, attn_norm[i], w_in[i], gla_gate_w2[i], gla_gate_b[i], mla_q_norm[i],
            mla_w_uq[i], mla_kv_norm[i], mla_w_ukv[i], qk_norm_q[i], qk_norm_k[i])
        seq = lambda a: a.reshape(b, s, a.shape[-1])
        og = _gla(seq(qg), seq(kg), seq(vg), seq(la), seq(gr), gla_out_norm[i])
        om = _mla(seq(q), seq(k), seq(v), qk_norm_q[i], qk_norm_k[i])
        h = _tail(h, og.reshape(t, GLA_WIDTH), om.reshape(t, MLA_WIDTH), p[i].reshape(t, PLE_DIM),
                  w_out[i], mlp_norm[i], w_mlp_up[i], w_mlp_down[i], ple_norm[i], w_ple_gate[i],
                  b_ple_gate[i], w_ple_proj[i])
    return h.reshape(b, s, d)
```

```python
import functools
import math

import jax
import jax.numpy as jnp
from jax import lax
from jax.experimental import pallas as pl
from jax.experimental.pallas import tpu as pltpu

F32 = jnp.float32
BF16 = jnp.bfloat16

D_MODEL = 1024
CHUNK = 64
PLE_DIM = 256
D_FF = 4 * D_MODEL
EPS = 1e-6
GLA_HEADS = 4
GLA_DK = 64
GLA_DV = 128
GLA_GATE_RANK = 16
GLA_TAU = 16.0
GLA_QK = GLA_HEADS * GLA_DK
GLA_WIDTH = GLA_HEADS * GLA_DV
MLA_HEADS = 8
MLA_NOPE = 64
MLA_ROPE = 32
MLA_V = 64
MLA_QK = MLA_NOPE + MLA_ROPE
MLA_Q_RANK = 256
MLA_KV_RANK = 128
MLA_WIDTH = MLA_HEADS * MLA_V
ROPE_THETA = 10000.0
LOG2E = math.log2(math.e)
IN_SPLITS = (GLA_QK, GLA_QK, GLA_WIDTH, GLA_GATE_RANK, GLA_WIDTH, MLA_Q_RANK, MLA_KV_RANK, MLA_ROPE)

LANES = 128
SUBLANES = 8
HALF_ROPE = MLA_ROPE // 2
HEAD_PAD = MLA_HEADS * LANES
D_IN_PAD = 2048
W_IN_BLOCK = 256
VMEM_LIMIT_BYTES = 56 * 1024 * 1024

TM_PROJ = 512
TM_TAIL = 512
FF_CHUNK = 1024
TQ = 512
TK = 256
MLA_UNROLL = 8
ROPE_ROWS = 64
GLA_SEQS_PER_STEP = (8, 4, 2, 1)
GLA_ROWS_PER_STEP = 256
GLA_FAST_MIN_LOG_DECAY = -80.0
MASKED_LOGIT = -1e30
MLA_MAX_SHIFT = 60

ROW_FREQ16, ROW_MASK96, ROW_KRMASK, ROW_GQ, ROW_GQ_SW, ROW_GK, ROW_GK_SW = range(7)


def _const_spec(shape):
    return pl.BlockSpec(shape, lambda *_: (0,) * len(shape), pipeline_mode=pl.Buffered(1))


def _rms_scale(v, n):
    return lax.rsqrt(jnp.sum(v * v, axis=-1, keepdims=True) * (1.0 / n) + EPS)


def _proj_kernel(x_ref, pos_ref, g_attn_ref, w_in_ref, w2_ref, b2_ref, gqn_ref, w_uq_ref,
                 gkvn_ref, w_uk_ref, w_uv_ref, tab_ref,
                 qg_ref, kg_ref, vg_ref, la_ref, gr_ref, q_ref, k_ref, v_ref,
                 qs_ref, ks_ref, slot_ref):
    @pl.when(pl.program_id(0) == 0)
    def _():
        qs_ref[...] = jnp.zeros_like(qs_ref)
        ks_ref[...] = jnp.zeros_like(ks_ref)
        slot_ref[...] = jnp.zeros_like(slot_ref)

    x = x_ref[...]
    xn = (x * _rms_scale(x, D_MODEL) * g_attn_ref[...]).astype(BF16)

    tab = lambda r: tab_ref[r:r + 1, :]
    mask96, krmask = tab(ROW_MASK96), tab(ROW_KRMASK)
    gq, gq_sw = tab(ROW_GQ) * LOG2E, tab(ROW_GQ_SW) * LOG2E
    gk, gk_sw = tab(ROW_GK) * math.sqrt(MLA_QK), tab(ROW_GK_SW) * math.sqrt(MLA_QK)
    half = LANES // 2

    lane = lax.broadcasted_iota(jnp.int32, (1, LANES), 1)
    groups = ROPE_ROWS // SUBLANES
    assert groups * HALF_ROPE == LANES
    x1_lanes = (lane >= 32) & (lane < 32 + HALF_ROPE)
    x2_lanes = (lane >= 32 + HALF_ROPE) & (lane < 64)

    def rope_tables(rows):
        pos = jnp.broadcast_to(pos_ref[rows, :].astype(F32), (ROPE_ROWS, LANES))
        packed = jnp.zeros((SUBLANES, LANES), F32)
        for a in range(groups):
            packed = jnp.where(lane // HALF_ROPE == a, pos[a * SUBLANES:(a + 1) * SUBLANES], packed)
        ang = packed * tab(ROW_FREQ16)
        cos_p, sin_p = jnp.cos(ang), jnp.sin(ang)
        cos, sin = [], []
        for a in range(groups):
            s1 = (32 - HALF_ROPE * a) % LANES
            s2 = (32 + HALF_ROPE - HALF_ROPE * a) % LANES
            c1, c2 = pltpu.roll(cos_p, s1, 1), pltpu.roll(cos_p, s2, 1)
            n1, n2 = pltpu.roll(sin_p, s1, 1), pltpu.roll(sin_p, s2, 1)
            cos.append(jnp.where(x1_lanes, c1, jnp.where(x2_lanes, c2, mask96)))
            sin.append(jnp.where(x1_lanes, -n1, jnp.where(x2_lanes, n2, 0.0)))
        return jnp.concatenate(cos, axis=0), jnp.concatenate(sin, axis=0)

    sum96 = (lax.broadcasted_iota(jnp.int32, (LANES, LANES), 0) < MLA_QK).astype(BF16)

    def head_rows(rb):
        rows = pl.ds(rb * ROPE_ROWS, ROPE_ROWS)
        cos, sin = rope_tables(rows)
        kr = slot_ref[rows, :] * krmask
        kr_rot = pltpu.roll(kr, half, 1) * (sin * gk_sw)
        cos_q, sin_q, cos_k = cos * gq, sin * gq_sw, cos * gk
        slabs = []
        for h in range(MLA_HEADS):
            sl = slice(h * LANES, (h + 1) * LANES)
            slabs += [qs_ref[rows, sl], ks_ref[rows, sl] + kr]
        squares = jnp.concatenate([(v * v).astype(BF16) for v in slabs], axis=0)
        rms = lax.rsqrt(jnp.dot(squares, sum96, preferred_element_type=F32) + MLA_QK * EPS)
        for h in range(MLA_HEADS):
            sl = slice(h * LANES, (h + 1) * LANES)
            qh, kh = slabs[2 * h], slabs[2 * h + 1]
            rq = rms[(2 * h) * ROPE_ROWS:(2 * h + 1) * ROPE_ROWS]
            rk = rms[(2 * h + 1) * ROPE_ROWS:(2 * h + 2) * ROPE_ROWS]
            q_ref[rows, sl] = ((qh * cos_q + pltpu.roll(qh, half, 1) * sin_q) * rq).astype(BF16)
            k_ref[rows, sl] = ((kh * cos_k + kr_rot) * rk).astype(BF16)

    assert x_ref.shape[0] // ROPE_ROWS == w_in_ref.shape[0] == 8
    cq = ckv = slot = None
    for r in range(w_in_ref.shape[0]):
        z = jnp.dot(xn, w_in_ref[r], preferred_element_type=F32)
        head_rows(r)
        half_cols = slice((r % 2) * W_IN_BLOCK, (r % 2 + 1) * W_IN_BLOCK)
        if r == 0:
            qg_ref[...] = (z * (GLA_DK ** -0.5)).astype(BF16)
        elif r == 1:
            kg_ref[...] = z.astype(BF16)
        elif r in (2, 3):
            vg_ref[:, half_cols] = z.astype(BF16)
        elif r in (4, 5):
            gr_ref[:, half_cols] = (z * jax.nn.sigmoid(z)).astype(BF16)
        elif r == 6:
            cq = z
        else:
            ckv, slot = z[:, :MLA_KV_RANK], z[:, MLA_KV_RANK:]
    slot_ref[...] = slot

    xg = jnp.dot(slot.astype(BF16), w2_ref[...], preferred_element_type=F32) + b2_ref[...]
    la_ref[...] = (jnp.minimum(xg, 0.0) - jnp.log(1.0 + jnp.exp(-jnp.abs(xg)))) * (1.0 / GLA_TAU)

    cqn = (cq * _rms_scale(cq, MLA_Q_RANK) * gqn_ref[...]).astype(BF16)
    qs_ref[...] = jnp.dot(cqn, w_uq_ref[...], preferred_element_type=F32)
    ckvn = (ckv * _rms_scale(ckv, MLA_KV_RANK) * gkvn_ref[...]).astype(BF16)
    ks_ref[...] = jnp.dot(ckvn, w_uk_ref[...], preferred_element_type=F32)
    v_ref[...] = jnp.dot(ckvn, w_uv_ref[...], preferred_element_type=F32).astype(BF16)


def _head_layout(nope, rope):
    x1, x2 = rope[..., :HALF_ROPE], rope[..., HALF_ROPE:]
    return jnp.concatenate([nope[..., :32], x1, x2, nope[..., 32:], x2, x1], axis=-1)


def _projection(x2, pos2, attn_norm, w_in, gate_w2, gate_b, q_norm, w_uq, kv_norm, w_ukv,
                qk_norm_q, qk_norm_k):
    t = x2.shape[0]
    tm = TM_PROJ
    assert t % tm == 0

    gq, gk, gv, g_low, g_r, c_q, c_kv, k_r = jnp.split(w_in, list(_cumsum(IN_SPLITS))[:-1], axis=1)
    zeros = lambda n: jnp.zeros((D_MODEL, n), w_in.dtype)
    kr_sw = jnp.concatenate([k_r[:, HALF_ROPE:], k_r[:, :HALF_ROPE]], axis=1)
    slot = jnp.concatenate([g_low, zeros(32 - GLA_GATE_RANK), k_r, zeros(32), kr_sw], axis=1)
    w_in_l = jnp.concatenate([gq, gk, gv, g_r, c_q, c_kv, slot], axis=1).astype(BF16)
    assert w_in_l.shape == (D_MODEL, D_IN_PAD)

    w2_l = jnp.zeros((LANES, GLA_QK), F32).at[:GLA_GATE_RANK].set(gate_w2).astype(BF16)

    w_uq_h = w_uq.reshape(MLA_Q_RANK, MLA_HEADS, MLA_QK)
    w_uq_l = _head_layout(w_uq_h[..., :MLA_NOPE], w_uq_h[..., MLA_NOPE:]).reshape(MLA_Q_RANK, HEAD_PAD)
    w_ukv_h = w_ukv.reshape(MLA_KV_RANK, MLA_HEADS, MLA_NOPE + MLA_V)
    w_uk_l = _head_layout(w_ukv_h[..., :MLA_NOPE],
                          jnp.zeros((MLA_KV_RANK, MLA_HEADS, MLA_ROPE), F32)).reshape(MLA_KV_RANK, HEAD_PAD)
    w_uv = w_ukv_h[..., MLA_NOPE:].reshape(MLA_KV_RANK, MLA_WIDTH)
    gq_l = _head_layout(qk_norm_q[:MLA_NOPE], qk_norm_q[MLA_NOPE:])
    gk_l = _head_layout(qk_norm_k[:MLA_NOPE], qk_norm_k[MLA_NOPE:])

    inv_freq = ROPE_THETA ** (-jnp.arange(0, MLA_ROPE, 2, dtype=F32) / MLA_ROPE)
    z32 = jnp.zeros((32,), F32)
    tab = jnp.zeros((SUBLANES, LANES), F32)
    tab = tab.at[ROW_FREQ16].set(jnp.tile(inv_freq, LANES // HALF_ROPE))
    tab = tab.at[ROW_MASK96].set(jnp.concatenate([jnp.ones((96,), F32), z32]))
    tab = tab.at[ROW_KRMASK].set(jnp.concatenate([z32, jnp.ones((32,), F32), z32, jnp.ones((32,), F32)]))
    tab = tab.at[ROW_GQ].set(gq_l).at[ROW_GQ_SW].set(jnp.roll(gq_l, LANES // 2))
    tab = tab.at[ROW_GK].set(gk_l).at[ROW_GK_SW].set(jnp.roll(gk_l, LANES // 2))

    n_blocks = D_IN_PAD // W_IN_BLOCK
    w_in_l = w_in_l.reshape(D_MODEL, n_blocks, W_IN_BLOCK).transpose(1, 0, 2)

    n = t // tm
    tile = lambda w: pl.BlockSpec((tm, w), lambda i: (jnp.minimum(i, n - 1), 0))
    prev = lambda w: pl.BlockSpec((tm, w), lambda i: (jnp.maximum(i - 1, 0), 0))
    out_widths = (GLA_QK, GLA_QK, GLA_WIDTH, GLA_QK, GLA_WIDTH, HEAD_PAD, HEAD_PAD, MLA_WIDTH)
    out_dtypes = (BF16, BF16, BF16, F32, BF16, BF16, BF16, BF16)
    out_specs = [tile(w) for w in out_widths]
    out_specs[5], out_specs[6] = prev(HEAD_PAD), prev(HEAD_PAD)
    return pl.pallas_call(
        _proj_kernel,
        grid=(n + 1,),
        in_specs=[
            tile(D_MODEL), prev(1),
            _const_spec((1, D_MODEL)), _const_spec((n_blocks, D_MODEL, W_IN_BLOCK)),
            _const_spec((LANES, GLA_QK)), _const_spec((1, GLA_QK)),
            _const_spec((1, MLA_Q_RANK)), _const_spec((MLA_Q_RANK, HEAD_PAD)),
            _const_spec((1, MLA_KV_RANK)), _const_spec((MLA_KV_RANK, HEAD_PAD)),
            _const_spec((MLA_KV_RANK, MLA_WIDTH)),
            _const_spec((8, LANES)),
        ],
        out_specs=out_specs,
        out_shape=[jax.ShapeDtypeStruct((t, w), d) for w, d in zip(out_widths, out_dtypes)],
        scratch_shapes=[pltpu.VMEM((tm, HEAD_PAD), F32), pltpu.VMEM((tm, HEAD_PAD), F32),
                        pltpu.VMEM((tm, LANES), F32)],
        compiler_params=pltpu.CompilerParams(
            dimension_semantics=("arbitrary",), vmem_limit_bytes=VMEM_LIMIT_BYTES),
        name="proj",
    )(x2, pos2, attn_norm.reshape(1, D_MODEL), w_in_l, w2_l, gate_b.reshape(1, GLA_QK),
      q_norm.reshape(1, MLA_Q_RANK), w_uq_l.astype(BF16), kv_norm.reshape(1, MLA_KV_RANK),
      w_uk_l.astype(BF16), w_uv.astype(BF16), tab)


def _cumsum(sizes):
    total = 0
    for s in sizes:
        total += s
        yield total


def _gla_kernel(q_ref, k_ref, v_ref, la_ref, gr_ref, gon_ref, o_ref, st_ref, kf_ref, b_ref):
    nseq, rows_per_step = q_ref.shape[0], q_ref.shape[1]

    @pl.when(pl.program_id(1) == 0)
    def _():
        st_ref[...] = jnp.zeros_like(st_ref)

    row = lax.broadcasted_iota(jnp.int32, (CHUNK, CHUNK), 0)
    col = lax.broadcasted_iota(jnp.int32, (CHUNK, CHUNK), 1)
    causal = col <= row
    lane = lax.broadcasted_iota(jnp.int32, (1, LANES), 1)
    lane_q = lax.broadcasted_iota(jnp.int32, (1, GLA_QK), 1)
    head_mask = [(lane // GLA_DK) == e for e in range(2)]
    gon = gon_ref[...]
    nt = (((1,), (1,)), ((), ()))
    tn = (((0,), (0,)), ((), ()))

    def exact_scores(q, k, b):
        kf_ref[...] = k
        b_ref[...] = b

        def key_row(s, sc):
            k_s = kf_ref[pl.ds(s, 1), :]
            b_s = b_ref[pl.ds(s, 1), :]
            prod = q * k_s * jnp.exp(jnp.minimum(b - b_s, 0.0))
            out = []
            for h in range(GLA_HEADS):
                hm = (lane_q // GLA_DK) == h
                col_h = jnp.sum(jnp.where(hm, prod, 0.0), axis=-1, keepdims=True)
                out.append(jnp.where(col == s, col_h, sc[h]))
            return tuple(out)

        zero = jnp.zeros((CHUNK, CHUNK), F32)
        return lax.fori_loop(0, CHUNK, key_row, (zero,) * GLA_HEADS)

    pair = lambda h: slice((h // 2) * LANES, (h // 2 + 1) * LANES)
    vsl = lambda h: slice(h * GLA_DV, (h + 1) * GLA_DV)
    heads = range(GLA_HEADS)

    def row_cumsum(x):
        row_id = lax.broadcasted_iota(jnp.int32, x.shape, 0)
        step = 1
        while step < CHUNK:
            if step < SUBLANES:
                moved = jnp.where(row_id >= step, pltpu.roll(x, step, 0), 0.0)
            else:
                moved = jnp.concatenate([jnp.zeros((step, x.shape[1]), x.dtype), x[:-step]], axis=0)
            x = x + moved
            step *= 2
        return x

    def chunk(seqs, c, factorised):
        rows = pl.ds(pl.multiple_of(c * CHUNK, CHUNK), CHUNK)
        q, k, b, a_h, kt, kd, eb_last, scores, o_inter = {}, {}, {}, {}, {}, {}, {}, {}, {}
        for n in seqs:
            b[n] = row_cumsum(la_ref[n, rows, :])
            b_last = b[n][CHUNK - 1:CHUNK, :]
            q[n] = q_ref[n, rows, :].astype(F32)
            k[n] = k_ref[n, rows, :].astype(F32)
            qt = q[n] * jnp.exp(b[n])
            eb_last[n] = jnp.exp(b_last)
            if factorised:
                k_grown = k[n] * jnp.exp(-b[n])
                kt[n] = k_grown.astype(BF16)
                kd[n] = k_grown * eb_last[n]
            else:
                kd[n] = k[n] * jnp.exp(b_last - b[n])
            a_h[n] = [jnp.where(head_mask[h % 2], qt[:, pair(h)], 0.0).astype(BF16) for h in heads]
        v, st, o = {}, {}, {}
        for n in seqs:
            for h in heads:
                st[n, h] = st_ref[n, h]
                if factorised:
                    both = lax.dot_general(
                        a_h[n][h], jnp.concatenate([st[n, h].astype(BF16), kt[n][:, pair(h)]], axis=0), nt,
                        preferred_element_type=F32)
                    o_inter[n, h], scores[n, h] = both[:, :GLA_DV], both[:, GLA_DV:]
                else:
                    o_inter[n, h] = lax.dot_general(a_h[n][h], st[n, h].astype(BF16), nt,
                                                    preferred_element_type=F32)
            if not factorised:
                for h, sc in enumerate(exact_scores(q[n], k[n], b[n])):
                    scores[n, h] = sc
        for n in seqs:
            for h in heads:
                sc = jnp.where(causal, scores[n, h], 0.0).astype(BF16)
                v[n, h] = v_ref[n, rows, vsl(h)]
                o[n, h] = o_inter[n, h] + jnp.dot(sc, v[n, h], preferred_element_type=F32)
        for n in seqs:
            for h in heads:
                kd_h = jnp.where(head_mask[h % 2], kd[n][:, pair(h)], 0.0).astype(BF16)
                upd = lax.dot_general(v[n, h], kd_h, tn, preferred_element_type=F32)
                st_ref[n, h] = st[n, h] * eb_last[n][:, pair(h)] + upd
        for n in seqs:
            for h in heads:
                on = o[n, h] * _rms_scale(o[n, h], GLA_DV) * gon
                o_ref[n, rows, vsl(h)] = (on * gr_ref[n, rows, vsl(h)].astype(F32)).astype(BF16)

    def factorised_chunks():
        def body(c, carry):
            chunk(range(nseq), c, True)
            return carry
        lax.fori_loop(0, rows_per_step // CHUNK, body, 0)

    def exact_chunks():
        def body(c, carry):
            for n in range(nseq):
                chunk([n], c, False)
            return carry
        lax.fori_loop(0, rows_per_step // CHUNK, body, 0)

    factorisable = jnp.min(la_ref[...]) >= GLA_FAST_MIN_LOG_DECAY / CHUNK
    lax.cond(factorisable, factorised_chunks, exact_chunks)


def _gla(qg, kg, vg, la, gr, out_norm):
    b, s, _ = qg.shape
    nseq = max(n for n in GLA_SEQS_PER_STEP if b % n == 0)
    rows = min(GLA_ROWS_PER_STEP, s)
    assert s % rows == 0 and rows % CHUNK == 0
    blk = lambda w: pl.BlockSpec((nseq, rows, w), lambda i, j: (i, j, 0))
    return pl.pallas_call(
        _gla_kernel,
        grid=(b // nseq, s // rows),
        in_specs=[blk(GLA_QK), blk(GLA_QK), blk(GLA_WIDTH), blk(GLA_QK), blk(GLA_WIDTH),
                  _const_spec((1, GLA_DV))],
        out_specs=blk(GLA_WIDTH),
        out_shape=jax.ShapeDtypeStruct((b, s, GLA_WIDTH), BF16),
        scratch_shapes=[pltpu.VMEM((nseq, GLA_HEADS, GLA_DV, LANES), F32),
                        pltpu.VMEM((CHUNK, GLA_QK), F32),
                        pltpu.VMEM((CHUNK, GLA_QK), F32)],
        compiler_params=pltpu.CompilerParams(
            dimension_semantics=("arbitrary", "arbitrary"), vmem_limit_bytes=VMEM_LIMIT_BYTES),
        name="gla",
    )(qg, kg, vg, la, gr, out_norm.reshape(1, GLA_DV))


def _mla_tiles(n_q):
    assert TQ == 2 * TK
    first = [(a, 2 * a, 0) for a in range(n_q)]
    full = [(a, j, TQ // CHUNK) for a in range(n_q) for j in range(2 * a)]
    second = [(a, 2 * a + 1, -(TK // CHUNK)) for a in range(n_q)]
    return tuple(zip(*(first + full + second)))


def _mla_kernel(ti_ref, tj_ref, toff_ref, bound_ref, q_ref, k_ref, v_ref, o_ref,
                s_ref, m_ref, acc_ref, v1_ref):
    seq = q_ref.shape[1]
    n_q = seq // TQ
    n_tiles = s_ref.shape[1]
    nt = (((1,), (1,)), ((), ()))
    heads = range(2)
    lane = lax.broadcasted_iota(jnp.int32, (1, LANES), 1)
    own = [(lane // MLA_V) == e for e in heads]
    unroll = lambda count: max(u for u in range(1, MLA_UNROLL + 1) if count % u == 0)

    def q_rows(i):
        return pl.ds(pl.multiple_of(i * TQ, TQ), TQ)

    def k_rows(j):
        return pl.ds(pl.multiple_of(j * TK, TK), TK)

    def masked_logits(e, t):
        hsl = slice(e * LANES, (e + 1) * LANES)
        s = lax.dot_general(q_ref[0, q_rows(ti_ref[t]), hsl], k_ref[0, k_rows(tj_ref[t]), hsl], nt,
                            preferred_element_type=F32)
        row_chunk = lax.broadcasted_iota(jnp.int32, (TQ, TK), 0) // CHUNK
        col_chunk = lax.broadcasted_iota(jnp.int32, (TQ, TK), 1) // CHUNK
        return jnp.where(col_chunk <= row_chunk + toff_ref[t], s, MASKED_LOGIT)

    def lane_fold_max(a):
        out = a[:, :LANES]
        for c in range(1, TK // LANES):
            out = jnp.maximum(out, a[:, c * LANES:(c + 1) * LANES])
        return out

    v = v_ref[0]
    for e in heads:
        v1_ref[e] = jnp.where(own[e], v, 1.0).astype(BF16)

    def exp_shifted(s, shift):
        return jnp.concatenate(
            [jnp.exp2(s[:, c * LANES:(c + 1) * LANES] - shift) for c in range(TK // LANES)],
            axis=1).astype(BF16)

    def accumulate(e, t, s, shift):
        acc_ref[e, ti_ref[t]] += jnp.dot(exp_shifted(s, shift), v1_ref[e, k_rows(tj_ref[t]), :],
                                         preferred_element_type=F32)

    def clear(i, carry):
        for e in heads:
            acc_ref[e, i] = jnp.zeros((TQ, LANES), F32)
        return carry

    def finish(i, carry):
        acc0, acc1 = acc_ref[0, i], acc_ref[1, i]
        num = jnp.where(own[0], acc0, acc1)
        den = pltpu.roll(jnp.where(own[0], acc1, acc0), LANES // 2, 1)
        o_ref[0, q_rows(i), :] = (num / den).astype(BF16)
        return carry

    def single_pass():
        lax.fori_loop(0, n_q, clear, 0)
        shift = bound_ref[0].astype(F32)

        def tile(t, carry):
            for e in heads:
                accumulate(e, t, masked_logits(e, t), shift)
            return carry

        lax.fori_loop(0, n_tiles - n_q, tile, 0, unroll=unroll(n_tiles - n_q))

        def lower_half_tile(t, carry):
            i = ti_ref[t]
            lower = pl.ds(pl.multiple_of(i * TQ + TK, TK), TK)
            row_chunk = lax.broadcasted_iota(jnp.int32, (TK, TK), 0) // CHUNK
            col_chunk = lax.broadcasted_iota(jnp.int32, (TK, TK), 1) // CHUNK
            for e in heads:
                hsl = slice(e * LANES, (e + 1) * LANES)
                s = lax.dot_general(q_ref[0, lower, hsl], k_ref[0, k_rows(tj_ref[t]), hsl], nt,
                                    preferred_element_type=F32)
                p = exp_shifted(jnp.where(col_chunk <= row_chunk, s, MASKED_LOGIT), shift)
                acc_ref[e, i, TK:, :] += jnp.dot(p, v1_ref[e, k_rows(tj_ref[t]), :],
                                                 preferred_element_type=F32)
            return finish(i, carry)

        lax.fori_loop(n_tiles - n_q, n_tiles, lower_half_tile, 0, unroll=unroll(n_q))

    def two_pass():
        def first_tile(t, carry):
            for e in heads:
                s = masked_logits(e, t)
                s_ref[e, t] = s
                m_ref[e, t] = lane_fold_max(s)
            return clear(t, carry)

        lax.fori_loop(0, n_q, first_tile, 0, unroll=unroll(n_q))

        def pass1(t, carry):
            i = ti_ref[t]
            for e in heads:
                s = masked_logits(e, t)
                s_ref[e, t] = s
                m_ref[e, i] = jnp.maximum(m_ref[e, i], lane_fold_max(s))
            return carry

        lax.fori_loop(n_q, n_tiles, pass1, 0, unroll=unroll(n_tiles - n_q))

        def row_max(i, carry):
            for e in heads:
                m_ref[e, i] = jnp.broadcast_to(jnp.max(m_ref[e, i], axis=-1, keepdims=True), (TQ, LANES))
            return carry

        lax.fori_loop(0, n_q, row_max, 0)

        def pass2(t, carry):
            for e in heads:
                accumulate(e, t, s_ref[e, t], m_ref[e, ti_ref[t]])
            return carry

        lax.fori_loop(0, n_tiles, pass2, 0, unroll=unroll(n_tiles))
        lax.fori_loop(0, n_q, finish, 0)

    lax.cond(bound_ref[0] <= MLA_MAX_SHIFT, single_pass, two_pass)


def _mla(q, k, v, gain_q, gain_k):
    b, s, _ = q.shape
    assert s % TQ == 0 and TQ % TK == 0
    n_q = s // TQ
    ti, tj, toff = _mla_tiles(n_q)
    bound = jnp.ceil(1.02 * LOG2E * math.sqrt(MLA_QK) * jnp.max(jnp.abs(gain_q)) * jnp.max(jnp.abs(gain_k)))
    bound = jnp.minimum(bound, 2.0 * MLA_MAX_SHIFT).astype(jnp.int32).reshape(1)
    blk = lambda w: pl.BlockSpec((1, s, w), lambda i, p, *_: (i, 0, p))
    return pl.pallas_call(
        _mla_kernel,
        grid_spec=pltpu.PrefetchScalarGridSpec(
            num_scalar_prefetch=4,
            grid=(b, MLA_HEADS // 2),
            in_specs=[blk(2 * LANES), blk(2 * LANES), blk(LANES)],
            out_specs=blk(LANES),
            scratch_shapes=[pltpu.VMEM((2, len(ti), TQ, TK), F32),
                            pltpu.VMEM((2, n_q, TQ, LANES), F32),
                            pltpu.VMEM((2, n_q, TQ, LANES), F32),
                            pltpu.VMEM((2, s, LANES), BF16)]),
        out_shape=jax.ShapeDtypeStruct((b, s, MLA_WIDTH), BF16),
        compiler_params=pltpu.CompilerParams(
            dimension_semantics=("arbitrary", "arbitrary"),
            vmem_limit_bytes=VMEM_LIMIT_BYTES),
        name="mla",
    )(jnp.asarray(ti, jnp.int32), jnp.asarray(tj, jnp.int32), jnp.asarray(toff, jnp.int32), bound,
      q, k, v)


def _tail_kernel(x_ref, og_ref, om_ref, p_ref, w_out_ref, g_mlp_ref, w_up_ref, w_down_ref,
                 g_ple_ref, w_gate_ref, b_gate_ref, w_pp_ref, o_ref):
    h = (x_ref[...]
         + jnp.dot(og_ref[...], w_out_ref[:GLA_WIDTH, :], preferred_element_type=F32)
         + jnp.dot(om_ref[...], w_out_ref[GLA_WIDTH:, :], preferred_element_type=F32))
    m = (h * _rms_scale(h, D_MODEL) * g_mlp_ref[...]).astype(BF16)
    mlp = None
    for c in range(D_FF // FF_CHUNK):
        cols = slice(c * FF_CHUNK, (c + 1) * FF_CHUNK)
        u = jnp.maximum(jnp.dot(m, w_up_ref[:, cols], preferred_element_type=F32), 0.0)
        d = jnp.dot((u * u).astype(BF16), w_down_ref[cols, :], preferred_element_type=F32)
        mlp = d if mlp is None else mlp + d
    h = h + mlp
    g = (h * _rms_scale(h, D_MODEL) * g_ple_ref[...]).astype(BF16)
    gate = jax.nn.sigmoid(jnp.dot(g, w_gate_ref[...], preferred_element_type=F32) + b_gate_ref[...])
    pp = jnp.dot(p_ref[...].astype(BF16), w_pp_ref[...], preferred_element_type=F32)
    o_ref[...] = h + pp * gate


def _tail(x2, og, om, p2, w_out, mlp_norm, w_up, w_down, ple_norm, w_gate, b_gate, w_pp):
    t = x2.shape[0]
    tm = TM_TAIL
    assert t % tm == 0
    row = lambda w: pl.BlockSpec((tm, w), lambda i: (i, 0))
    return pl.pallas_call(
        _tail_kernel,
        grid=(t // tm,),
        in_specs=[row(D_MODEL), row(GLA_WIDTH), row(MLA_WIDTH), row(PLE_DIM),
                  _const_spec((D_MODEL, D_MODEL)), _const_spec((1, D_MODEL)),
                  _const_spec((D_MODEL, D_FF)), _const_spec((D_FF, D_MODEL)),
                  _const_spec((1, D_MODEL)), _const_spec((D_MODEL, D_MODEL)),
                  _const_spec((1, D_MODEL)), _const_spec((PLE_DIM, D_MODEL))],
        out_specs=row(D_MODEL),
        out_shape=jax.ShapeDtypeStruct((t, D_MODEL), F32),
        compiler_params=pltpu.CompilerParams(
            dimension_semantics=("arbitrary",), vmem_limit_bytes=VMEM_LIMIT_BYTES),
        name="tail",
    )(x2, og, om, p2, w_out.astype(BF16), mlp_norm.reshape(1, D_MODEL), w_up.astype(BF16),
      w_down.astype(BF16), ple_norm.reshape(1, D_MODEL), w_gate.astype(BF16),
      b_gate.reshape(1, D_MODEL), w_pp.astype(BF16))


def kernel(x, p, positions, attn_norm, w_in, gla_gate_w2, gla_gate_b, gla_out_norm, mla_q_norm,
           mla_w_uq, mla_kv_norm, mla_w_ukv, qk_norm_q, qk_norm_k, w_out, mlp_norm, w_mlp_up,
           w_mlp_down, ple_norm, w_ple_gate, b_ple_gate, w_ple_proj):
    b, s, d = x.shape
    depth = w_in.shape[0]
    t = b * s
    pos2 = positions.reshape(t, 1)
    h = x.reshape(t, d)
    for i in range(depth):
        qg, kg, vg, la, gr, q, k, v = _projection(
            h, pos2, attn_norm[i], w_in[i], gla_gate_w2[i], gla_gate_b[i], mla_q_norm[i],
            mla_w_uq[i], mla_kv_norm[i], mla_w_ukv[i], qk_norm_q[i], qk_norm_k[i])
        seq = lambda a: a.reshape(b, s, a.shape[-1])
        og = _gla(seq(qg), seq(kg), seq(vg), seq(la), seq(gr), gla_out_norm[i])
        om = _mla(seq(q), seq(k), seq(v), qk_norm_q[i], qk_norm_k[i])
        h = _tail(h, og.reshape(t, GLA_WIDTH), om.reshape(t, MLA_WIDTH), p[i].reshape(t, PLE_DIM),
                  w_out[i], mlp_norm[i], w_mlp_up[i], w_mlp_down[i], ple_norm[i], w_ple_gate[i],
                  b_ple_gate[i], w_ple_proj[i])
    return h.reshape(b, s, d)
```

```python
import math

import jax
import jax.numpy as jnp
from jax import lax
from jax.experimental import pallas as pl
from jax.experimental.pallas import tpu as pltpu

F32 = jnp.float32
BF16 = jnp.bfloat16

D_MODEL = 1024
CHUNK = 64
PLE_DIM = 256
D_FF = 4 * D_MODEL
EPS = 1e-6
GLA_HEADS = 4
GLA_DK = 64
GLA_DV = 128
GLA_GATE_RANK = 16
GLA_TAU = 16.0
GLA_QK = GLA_HEADS * GLA_DK
GLA_WIDTH = GLA_HEADS * GLA_DV
MLA_HEADS = 8
MLA_NOPE = 64
MLA_ROPE = 32
MLA_V = 64
MLA_QK = MLA_NOPE + MLA_ROPE
MLA_Q_RANK = 256
MLA_KV_RANK = 128
MLA_WIDTH = MLA_HEADS * MLA_V
ROPE_THETA = 10000.0
LOG2E = math.log2(math.e)
IN_SPLITS = (GLA_QK, GLA_QK, GLA_WIDTH, GLA_GATE_RANK, GLA_WIDTH, MLA_Q_RANK, MLA_KV_RANK, MLA_ROPE)

LANES = 128
SUBLANES = 8
HALF_ROPE = MLA_ROPE // 2
HALF_NOPE = MLA_NOPE // 2
HEAD_PAD = MLA_HEADS * LANES
D_IN_PAD = 2048
W_IN_BLOCK = 256
VMEM_LIMIT_BYTES = 56 * 1024 * 1024

TM_PROJ = 512
TM_TAIL = 512
FF_CHUNK = 1024
TQ = 512
TK = 256
MLA_UNROLL = 8
ROPE_ROWS = 64
GLA_SEQS_PER_STEP = (8, 4, 2, 1)
GLA_ROWS_PER_STEP = 256
GLA_FAST_MIN_LOG_DECAY = -80.0
MASKED_LOGIT = -1e30
MLA_MAX_SHIFT = 40

ROW_FREQ16, ROW_MASK96, ROW_KRMASK, ROW_GQ, ROW_GQ_SW, ROW_GK, ROW_GK_SW = range(7)


def _const_spec(shape):
    return pl.BlockSpec(shape, lambda *_: (0,) * len(shape), pipeline_mode=pl.Buffered(1))


def _rms_scale(v, n):
    return lax.rsqrt(jnp.sum(v * v, axis=-1, keepdims=True) * (1.0 / n) + EPS)


def _proj_kernel(x_ref, pos_ref, g_attn_ref, w_in_ref, w2_ref, b2_ref, gqn_ref, w_uq_ref,
                 gkvn_ref, w_uk_ref, w_uv_ref, tab_ref,
                 qg_ref, kg_ref, vg_ref, la_ref, gr_ref, q_ref, k_ref, v_ref,
                 qs_ref, ks_ref, slot_ref):
    @pl.when(pl.program_id(0) == 0)
    def _():
        qs_ref[...] = jnp.zeros_like(qs_ref)
        ks_ref[...] = jnp.zeros_like(ks_ref)
        slot_ref[...] = jnp.zeros_like(slot_ref)

    x = x_ref[...]
    xn = (x * _rms_scale(x, D_MODEL) * g_attn_ref[...]).astype(BF16)

    tab = lambda r: tab_ref[r:r + 1, :]
    mask96, krmask = tab(ROW_MASK96), tab(ROW_KRMASK)
    gq, gq_sw = tab(ROW_GQ) * LOG2E, tab(ROW_GQ_SW) * LOG2E
    gk, gk_sw = tab(ROW_GK) * math.sqrt(MLA_QK), tab(ROW_GK_SW) * math.sqrt(MLA_QK)
    half = LANES // 2

    lane = lax.broadcasted_iota(jnp.int32, (1, LANES), 1)
    groups = ROPE_ROWS // SUBLANES
    assert groups * HALF_ROPE == LANES
    x1_lanes = (lane >= HALF_NOPE) & (lane < HALF_NOPE + HALF_ROPE)
    x2_lanes = (lane >= HALF_NOPE + HALF_ROPE) & (lane < HALF_NOPE + MLA_ROPE)

    def rope_tables(rows):
        pos = jnp.broadcast_to(pos_ref[rows, :].astype(F32), (ROPE_ROWS, LANES))
        packed = jnp.zeros((SUBLANES, LANES), F32)
        for a in range(groups):
            packed = jnp.where(lane // HALF_ROPE == a, pos[a * SUBLANES:(a + 1) * SUBLANES], packed)
        ang = packed * tab(ROW_FREQ16)
        cos_p, sin_p = jnp.cos(ang), jnp.sin(ang)
        cos, sin = [], []
        for a in range(groups):
            s1 = (HALF_NOPE - HALF_ROPE * a) % LANES
            s2 = (HALF_NOPE + HALF_ROPE - HALF_ROPE * a) % LANES
            c1, c2 = pltpu.roll(cos_p, s1, 1), pltpu.roll(cos_p, s2, 1)
            n1, n2 = pltpu.roll(sin_p, s1, 1), pltpu.roll(sin_p, s2, 1)
            cos.append(jnp.where(x1_lanes, c1, jnp.where(x2_lanes, c2, mask96)))
            sin.append(jnp.where(x1_lanes, -n1, jnp.where(x2_lanes, n2, 0.0)))
        return jnp.concatenate(cos, axis=0), jnp.concatenate(sin, axis=0)

    sum96 = (lax.broadcasted_iota(jnp.int32, (LANES, LANES), 0) < MLA_QK).astype(BF16)

    def head_rows(rb):
        rows = pl.ds(rb * ROPE_ROWS, ROPE_ROWS)
        cos, sin = rope_tables(rows)
        kr = slot_ref[rows, :] * krmask
        kr_rot = pltpu.roll(kr, half, 1) * (sin * gk_sw)
        cos_q, sin_q, cos_k = cos * gq, sin * gq_sw, cos * gk
        slabs = []
        for h in range(MLA_HEADS):
            sl = slice(h * LANES, (h + 1) * LANES)
            slabs += [qs_ref[rows, sl], ks_ref[rows, sl] + kr]
        squares = jnp.concatenate([(v * v).astype(BF16) for v in slabs], axis=0)
        rms = lax.rsqrt(jnp.dot(squares, sum96, preferred_element_type=F32) + MLA_QK * EPS)
        for h in range(MLA_HEADS):
            sl = slice(h * LANES, (h + 1) * LANES)
            qh, kh = slabs[2 * h], slabs[2 * h + 1]
            rq = rms[(2 * h) * ROPE_ROWS:(2 * h + 1) * ROPE_ROWS]
            rk = rms[(2 * h + 1) * ROPE_ROWS:(2 * h + 2) * ROPE_ROWS]
            q_ref[rows, sl] = ((qh * cos_q + pltpu.roll(qh, half, 1) * sin_q) * rq).astype(BF16)
            k_ref[rows, sl] = ((kh * cos_k + kr_rot) * rk).astype(BF16)

    n_blocks = D_IN_PAD // W_IN_BLOCK
    assert x_ref.shape[0] // ROPE_ROWS == n_blocks == 8
    cq = ckv = slot = None
    for r in range(n_blocks):
        z = jnp.dot(xn, w_in_ref[:, r * W_IN_BLOCK:(r + 1) * W_IN_BLOCK], preferred_element_type=F32)
        head_rows(r)
        half_cols = slice((r % 2) * W_IN_BLOCK, (r % 2 + 1) * W_IN_BLOCK)
        if r == 0:
            qg_ref[...] = (z * (GLA_DK ** -0.5)).astype(BF16)
        elif r == 1:
            kg_ref[...] = z.astype(BF16)
        elif r in (2, 3):
            vg_ref[:, half_cols] = z.astype(BF16)
        elif r in (4, 5):
            gr_ref[:, half_cols] = (z * jax.nn.sigmoid(z)).astype(BF16)
        elif r == 6:
            cq = z
        else:
            ckv, slot = z[:, :MLA_KV_RANK], z[:, MLA_KV_RANK:]
    slot_ref[...] = slot

    xg = jnp.dot(slot.astype(BF16), w2_ref[...], preferred_element_type=F32) + b2_ref[...]
    la_ref[...] = (jnp.minimum(xg, 0.0) - jnp.log(1.0 + jnp.exp(-jnp.abs(xg)))) * (1.0 / GLA_TAU)

    cqn = (cq * _rms_scale(cq, MLA_Q_RANK) * gqn_ref[...]).astype(BF16)
    qs_ref[...] = jnp.dot(cqn, w_uq_ref[...], preferred_element_type=F32)
    ckvn = (ckv * _rms_scale(ckv, MLA_KV_RANK) * gkvn_ref[...]).astype(BF16)
    ks_ref[...] = jnp.dot(ckvn, w_uk_ref[...], preferred_element_type=F32)
    v_ref[...] = jnp.dot(ckvn, w_uv_ref[...], preferred_element_type=F32).astype(BF16)


def _head_layout(nope, rope):
    x1, x2 = rope[..., :HALF_ROPE], rope[..., HALF_ROPE:]
    return jnp.concatenate([nope[..., :HALF_NOPE], x1, x2, nope[..., HALF_NOPE:], x2, x1], axis=-1)


def _projection(x2, pos2, attn_norm, w_in, gate_w2, gate_b, q_norm, w_uq, kv_norm, w_ukv,
                qk_norm_q, qk_norm_k):
    t = x2.shape[0]
    tm = TM_PROJ
    assert t % tm == 0

    gq, gk, gv, g_low, g_r, c_q, c_kv, k_r = jnp.split(w_in, list(_cumsum(IN_SPLITS))[:-1], axis=1)
    zeros = lambda n: jnp.zeros((D_MODEL, n), w_in.dtype)
    kr_sw = jnp.concatenate([k_r[:, HALF_ROPE:], k_r[:, :HALF_ROPE]], axis=1)
    slot = jnp.concatenate([g_low, zeros(HALF_NOPE - GLA_GATE_RANK), k_r, zeros(HALF_NOPE), kr_sw], axis=1)
    w_in_l = jnp.concatenate([gq, gk, gv, g_r, c_q, c_kv, slot], axis=1).astype(BF16)
    assert w_in_l.shape == (D_MODEL, D_IN_PAD)

    w2_l = jnp.zeros((LANES, GLA_QK), F32).at[:GLA_GATE_RANK].set(gate_w2).astype(BF16)

    w_uq_h = w_uq.reshape(MLA_Q_RANK, MLA_HEADS, MLA_QK)
    w_uq_l = _head_layout(w_uq_h[..., :MLA_NOPE], w_uq_h[..., MLA_NOPE:]).reshape(MLA_Q_RANK, HEAD_PAD)
    w_ukv_h = w_ukv.reshape(MLA_KV_RANK, MLA_HEADS, MLA_NOPE + MLA_V)
    w_uk_l = _head_layout(w_ukv_h[..., :MLA_NOPE],
                          jnp.zeros((MLA_KV_RANK, MLA_HEADS, MLA_ROPE), F32)).reshape(MLA_KV_RANK, HEAD_PAD)
    w_uv = w_ukv_h[..., MLA_NOPE:].reshape(MLA_KV_RANK, MLA_WIDTH)
    gq_l = _head_layout(qk_norm_q[:MLA_NOPE], qk_norm_q[MLA_NOPE:])
    gk_l = _head_layout(qk_norm_k[:MLA_NOPE], qk_norm_k[MLA_NOPE:])

    inv_freq = ROPE_THETA ** (-jnp.arange(0, MLA_ROPE, 2, dtype=F32) / MLA_ROPE)
    z32, o32 = jnp.zeros((HALF_NOPE,), F32), jnp.ones((MLA_ROPE,), F32)
    tab = jnp.zeros((SUBLANES, LANES), F32)
    tab = tab.at[ROW_FREQ16].set(jnp.tile(inv_freq, LANES // HALF_ROPE))
    tab = tab.at[ROW_MASK96].set(jnp.concatenate([jnp.ones((MLA_QK,), F32), z32]))
    tab = tab.at[ROW_KRMASK].set(jnp.concatenate([z32, o32, z32, o32]))
    tab = tab.at[ROW_GQ].set(gq_l).at[ROW_GQ_SW].set(jnp.roll(gq_l, LANES // 2))
    tab = tab.at[ROW_GK].set(gk_l).at[ROW_GK_SW].set(jnp.roll(gk_l, LANES // 2))

    n = t // tm
    tile = lambda w: pl.BlockSpec((tm, w), lambda i: (jnp.minimum(i, n - 1), 0))
    prev = lambda w: pl.BlockSpec((tm, w), lambda i: (jnp.maximum(i - 1, 0), 0))
    out_widths = (GLA_QK, GLA_QK, GLA_WIDTH, GLA_QK, GLA_WIDTH, HEAD_PAD, HEAD_PAD, MLA_WIDTH)
    out_dtypes = (BF16, BF16, BF16, F32, BF16, BF16, BF16, BF16)
    out_specs = [tile(w) for w in out_widths]
    out_specs[5], out_specs[6] = prev(HEAD_PAD), prev(HEAD_PAD)
    return pl.pallas_call(
        _proj_kernel,
        grid=(n + 1,),
        in_specs=[
            tile(D_MODEL), prev(1),
            _const_spec((1, D_MODEL)), _const_spec((D_MODEL, D_IN_PAD)),
            _const_spec((LANES, GLA_QK)), _const_spec((1, GLA_QK)),
            _const_spec((1, MLA_Q_RANK)), _const_spec((MLA_Q_RANK, HEAD_PAD)),
            _const_spec((1, MLA_KV_RANK)), _const_spec((MLA_KV_RANK, HEAD_PAD)),
            _const_spec((MLA_KV_RANK, MLA_WIDTH)),
            _const_spec((8, LANES)),
        ],
        out_specs=out_specs,
        out_shape=[jax.ShapeDtypeStruct((t, w), d) for w, d in zip(out_widths, out_dtypes)],
        scratch_shapes=[pltpu.VMEM((tm, HEAD_PAD), F32), pltpu.VMEM((tm, HEAD_PAD), F32),
                        pltpu.VMEM((tm, LANES), F32)],
        compiler_params=pltpu.CompilerParams(
            dimension_semantics=("arbitrary",), vmem_limit_bytes=VMEM_LIMIT_BYTES),
        name="proj",
    )(x2, pos2, attn_norm.reshape(1, D_MODEL), w_in_l, w2_l, gate_b.reshape(1, GLA_QK),
      q_norm.reshape(1, MLA_Q_RANK), w_uq_l.astype(BF16), kv_norm.reshape(1, MLA_KV_RANK),
      w_uk_l.astype(BF16), w_uv.astype(BF16), tab)


def _cumsum(sizes):
    total = 0
    for s in sizes:
        total += s
        yield total


def _gla_kernel(q_ref, k_ref, v_ref, la_ref, gr_ref, gon_ref, o_ref, st_ref, kf_ref, b_ref):
    nseq, rows_per_step = q_ref.shape[0], q_ref.shape[1]

    @pl.when(pl.program_id(1) == 0)
    def _():
        st_ref[...] = jnp.zeros_like(st_ref)

    row = lax.broadcasted_iota(jnp.int32, (CHUNK, CHUNK), 0)
    col = lax.broadcasted_iota(jnp.int32, (CHUNK, CHUNK), 1)
    causal = col <= row
    lane = lax.broadcasted_iota(jnp.int32, (1, LANES), 1)
    lane_q = lax.broadcasted_iota(jnp.int32, (1, GLA_QK), 1)
    head_mask = [(lane // GLA_DK) == e for e in range(2)]
    gon = gon_ref[...]
    nt = (((1,), (1,)), ((), ()))
    tn = (((0,), (0,)), ((), ()))

    def exact_scores(q, k, b):
        kf_ref[...] = k
        b_ref[...] = b

        def key_row(s, sc):
            k_s = kf_ref[pl.ds(s, 1), :]
            b_s = b_ref[pl.ds(s, 1), :]
            prod = q * k_s * jnp.exp(jnp.minimum(b - b_s, 0.0))
            out = []
            for h in range(GLA_HEADS):
                hm = (lane_q // GLA_DK) == h
                col_h = jnp.sum(jnp.where(hm, prod, 0.0), axis=-1, keepdims=True)
                out.append(jnp.where(col == s, col_h, sc[h]))
            return tuple(out)

        zero = jnp.zeros((CHUNK, CHUNK), F32)
        return lax.fori_loop(0, CHUNK, key_row, (zero,) * GLA_HEADS)

    pair = lambda h: slice((h // 2) * LANES, (h // 2 + 1) * LANES)
    vsl = lambda h: slice(h * GLA_DV, (h + 1) * GLA_DV)
    heads = range(GLA_HEADS)

    def row_cumsum(x):
        row_id = lax.broadcasted_iota(jnp.int32, x.shape, 0)
        step = 1
        while step < CHUNK:
            if step < SUBLANES:
                moved = jnp.where(row_id >= step, pltpu.roll(x, step, 0), 0.0)
            else:
                moved = jnp.concatenate([jnp.zeros((step, x.shape[1]), x.dtype), x[:-step]], axis=0)
            x = x + moved
            step *= 2
        return x

    def chunk(seqs, c, factorised):
        rows = pl.ds(pl.multiple_of(c * CHUNK, CHUNK), CHUNK)
        q, k, b, a_h, kt, kd, eb_last, scores, o_inter = {}, {}, {}, {}, {}, {}, {}, {}, {}
        for n in seqs:
            b[n] = row_cumsum(la_ref[n, rows, :])
            b_last = b[n][CHUNK - 1:CHUNK, :]
            q[n] = q_ref[n, rows, :].astype(F32)
            k[n] = k_ref[n, rows, :].astype(F32)
            qt = q[n] * jnp.exp(b[n])
            eb_last[n] = jnp.exp(b_last)
            if factorised:
                k_grown = k[n] * jnp.exp(-b[n])
                kt[n] = k_grown.astype(BF16)
                kd[n] = k_grown * eb_last[n]
            else:
                kd[n] = k[n] * jnp.exp(b_last - b[n])
            a_h[n] = [jnp.where(head_mask[h % 2], qt[:, pair(h)], 0.0).astype(BF16) for h in heads]
        v, st, o = {}, {}, {}
        for n in seqs:
            for h in heads:
                st[n, h] = st_ref[n, h]
                if factorised:
                    both = lax.dot_general(
                        a_h[n][h], jnp.concatenate([st[n, h].astype(BF16), kt[n][:, pair(h)]], axis=0), nt,
                        preferred_element_type=F32)
                    o_inter[n, h], scores[n, h] = both[:, :GLA_DV], both[:, GLA_DV:]
                else:
                    o_inter[n, h] = lax.dot_general(a_h[n][h], st[n, h].astype(BF16), nt,
                                                    preferred_element_type=F32)
            if not factorised:
                for h, sc in enumerate(exact_scores(q[n], k[n], b[n])):
                    scores[n, h] = sc
        for n in seqs:
            for h in heads:
                sc = jnp.where(causal, scores[n, h], 0.0).astype(BF16)
                v[n, h] = v_ref[n, rows, vsl(h)]
                o[n, h] = o_inter[n, h] + jnp.dot(sc, v[n, h], preferred_element_type=F32)
        for n in seqs:
            for h in heads:
                kd_h = jnp.where(head_mask[h % 2], kd[n][:, pair(h)], 0.0).astype(BF16)
                upd = lax.dot_general(v[n, h], kd_h, tn, preferred_element_type=F32)
                st_ref[n, h] = st[n, h] * eb_last[n][:, pair(h)] + upd
        for n in seqs:
            for h in heads:
                on = o[n, h] * _rms_scale(o[n, h], GLA_DV) * gon
                o_ref[n, rows, vsl(h)] = (on * gr_ref[n, rows, vsl(h)].astype(F32)).astype(BF16)

    def factorised_chunks():
        def body(c, carry):
            chunk(range(nseq), c, True)
            return carry
        lax.fori_loop(0, rows_per_step // CHUNK, body, 0)

    def exact_chunks():
        def body(c, carry):
            for n in range(nseq):
                chunk([n], c, False)
            return carry
        lax.fori_loop(0, rows_per_step // CHUNK, body, 0)

    factorisable = jnp.min(la_ref[...]) >= GLA_FAST_MIN_LOG_DECAY / CHUNK
    lax.cond(factorisable, factorised_chunks, exact_chunks)


def _gla(qg, kg, vg, la, gr, out_norm):
    b, s, _ = qg.shape
    nseq = max(n for n in GLA_SEQS_PER_STEP if b % n == 0)
    rows = min(GLA_ROWS_PER_STEP, s)
    assert s % rows == 0 and rows % CHUNK == 0
    blk = lambda w: pl.BlockSpec((nseq, rows, w), lambda i, j: (i, j, 0))
    return pl.pallas_call(
        _gla_kernel,
        grid=(b // nseq, s // rows),
        in_specs=[blk(GLA_QK), blk(GLA_QK), blk(GLA_WIDTH), blk(GLA_QK), blk(GLA_WIDTH),
                  _const_spec((1, GLA_DV))],
        out_specs=blk(GLA_WIDTH),
        out_shape=jax.ShapeDtypeStruct((b, s, GLA_WIDTH), BF16),
        scratch_shapes=[pltpu.VMEM((nseq, GLA_HEADS, GLA_DV, LANES), F32),
                        pltpu.VMEM((CHUNK, GLA_QK), F32),
                        pltpu.VMEM((CHUNK, GLA_QK), F32)],
        compiler_params=pltpu.CompilerParams(
            dimension_semantics=("arbitrary", "arbitrary"), vmem_limit_bytes=VMEM_LIMIT_BYTES),
        name="gla",
    )(qg, kg, vg, la, gr, out_norm.reshape(1, GLA_DV))


def _mla_tiles(n_q):
    assert TQ == 2 * TK
    first = [(a, 2 * a, 0) for a in range(n_q)]
    full = [(a, j, TQ // CHUNK) for a in range(n_q) for j in range(2 * a)]
    second = [(a, 2 * a + 1, -(TK // CHUNK)) for a in range(n_q)]
    return tuple(zip(*(first + full + second)))


def _mla_kernel(ti_ref, tj_ref, toff_ref, bound_ref, q_ref, k_ref, v_ref, o_ref,
                s_ref, m_ref, acc_ref, v1_ref):
    seq = q_ref.shape[1]
    n_q = seq // TQ
    n_tiles = s_ref.shape[1]
    nt = (((1,), (1,)), ((), ()))
    heads = range(2)
    lane = lax.broadcasted_iota(jnp.int32, (1, LANES), 1)
    own = [(lane // MLA_V) == e for e in heads]
    unroll = lambda count: max(u for u in range(1, MLA_UNROLL + 1) if count % u == 0)

    def q_rows(i):
        return pl.ds(pl.multiple_of(i * TQ, TQ), TQ)

    def k_rows(j):
        return pl.ds(pl.multiple_of(j * TK, TK), TK)

    def masked_logits(e, t):
        hsl = slice(e * LANES, (e + 1) * LANES)
        s = lax.dot_general(q_ref[0, q_rows(ti_ref[t]), hsl], k_ref[0, k_rows(tj_ref[t]), hsl], nt,
                            preferred_element_type=F32)
        row_chunk = lax.broadcasted_iota(jnp.int32, (TQ, TK), 0) // CHUNK
        col_chunk = lax.broadcasted_iota(jnp.int32, (TQ, TK), 1) // CHUNK
        return jnp.where(col_chunk <= row_chunk + toff_ref[t], s, MASKED_LOGIT)

    def lane_fold_max(a):
        out = a[:, :LANES]
        for c in range(1, TK // LANES):
            out = jnp.maximum(out, a[:, c * LANES:(c + 1) * LANES])
        return out

    v = v_ref[0]
    for e in heads:
        v1_ref[e] = jnp.where(own[e], v, 1.0).astype(BF16)

    def exp_shifted(s, shift):
        return jnp.concatenate(
            [jnp.exp2(s[:, c * LANES:(c + 1) * LANES] - shift) for c in range(TK // LANES)],
            axis=1).astype(BF16)

    def accumulate(e, t, s, shift):
        acc_ref[e, ti_ref[t]] += jnp.dot(exp_shifted(s, shift), v1_ref[e, k_rows(tj_ref[t]), :],
                                         preferred_element_type=F32)

    def clear(i, carry):
        for e in heads:
            acc_ref[e, i] = jnp.zeros((TQ, LANES), F32)
        return carry

    def finish(i, carry):
        acc0, acc1 = acc_ref[0, i], acc_ref[1, i]
        num = jnp.where(own[0], acc0, acc1)
        den = pltpu.roll(jnp.where(own[0], acc1, acc0), LANES // 2, 1)
        o_ref[0, q_rows(i), :] = (num / den).astype(BF16)
        return carry

    def single_pass():
        lax.fori_loop(0, n_q, clear, 0)
        shift = bound_ref[0].astype(F32)

        def tile(t, carry):
            for e in heads:
                accumulate(e, t, masked_logits(e, t), shift)
            return carry

        lax.fori_loop(0, n_tiles - n_q, tile, 0, unroll=unroll(n_tiles - n_q))

        def lower_half_tile(t, carry):
            i = ti_ref[t]
            lower = pl.ds(pl.multiple_of(i * TQ + TK, TK), TK)
            row_chunk = lax.broadcasted_iota(jnp.int32, (TK, TK), 0) // CHUNK
            col_chunk = lax.broadcasted_iota(jnp.int32, (TK, TK), 1) // CHUNK
            for e in heads:
                hsl = slice(e * LANES, (e + 1) * LANES)
                s = lax.dot_general(q_ref[0, lower, hsl], k_ref[0, k_rows(tj_ref[t]), hsl], nt,
                                    preferred_element_type=F32)
                p = exp_shifted(jnp.where(col_chunk <= row_chunk, s, MASKED_LOGIT), shift)
                acc_ref[e, i, TK:, :] += jnp.dot(p, v1_ref[e, k_rows(tj_ref[t]), :],
                                                 preferred_element_type=F32)
            return finish(i, carry)

        lax.fori_loop(n_tiles - n_q, n_tiles, lower_half_tile, 0, unroll=unroll(n_q))

    def two_pass():
        def first_tile(t, carry):
            for e in heads:
                s = masked_logits(e, t)
                s_ref[e, t] = s
                m_ref[e, t] = lane_fold_max(s)
            return clear(t, carry)

        lax.fori_loop(0, n_q, first_tile, 0, unroll=unroll(n_q))

        def pass1(t, carry):
            i = ti_ref[t]
            for e in heads:
                s = masked_logits(e, t)
                s_ref[e, t] = s
                m_ref[e, i] = jnp.maximum(m_ref[e, i], lane_fold_max(s))
            return carry

        lax.fori_loop(n_q, n_tiles, pass1, 0, unroll=unroll(n_tiles - n_q))

        def row_max(i, carry):
            for e in heads:
                m_ref[e, i] = jnp.broadcast_to(jnp.max(m_ref[e, i], axis=-1, keepdims=True), (TQ, LANES))
            return carry

        lax.fori_loop(0, n_q, row_max, 0)

        def pass2(t, carry):
            for e in heads:
                accumulate(e, t, s_ref[e, t], m_ref[e, ti_ref[t]])
            return carry

        lax.fori_loop(0, n_tiles, pass2, 0, unroll=unroll(n_tiles))
        lax.fori_loop(0, n_q, finish, 0)

    lax.cond(bound_ref[0] <= MLA_MAX_SHIFT, single_pass, two_pass)


def _mla(q, k, v, gain_q, gain_k):
    b, s, _ = q.shape
    assert s % TQ == 0 and TQ % TK == 0
    n_q = s // TQ
    ti, tj, toff = _mla_tiles(n_q)
    bound = jnp.ceil(1.02 * LOG2E * math.sqrt(MLA_QK) * jnp.max(jnp.abs(gain_q)) * jnp.max(jnp.abs(gain_k)))
    bound = jnp.minimum(bound, 2.0 * MLA_MAX_SHIFT).astype(jnp.int32).reshape(1)
    blk = lambda w: pl.BlockSpec((1, s, w), lambda i, p, *_: (i, 0, p))
    return pl.pallas_call(
        _mla_kernel,
        grid_spec=pltpu.PrefetchScalarGridSpec(
            num_scalar_prefetch=4,
            grid=(b, MLA_HEADS // 2),
            in_specs=[blk(2 * LANES), blk(2 * LANES), blk(LANES)],
            out_specs=blk(LANES),
            scratch_shapes=[pltpu.VMEM((2, len(ti), TQ, TK), F32),
                            pltpu.VMEM((2, n_q, TQ, LANES), F32),
                            pltpu.VMEM((2, n_q, TQ, LANES), F32),
                            pltpu.VMEM((2, s, LANES), BF16)]),
        out_shape=jax.ShapeDtypeStruct((b, s, MLA_WIDTH), BF16),
        compiler_params=pltpu.CompilerParams(
            dimension_semantics=("arbitrary", "arbitrary"),
            vmem_limit_bytes=VMEM_LIMIT_BYTES),
        name="mla",
    )(jnp.asarray(ti, jnp.int32), jnp.asarray(tj, jnp.int32), jnp.asarray(toff, jnp.int32), bound,
      q, k, v)


def _tail_kernel(x_ref, og_ref, om_ref, p_ref, w_out_ref, g_mlp_ref, w_up_ref, w_down_ref,
                 g_ple_ref, w_gate_ref, b_gate_ref, w_pp_ref, o_ref):
    h = (x_ref[...]
         + jnp.dot(og_ref[...], w_out_ref[:GLA_WIDTH, :], preferred_element_type=F32)
         + jnp.dot(om_ref[...], w_out_ref[GLA_WIDTH:, :], preferred_element_type=F32))
    m = (h * _rms_scale(h, D_MODEL) * g_mlp_ref[...]).astype(BF16)
    mlp = None
    for c in range(D_FF // FF_CHUNK):
        cols = slice(c * FF_CHUNK, (c + 1) * FF_CHUNK)
        u = jnp.maximum(jnp.dot(m, w_up_ref[:, cols], preferred_element_type=F32), 0.0)
        d = jnp.dot((u * u).astype(BF16), w_down_ref[cols, :], preferred_element_type=F32)
        mlp = d if mlp is None else mlp + d
    h = h + mlp
    g = (h * _rms_scale(h, D_MODEL) * g_ple_ref[...]).astype(BF16)
    gate = jax.nn.sigmoid(jnp.dot(g, w_gate_ref[...], preferred_element_type=F32) + b_gate_ref[...])
    pp = jnp.dot(p_ref[...].astype(BF16), w_pp_ref[...], preferred_element_type=F32)
    o_ref[...] = h + pp * gate


def _tail(x2, og, om, p2, w_out, mlp_norm, w_up, w_down, ple_norm, w_gate, b_gate, w_pp):
    t = x2.shape[0]
    tm = TM_TAIL
    assert t % tm == 0
    row = lambda w: pl.BlockSpec((tm, w), lambda i: (i, 0))
    return pl.pallas_call(
        _tail_kernel,
        grid=(t // tm,),
        in_specs=[row(D_MODEL), row(GLA_WIDTH), row(MLA_WIDTH), row(PLE_DIM),
                  _const_spec((D_MODEL, D_MODEL)), _const_spec((1, D_MODEL)),
                  _const_spec((D_MODEL, D_FF)), _const_spec((D_FF, D_MODEL)),
                  _const_spec((1, D_MODEL)), _const_spec((D_MODEL, D_MODEL)),
                  _const_spec((1, D_MODEL)), _const_spec((PLE_DIM, D_MODEL))],
        out_specs=row(D_MODEL),
        out_shape=jax.ShapeDtypeStruct((t, D_MODEL), F32),
        compiler_params=pltpu.CompilerParams(
            dimension_semantics=("arbitrary",), vmem_limit_bytes=VMEM_LIMIT_BYTES),
        name="tail",
    )(x2, og, om, p2, w_out.astype(BF16), mlp_norm.reshape(1, D_MODEL), w_up.astype(BF16),
      w_down.astype(BF16), ple_norm.reshape(1, D_MODEL), w_gate.astype(BF16),
      b_gate.reshape(1, D_MODEL), w_pp.astype(BF16))


def kernel(x, p, positions, attn_norm, w_in, gla_gate_w2, gla_gate_b, gla_out_norm, mla_q_norm,
           mla_w_uq, mla_kv_norm, mla_w_ukv, qk_norm_q, qk_norm_k, w_out, mlp_norm, w_mlp_up,
           w_mlp_down, ple_norm, w_ple_gate, b_ple_gate, w_ple_proj):
    b, s, d = x.shape
    depth = w_in.shape[0]
    t = b * s
    pos2 = positions.reshape(t, 1)
    h = x.reshape(t, d)
    for i in range(depth):
        qg, kg, vg, la, gr, q, k, v = _projection(
            h, pos2, attn_norm[i], w_in[i], gla_gate_w2[i], gla_gate_b[i], mla_q_norm[i],
            mla_w_uq[i], mla_kv_norm[i], mla_w_ukv[i], qk_norm_q[i], qk_norm_k[i])
        seq = lambda a: a.reshape(b, s, a.shape[-1])
        og = _gla(seq(qg), seq(kg), seq(vg), seq(la), seq(gr), gla_out_norm[i])
        om = _mla(seq(q), seq(k), seq(v), qk_norm_q[i], qk_norm_k[i])
        h = _tail(h, og.reshape(t, GLA_WIDTH), om.reshape(t, MLA_WIDTH), p[i].reshape(t, PLE_DIM),
                  w_out[i], mlp_norm[i], w_mlp_up[i], w_mlp_down[i], ple_norm[i], w_ple_gate[i],
                  b_ple_gate[i], w_ple_proj[i])
    return h.reshape(b, s, d)
```

```python
import math

import jax
import jax.numpy as jnp
from jax import lax
from jax.experimental import pallas as pl
from jax.experimental.pallas import tpu as pltpu

F32 = jnp.float32
BF16 = jnp.bfloat16

D_MODEL = 1024
CHUNK = 64
PLE_DIM = 256
D_FF = 4 * D_MODEL
EPS = 1e-6
GLA_HEADS = 4
GLA_DK = 64
GLA_DV = 128
GLA_GATE_RANK = 16
GLA_TAU = 16.0
GLA_QK = GLA_HEADS * GLA_DK
GLA_WIDTH = GLA_HEADS * GLA_DV
MLA_HEADS = 8
MLA_NOPE = 64
MLA_ROPE = 32
MLA_V = 64
MLA_QK = MLA_NOPE + MLA_ROPE
MLA_Q_RANK = 256
MLA_KV_RANK = 128
MLA_WIDTH = MLA_HEADS * MLA_V
ROPE_THETA = 10000.0
LOG2E = math.log2(math.e)
IN_SPLITS = (GLA_QK, GLA_QK, GLA_WIDTH, GLA_GATE_RANK, GLA_WIDTH, MLA_Q_RANK, MLA_KV_RANK, MLA_ROPE)

LANES = 128
SUBLANES = 8
HALF_ROPE = MLA_ROPE // 2
HALF_NOPE = MLA_NOPE // 2
HEAD_PAD = MLA_HEADS * LANES
D_IN_PAD = 2048
W_IN_BLOCK = 256
VMEM_LIMIT_BYTES = 56 * 1024 * 1024

TM_PROJ = 512
TM_TAIL = 512
FF_CHUNK = 1024
TQ = 512
TK = 256
MLA_UNROLL = 8
ROPE_ROWS = 64
GLA_SEQS_PER_STEP = (8, 4, 2, 1)
GLA_ROWS_PER_STEP = 256
GLA_FAST_MIN_LOG_DECAY = -80.0
MASKED_LOGIT = -1e30
MLA_MAX_SHIFT = 40

ROW_FREQ16, ROW_MASK96, ROW_KRMASK, ROW_GQ, ROW_GQ_SW, ROW_GK, ROW_GK_SW = range(7)


def _const_spec(shape):
    return pl.BlockSpec(shape, lambda *_: (0,) * len(shape), pipeline_mode=pl.Buffered(1))


def _rms_scale(v, n):
    return lax.rsqrt(jnp.sum(v * v, axis=-1, keepdims=True) * (1.0 / n) + EPS)


def _proj_kernel(x_ref, pos_ref, g_attn_ref, w_in_ref, w2_ref, b2_ref, gqn_ref, w_uq_ref,
                 gkvn_ref, w_uk_ref, w_uv_ref, tab_ref,
                 qg_ref, kg_ref, vg_ref, la_ref, gr_ref, q_ref, k_ref, v_ref,
                 qs_ref, ks_ref, slot_ref):
    @pl.when(pl.program_id(0) == 0)
    def _():
        qs_ref[...] = jnp.zeros_like(qs_ref)
        ks_ref[...] = jnp.zeros_like(ks_ref)
        slot_ref[...] = jnp.zeros_like(slot_ref)

    x = x_ref[...]
    xn = (x * _rms_scale(x, D_MODEL) * g_attn_ref[...]).astype(BF16)

    tab = lambda r: tab_ref[r:r + 1, :]
    mask96, krmask = tab(ROW_MASK96), tab(ROW_KRMASK)
    gq, gq_sw = tab(ROW_GQ) * LOG2E, tab(ROW_GQ_SW) * LOG2E
    gk, gk_sw = tab(ROW_GK) * math.sqrt(MLA_QK), tab(ROW_GK_SW) * math.sqrt(MLA_QK)
    half = LANES // 2

    lane = lax.broadcasted_iota(jnp.int32, (1, LANES), 1)
    groups = ROPE_ROWS // SUBLANES
    assert groups * HALF_ROPE == LANES
    x1_lanes = (lane >= HALF_NOPE) & (lane < HALF_NOPE + HALF_ROPE)
    x2_lanes = (lane >= HALF_NOPE + HALF_ROPE) & (lane < HALF_NOPE + MLA_ROPE)

    def rope_tables(rb):
        ang = pos_ref[rb].astype(F32) * tab(ROW_FREQ16)
        cos_p, sin_p = jnp.cos(ang), jnp.sin(ang)
        cos, sin = [], []
        for a in range(groups):
            s1 = (HALF_NOPE - HALF_ROPE * a) % LANES
            s2 = (HALF_NOPE + HALF_ROPE - HALF_ROPE * a) % LANES
            c1, c2 = pltpu.roll(cos_p, s1, 1), pltpu.roll(cos_p, s2, 1)
            n1, n2 = pltpu.roll(sin_p, s1, 1), pltpu.roll(sin_p, s2, 1)
            cos.append(jnp.where(x1_lanes, c1, jnp.where(x2_lanes, c2, mask96)))
            sin.append(jnp.where(x1_lanes, -n1, jnp.where(x2_lanes, n2, 0.0)))
        return jnp.concatenate(cos, axis=0), jnp.concatenate(sin, axis=0)

    sum96 = (lax.broadcasted_iota(jnp.int32, (LANES, LANES), 0) < MLA_QK).astype(BF16)

    def head_rows(rb):
        rows = pl.ds(rb * ROPE_ROWS, ROPE_ROWS)
        cos, sin = rope_tables(rb)
        kr = slot_ref[rows, :] * krmask
        kr_rot = pltpu.roll(kr, half, 1) * (sin * gk_sw)
        cos_q, sin_q, cos_k = cos * gq, sin * gq_sw, cos * gk
        slabs = []
        for h in range(MLA_HEADS):
            sl = slice(h * LANES, (h + 1) * LANES)
            slabs += [qs_ref[rows, sl], ks_ref[rows, sl] + kr]
        squares = jnp.concatenate([(v * v).astype(BF16) for v in slabs], axis=0)
        rms = lax.rsqrt(jnp.dot(squares, sum96, preferred_element_type=F32) + MLA_QK * EPS)
        for h in range(MLA_HEADS):
            sl = slice(h * LANES, (h + 1) * LANES)
            qh, kh = slabs[2 * h], slabs[2 * h + 1]
            rq = rms[(2 * h) * ROPE_ROWS:(2 * h + 1) * ROPE_ROWS]
            rk = rms[(2 * h + 1) * ROPE_ROWS:(2 * h + 2) * ROPE_ROWS]
            q_ref[rows, sl] = ((qh * cos_q + pltpu.roll(qh, half, 1) * sin_q) * rq).astype(BF16)
            k_ref[rows, sl] = ((kh * cos_k + kr_rot) * rk).astype(BF16)

    n_blocks = w_in_ref.shape[0]
    assert x_ref.shape[0] // ROPE_ROWS == n_blocks == 8
    cq = ckv = slot = None
    for r in range(n_blocks):
        z = jnp.dot(xn, w_in_ref[r], preferred_element_type=F32)
        head_rows(r)
        half_cols = slice((r % 2) * W_IN_BLOCK, (r % 2 + 1) * W_IN_BLOCK)
        if r == 0:
            qg_ref[...] = (z * (GLA_DK ** -0.5)).astype(BF16)
        elif r == 1:
            kg_ref[...] = z.astype(BF16)
        elif r in (2, 3):
            vg_ref[:, half_cols] = z.astype(BF16)
        elif r in (4, 5):
            gr_ref[:, half_cols] = (z * jax.nn.sigmoid(z)).astype(BF16)
        elif r == 6:
            cq = z
        else:
            ckv, slot = z[:, :MLA_KV_RANK], z[:, MLA_KV_RANK:]
    slot_ref[...] = slot

    xg = jnp.dot(slot.astype(BF16), w2_ref[...], preferred_element_type=F32) + b2_ref[...]
    la_ref[...] = (jnp.minimum(xg, 0.0) - jnp.log(1.0 + jnp.exp(-jnp.abs(xg)))) * (1.0 / GLA_TAU)

    cqn = (cq * _rms_scale(cq, MLA_Q_RANK) * gqn_ref[...]).astype(BF16)
    qs_ref[...] = jnp.dot(cqn, w_uq_ref[...], preferred_element_type=F32)
    ckvn = (ckv * _rms_scale(ckv, MLA_KV_RANK) * gkvn_ref[...]).astype(BF16)
    ks_ref[...] = jnp.dot(ckvn, w_uk_ref[...], preferred_element_type=F32)
    v_ref[...] = jnp.dot(ckvn, w_uv_ref[...], preferred_element_type=F32).astype(BF16)


def _head_layout(nope, rope):
    x1, x2 = rope[..., :HALF_ROPE], rope[..., HALF_ROPE:]
    return jnp.concatenate([nope[..., :HALF_NOPE], x1, x2, nope[..., HALF_NOPE:], x2, x1], axis=-1)


def _projection(x2, positions, attn_norm, w_in, gate_w2, gate_b, q_norm, w_uq, kv_norm, w_ukv,
                qk_norm_q, qk_norm_k):
    t = x2.shape[0]
    tm = TM_PROJ
    assert t % tm == 0

    gq, gk, gv, g_low, g_r, c_q, c_kv, k_r = jnp.split(w_in, list(_cumsum(IN_SPLITS))[:-1], axis=1)
    zeros = lambda n: jnp.zeros((D_MODEL, n), w_in.dtype)
    kr_sw = jnp.concatenate([k_r[:, HALF_ROPE:], k_r[:, :HALF_ROPE]], axis=1)
    slot = jnp.concatenate([g_low, zeros(HALF_NOPE - GLA_GATE_RANK), k_r, zeros(HALF_NOPE), kr_sw], axis=1)
    w_in_l = jnp.concatenate([gq, gk, gv, g_r, c_q, c_kv, slot], axis=1).astype(BF16)
    assert w_in_l.shape == (D_MODEL, D_IN_PAD)

    w2_l = jnp.zeros((LANES, GLA_QK), F32).at[:GLA_GATE_RANK].set(gate_w2).astype(BF16)

    w_uq_h = w_uq.reshape(MLA_Q_RANK, MLA_HEADS, MLA_QK)
    w_uq_l = _head_layout(w_uq_h[..., :MLA_NOPE], w_uq_h[..., MLA_NOPE:]).reshape(MLA_Q_RANK, HEAD_PAD)
    w_ukv_h = w_ukv.reshape(MLA_KV_RANK, MLA_HEADS, MLA_NOPE + MLA_V)
    w_uk_l = _head_layout(w_ukv_h[..., :MLA_NOPE],
                          jnp.zeros((MLA_KV_RANK, MLA_HEADS, MLA_ROPE), F32)).reshape(MLA_KV_RANK, HEAD_PAD)
    w_uv = w_ukv_h[..., MLA_NOPE:].reshape(MLA_KV_RANK, MLA_WIDTH)
    gq_l = _head_layout(qk_norm_q[:MLA_NOPE], qk_norm_q[MLA_NOPE:])
    gk_l = _head_layout(qk_norm_k[:MLA_NOPE], qk_norm_k[MLA_NOPE:])

    inv_freq = ROPE_THETA ** (-jnp.arange(0, MLA_ROPE, 2, dtype=F32) / MLA_ROPE)
    z32, o32 = jnp.zeros((HALF_NOPE,), F32), jnp.ones((MLA_ROPE,), F32)
    tab = jnp.zeros((SUBLANES, LANES), F32)
    tab = tab.at[ROW_FREQ16].set(jnp.tile(inv_freq, LANES // HALF_ROPE))
    tab = tab.at[ROW_MASK96].set(jnp.concatenate([jnp.ones((MLA_QK,), F32), z32]))
    tab = tab.at[ROW_KRMASK].set(jnp.concatenate([z32, o32, z32, o32]))
    tab = tab.at[ROW_GQ].set(gq_l).at[ROW_GQ_SW].set(jnp.roll(gq_l, LANES // 2))
    tab = tab.at[ROW_GK].set(gk_l).at[ROW_GK_SW].set(jnp.roll(gk_l, LANES // 2))

    groups = ROPE_ROWS // SUBLANES
    pos_packed = jnp.broadcast_to(
        positions.reshape(t // ROPE_ROWS, groups, SUBLANES).transpose(0, 2, 1)[..., None],
        (t // ROPE_ROWS, SUBLANES, groups, HALF_ROPE)).reshape(t // ROPE_ROWS, SUBLANES, LANES)

    n_blocks = D_IN_PAD // W_IN_BLOCK
    w_in_l = w_in_l.reshape(D_MODEL, n_blocks, W_IN_BLOCK).transpose(1, 0, 2)

    n = t // tm
    tile = lambda w: pl.BlockSpec((tm, w), lambda i: (jnp.minimum(i, n - 1), 0))
    prev = lambda w: pl.BlockSpec((tm, w), lambda i: (jnp.maximum(i - 1, 0), 0))
    out_widths = (GLA_QK, GLA_QK, GLA_WIDTH, GLA_QK, GLA_WIDTH, HEAD_PAD, HEAD_PAD, MLA_WIDTH)
    out_dtypes = (BF16, BF16, BF16, F32, BF16, BF16, BF16, BF16)
    out_specs = [tile(w) for w in out_widths]
    out_specs[5], out_specs[6] = prev(HEAD_PAD), prev(HEAD_PAD)
    return pl.pallas_call(
        _proj_kernel,
        grid=(n + 1,),
        in_specs=[
            tile(D_MODEL),
            pl.BlockSpec((tm // ROPE_ROWS, SUBLANES, LANES), lambda i: (jnp.maximum(i - 1, 0), 0, 0)),
            _const_spec((1, D_MODEL)), _const_spec((n_blocks, D_MODEL, W_IN_BLOCK)),
            _const_spec((LANES, GLA_QK)), _const_spec((1, GLA_QK)),
            _const_spec((1, MLA_Q_RANK)), _const_spec((MLA_Q_RANK, HEAD_PAD)),
            _const_spec((1, MLA_KV_RANK)), _const_spec((MLA_KV_RANK, HEAD_PAD)),
            _const_spec((MLA_KV_RANK, MLA_WIDTH)),
            _const_spec((8, LANES)),
        ],
        out_specs=out_specs,
        out_shape=[jax.ShapeDtypeStruct((t, w), d) for w, d in zip(out_widths, out_dtypes)],
        scratch_shapes=[pltpu.VMEM((tm, HEAD_PAD), F32), pltpu.VMEM((tm, HEAD_PAD), F32),
                        pltpu.VMEM((tm, LANES), F32)],
        compiler_params=pltpu.CompilerParams(
            dimension_semantics=("arbitrary",), vmem_limit_bytes=VMEM_LIMIT_BYTES),
        name="proj",
    )(x2, pos_packed, attn_norm.reshape(1, D_MODEL), w_in_l, w2_l, gate_b.reshape(1, GLA_QK),
      q_norm.reshape(1, MLA_Q_RANK), w_uq_l.astype(BF16), kv_norm.reshape(1, MLA_KV_RANK),
      w_uk_l.astype(BF16), w_uv.astype(BF16), tab)


def _cumsum(sizes):
    total = 0
    for s in sizes:
        total += s
        yield total


def _gla_kernel(q_ref, k_ref, v_ref, la_ref, gr_ref, gon_ref, o_ref, st_ref, kf_ref, b_ref):
    nseq, rows_per_step = q_ref.shape[0], q_ref.shape[1]

    @pl.when(pl.program_id(1) == 0)
    def _():
        st_ref[...] = jnp.zeros_like(st_ref)

    row = lax.broadcasted_iota(jnp.int32, (CHUNK, CHUNK), 0)
    col = lax.broadcasted_iota(jnp.int32, (CHUNK, CHUNK), 1)
    causal = col <= row
    lane = lax.broadcasted_iota(jnp.int32, (1, LANES), 1)
    lane_q = lax.broadcasted_iota(jnp.int32, (1, GLA_QK), 1)
    head_mask = [(lane // GLA_DK) == e for e in range(2)]
    gon = gon_ref[...]
    nt = (((1,), (1,)), ((), ()))
    tn = (((0,), (0,)), ((), ()))

    def exact_scores(q, k, b):
        kf_ref[...] = k
        b_ref[...] = b

        def key_row(s, sc):
            k_s = kf_ref[pl.ds(s, 1), :]
            b_s = b_ref[pl.ds(s, 1), :]
            prod = q * k_s * jnp.exp(jnp.minimum(b - b_s, 0.0))
            out = []
            for h in range(GLA_HEADS):
                hm = (lane_q // GLA_DK) == h
                col_h = jnp.sum(jnp.where(hm, prod, 0.0), axis=-1, keepdims=True)
                out.append(jnp.where(col == s, col_h, sc[h]))
            return tuple(out)

        zero = jnp.zeros((CHUNK, CHUNK), F32)
        return lax.fori_loop(0, CHUNK, key_row, (zero,) * GLA_HEADS)

    pair = lambda h: slice((h // 2) * LANES, (h // 2 + 1) * LANES)
    vsl = lambda h: slice(h * GLA_DV, (h + 1) * GLA_DV)
    heads = range(GLA_HEADS)

    def row_cumsum(x):
        row_id = lax.broadcasted_iota(jnp.int32, x.shape, 0)
        step = 1
        while step < CHUNK:
            if step < SUBLANES:
                moved = jnp.where(row_id >= step, pltpu.roll(x, step, 0), 0.0)
            else:
                moved = jnp.concatenate([jnp.zeros((step, x.shape[1]), x.dtype), x[:-step]], axis=0)
            x = x + moved
            step *= 2
        return x

    def chunk(seqs, c, factorised):
        rows = pl.ds(pl.multiple_of(c * CHUNK, CHUNK), CHUNK)
        q, k, b, a_h, kt, kd, eb_last, scores, o_inter = {}, {}, {}, {}, {}, {}, {}, {}, {}
        for n in seqs:
            b[n] = row_cumsum(la_ref[n, rows, :])
            b_last = b[n][CHUNK - 1:CHUNK, :]
            q[n] = q_ref[n, rows, :].astype(F32)
            k[n] = k_ref[n, rows, :].astype(F32)
            qt = q[n] * jnp.exp(b[n])
            eb_last[n] = jnp.exp(b_last)
            if factorised:
                k_grown = k[n] * jnp.exp(-b[n])
                kt[n] = k_grown.astype(BF16)
                kd[n] = k_grown * eb_last[n]
            else:
                kd[n] = k[n] * jnp.exp(b_last - b[n])
            a_h[n] = [jnp.where(head_mask[h % 2], qt[:, pair(h)], 0.0).astype(BF16) for h in heads]
        v, st, o = {}, {}, {}
        for n in seqs:
            for h in heads:
                st[n, h] = st_ref[n, h]
                if factorised:
                    both = lax.dot_general(
                        a_h[n][h], jnp.concatenate([st[n, h].astype(BF16), kt[n][:, pair(h)]], axis=0), nt,
                        preferred_element_type=F32)
                    o_inter[n, h], scores[n, h] = both[:, :GLA_DV], both[:, GLA_DV:]
                else:
                    o_inter[n, h] = lax.dot_general(a_h[n][h], st[n, h].astype(BF16), nt,
                                                    preferred_element_type=F32)
            if not factorised:
                for h, sc in enumerate(exact_scores(q[n], k[n], b[n])):
                    scores[n, h] = sc
        for n in seqs:
            for h in heads:
                sc = jnp.where(causal, scores[n, h], 0.0).astype(BF16)
                v[n, h] = v_ref[n, rows, vsl(h)]
                o[n, h] = o_inter[n, h] + jnp.dot(sc, v[n, h], preferred_element_type=F32)
        for n in seqs:
            for h in heads:
                kd_h = jnp.where(head_mask[h % 2], kd[n][:, pair(h)], 0.0).astype(BF16)
                upd = lax.dot_general(v[n, h], kd_h, tn, preferred_element_type=F32)
                st_ref[n, h] = st[n, h] * eb_last[n][:, pair(h)] + upd
        for n in seqs:
            for h in heads:
                on = o[n, h] * _rms_scale(o[n, h], GLA_DV) * gon
                o_ref[n, rows, vsl(h)] = (on * gr_ref[n, rows, vsl(h)].astype(F32)).astype(BF16)

    def factorised_chunks():
        def body(c, carry):
            chunk(range(nseq), c, True)
            return carry
        lax.fori_loop(0, rows_per_step // CHUNK, body, 0)

    def exact_chunks():
        def body(c, carry):
            for n in range(nseq):
                chunk([n], c, False)
            return carry
        lax.fori_loop(0, rows_per_step // CHUNK, body, 0)

    factorisable = jnp.min(la_ref[...]) >= GLA_FAST_MIN_LOG_DECAY / CHUNK
    lax.cond(factorisable, factorised_chunks, exact_chunks)


def _gla(qg, kg, vg, la, gr, out_norm):
    b, s, _ = qg.shape
    nseq = max(n for n in GLA_SEQS_PER_STEP if b % n == 0)
    rows = min(GLA_ROWS_PER_STEP, s)
    assert s % rows == 0 and rows % CHUNK == 0
    blk = lambda w: pl.BlockSpec((nseq, rows, w), lambda i, j: (i, j, 0))
    return pl.pallas_call(
        _gla_kernel,
        grid=(b // nseq, s // rows),
        in_specs=[blk(GLA_QK), blk(GLA_QK), blk(GLA_WIDTH), blk(GLA_QK), blk(GLA_WIDTH),
                  _const_spec((1, GLA_DV))],
        out_specs=blk(GLA_WIDTH),
        out_shape=jax.ShapeDtypeStruct((b, s, GLA_WIDTH), BF16),
        scratch_shapes=[pltpu.VMEM((nseq, GLA_HEADS, GLA_DV, LANES), F32),
                        pltpu.VMEM((CHUNK, GLA_QK), F32),
                        pltpu.VMEM((CHUNK, GLA_QK), F32)],
        compiler_params=pltpu.CompilerParams(
            dimension_semantics=("arbitrary", "arbitrary"), vmem_limit_bytes=VMEM_LIMIT_BYTES),
        name="gla",
    )(qg, kg, vg, la, gr, out_norm.reshape(1, GLA_DV))


def _mla_tiles(n_q):
    assert TQ == 2 * TK
    first = [(a, 2 * a, 0) for a in range(n_q)]
    full = [(a, j, TQ // CHUNK) for a in range(n_q) for j in range(2 * a)]
    second = [(a, 2 * a + 1, -(TK // CHUNK)) for a in range(n_q)]
    return tuple(zip(*(first + full + second)))


def _mla_kernel(ti_ref, tj_ref, toff_ref, bound_ref, q_ref, k_ref, v_ref, o_ref,
                s_ref, m_ref, acc_ref, v1_ref):
    seq = q_ref.shape[1]
    n_q = seq // TQ
    n_tiles = s_ref.shape[1]
    nt = (((1,), (1,)), ((), ()))
    heads = range(2)
    lane = lax.broadcasted_iota(jnp.int32, (1, LANES), 1)
    own = [(lane // MLA_V) == e for e in heads]
    unroll = lambda count: max(u for u in range(1, MLA_UNROLL + 1) if count % u == 0)

    def q_rows(i):
        return pl.ds(pl.multiple_of(i * TQ, TQ), TQ)

    def k_rows(j):
        return pl.ds(pl.multiple_of(j * TK, TK), TK)

    def masked_logits(e, t):
        hsl = slice(e * LANES, (e + 1) * LANES)
        s = lax.dot_general(q_ref[0, q_rows(ti_ref[t]), hsl], k_ref[0, k_rows(tj_ref[t]), hsl], nt,
                            preferred_element_type=F32)
        row_chunk = lax.broadcasted_iota(jnp.int32, (TQ, TK), 0) // CHUNK
        col_chunk = lax.broadcasted_iota(jnp.int32, (TQ, TK), 1) // CHUNK
        return jnp.where(col_chunk <= row_chunk + toff_ref[t], s, MASKED_LOGIT)

    def lane_fold_max(a):
        out = a[:, :LANES]
        for c in range(1, TK // LANES):
            out = jnp.maximum(out, a[:, c * LANES:(c + 1) * LANES])
        return out

    v = v_ref[0]
    for e in heads:
        v1_ref[e] = jnp.where(own[e], v, 1.0).astype(BF16)

    def exp_shifted(s, shift):
        return jnp.concatenate(
            [jnp.exp2(s[:, c * LANES:(c + 1) * LANES] - shift) for c in range(TK // LANES)],
            axis=1).astype(BF16)

    def accumulate(e, t, s, shift):
        acc_ref[e, ti_ref[t]] += jnp.dot(exp_shifted(s, shift), v1_ref[e, k_rows(tj_ref[t]), :],
                                         preferred_element_type=F32)

    def clear(i, carry):
        for e in heads:
            acc_ref[e, i] = jnp.zeros((TQ, LANES), F32)
        return carry

    def finish(i, carry):
        acc0, acc1 = acc_ref[0, i], acc_ref[1, i]
        num = jnp.where(own[0], acc0, acc1)
        den = pltpu.roll(jnp.where(own[0], acc1, acc0), LANES // 2, 1)
        o_ref[0, q_rows(i), :] = (num / den).astype(BF16)
        return carry

    def single_pass():
        lax.fori_loop(0, n_q, clear, 0)
        shift = bound_ref[0].astype(F32)

        def tile(t, carry):
            for e in heads:
                accumulate(e, t, masked_logits(e, t), shift)
            return carry

        lax.fori_loop(0, n_tiles - n_q, tile, 0, unroll=unroll(n_tiles - n_q))

        def lower_half_tile(t, carry):
            i = ti_ref[t]
            lower = pl.ds(pl.multiple_of(i * TQ + TK, TK), TK)
            row_chunk = lax.broadcasted_iota(jnp.int32, (TK, TK), 0) // CHUNK
            col_chunk = lax.broadcasted_iota(jnp.int32, (TK, TK), 1) // CHUNK
            for e in heads:
                hsl = slice(e * LANES, (e + 1) * LANES)
                s = lax.dot_general(q_ref[0, lower, hsl], k_ref[0, k_rows(tj_ref[t]), hsl], nt,
                                    preferred_element_type=F32)
                p = exp_shifted(jnp.where(col_chunk <= row_chunk, s, MASKED_LOGIT), shift)
                acc_ref[e, i, TK:, :] += jnp.dot(p, v1_ref[e, k_rows(tj_ref[t]), :],
                                                 preferred_element_type=F32)
            return finish(i, carry)

        lax.fori_loop(n_tiles - n_q, n_tiles, lower_half_tile, 0, unroll=unroll(n_q))

    def two_pass():
        def first_tile(t, carry):
            for e in heads:
                s = masked_logits(e, t)
                s_ref[e, t] = s
                m_ref[e, t] = lane_fold_max(s)
            return clear(t, carry)

        lax.fori_loop(0, n_q, first_tile, 0, unroll=unroll(n_q))

        def pass1(t, carry):
            i = ti_ref[t]
            for e in heads:
                s = masked_logits(e, t)
                s_ref[e, t] = s
                m_ref[e, i] = jnp.maximum(m_ref[e, i], lane_fold_max(s))
            return carry

        lax.fori_loop(n_q, n_tiles, pass1, 0, unroll=unroll(n_tiles - n_q))

        def row_max(i, carry):
            for e in heads:
                m_ref[e, i] = jnp.broadcast_to(jnp.max(m_ref[e, i], axis=-1, keepdims=True), (TQ, LANES))
            return carry

        lax.fori_loop(0, n_q, row_max, 0)

        def pass2(t, carry):
            for e in heads:
                accumulate(e, t, s_ref[e, t], m_ref[e, ti_ref[t]])
            return carry

        lax.fori_loop(0, n_tiles, pass2, 0, unroll=unroll(n_tiles))
        lax.fori_loop(0, n_q, finish, 0)

    lax.cond(bound_ref[0] <= MLA_MAX_SHIFT, single_pass, two_pass)


def _mla(q, k, v, gain_q, gain_k):
    b, s, _ = q.shape
    assert s % TQ == 0 and TQ % TK == 0
    n_q = s // TQ
    ti, tj, toff = _mla_tiles(n_q)
    bound = jnp.ceil(1.02 * LOG2E * math.sqrt(MLA_QK) * jnp.max(jnp.abs(gain_q)) * jnp.max(jnp.abs(gain_k)))
    bound = jnp.minimum(bound, 2.0 * MLA_MAX_SHIFT).astype(jnp.int32).reshape(1)
    blk = lambda w: pl.BlockSpec((1, s, w), lambda i, p, *_: (i, 0, p))
    return pl.pallas_call(
        _mla_kernel,
        grid_spec=pltpu.PrefetchScalarGridSpec(
            num_scalar_prefetch=4,
            grid=(b, MLA_HEADS // 2),
            in_specs=[blk(2 * LANES), blk(2 * LANES), blk(LANES)],
            out_specs=blk(LANES),
            scratch_shapes=[pltpu.VMEM((2, len(ti), TQ, TK), F32),
                            pltpu.VMEM((2, n_q, TQ, LANES), F32),
                            pltpu.VMEM((2, n_q, TQ, LANES), F32),
                            pltpu.VMEM((2, s, LANES), BF16)]),
        out_shape=jax.ShapeDtypeStruct((b, s, MLA_WIDTH), BF16),
        compiler_params=pltpu.CompilerParams(
            dimension_semantics=("arbitrary", "arbitrary"),
            vmem_limit_bytes=VMEM_LIMIT_BYTES),
        name="mla",
    )(jnp.asarray(ti, jnp.int32), jnp.asarray(tj, jnp.int32), jnp.asarray(toff, jnp.int32), bound,
      q, k, v)


def _tail_kernel(x_ref, og_ref, om_ref, p_ref, w_out_ref, g_mlp_ref, w_up_ref, w_down_ref,
                 g_ple_ref, w_gate_ref, b_gate_ref, w_pp_ref, o_ref):
    h = (x_ref[...]
         + jnp.dot(og_ref[...], w_out_ref[:GLA_WIDTH, :], preferred_element_type=F32)
         + jnp.dot(om_ref[...], w_out_ref[GLA_WIDTH:, :], preferred_element_type=F32))
    m = (h * _rms_scale(h, D_MODEL) * g_mlp_ref[...]).astype(BF16)
    mlp = None
    for c in range(D_FF // FF_CHUNK):
        cols = slice(c * FF_CHUNK, (c + 1) * FF_CHUNK)
        u = jnp.maximum(jnp.dot(m, w_up_ref[:, cols], preferred_element_type=F32), 0.0)
        d = jnp.dot((u * u).astype(BF16), w_down_ref[cols, :], preferred_element_type=F32)
        mlp = d if mlp is None else mlp + d
    h = h + mlp
    g = (h * _rms_scale(h, D_MODEL) * g_ple_ref[...]).astype(BF16)
    gate = jax.nn.sigmoid(jnp.dot(g, w_gate_ref[...], preferred_element_type=F32) + b_gate_ref[...])
    pp = jnp.dot(p_ref[...].astype(BF16), w_pp_ref[...], preferred_element_type=F32)
    o_ref[...] = h + pp * gate


def _tail(x2, og, om, p2, w_out, mlp_norm, w_up, w_down, ple_norm, w_gate, b_gate, w_pp):
    t = x2.shape[0]
    tm = TM_TAIL
    assert t % tm == 0
    row = lambda w: pl.BlockSpec((tm, w), lambda i: (i, 0))
    return pl.pallas_call(
        _tail_kernel,
        grid=(t // tm,),
        in_specs=[row(D_MODEL), row(GLA_WIDTH), row(MLA_WIDTH), row(PLE_DIM),
                  _const_spec((D_MODEL, D_MODEL)), _const_spec((1, D_MODEL)),
                  _const_spec((D_MODEL, D_FF)), _const_spec((D_FF, D_MODEL)),
                  _const_spec((1, D_MODEL)), _const_spec((D_MODEL, D_MODEL)),
                  _const_spec((1, D_MODEL)), _const_spec((PLE_DIM, D_MODEL))],
        out_specs=row(D_MODEL),
        out_shape=jax.ShapeDtypeStruct((t, D_MODEL), F32),
        compiler_params=pltpu.CompilerParams(
            dimension_semantics=("arbitrary",), vmem_limit_bytes=VMEM_LIMIT_BYTES),
        name="tail",
    )(x2, og, om, p2, w_out.astype(BF16), mlp_norm.reshape(1, D_MODEL), w_up.astype(BF16),
      w_down.astype(BF16), ple_norm.reshape(1, D_MODEL), w_gate.astype(BF16),
      b_gate.reshape(1, D_MODEL), w_pp.astype(BF16))


def kernel(x, p, positions, attn_norm, w_in, gla_gate_w2, gla_gate_b, gla_out_norm, mla_q_norm,
           mla_w_uq, mla_kv_norm, mla_w_ukv, qk_norm_q, qk_norm_k, w_out, mlp_norm, w_mlp_up,
           w_mlp_down, ple_norm, w_ple_gate, b_ple_gate, w_ple_proj):
    b, s, d = x.shape
    depth = w_in.shape[0]
    t = b * s
    h = x.reshape(t, d)
    for i in range(depth):
        qg, kg, vg, la, gr, q, k, v = _projection(
            h, positions, attn_norm[i], w_in[i], gla_gate_w2[i], gla_gate_b[i], mla_q_norm[i],
            mla_w_uq[i], mla_kv_norm[i], mla_w_ukv[i], qk_norm_q[i], qk_norm_k[i])
        seq = lambda a: a.reshape(b, s, a.shape[-1])
        og = _gla(seq(qg), seq(kg), seq(vg), seq(la), seq(gr), gla_out_norm[i])
        om = _mla(seq(q), seq(k), seq(v), qk_norm_q[i], qk_norm_k[i])
        h = _tail(h, og.reshape(t, GLA_WIDTH), om.reshape(t, MLA_WIDTH), p[i].reshape(t, PLE_DIM),
                  w_out[i], mlp_norm[i], w_mlp_up[i], w_mlp_down[i], ple_norm[i], w_ple_gate[i],
                  b_ple_gate[i], w_ple_proj[i])
    return h.reshape(b, s, d)
```

```python
import math

import jax
import jax.numpy as jnp
from jax import lax
from jax.experimental import pallas as pl
from jax.experimental.pallas import tpu as pltpu

F32 = jnp.float32
BF16 = jnp.bfloat16

D_MODEL = 1024
CHUNK = 64
PLE_DIM = 256
D_FF = 4 * D_MODEL
EPS = 1e-6
GLA_HEADS = 4
GLA_DK = 64
GLA_DV = 128
GLA_GATE_RANK = 16
GLA_TAU = 16.0
GLA_QK = GLA_HEADS * GLA_DK
GLA_WIDTH = GLA_HEADS * GLA_DV
MLA_HEADS = 8
MLA_NOPE = 64
MLA_ROPE = 32
MLA_V = 64
MLA_QK = MLA_NOPE + MLA_ROPE
MLA_Q_RANK = 256
MLA_KV_RANK = 128
MLA_WIDTH = MLA_HEADS * MLA_V
ROPE_THETA = 10000.0
LOG2E = math.log2(math.e)
IN_SPLITS = (GLA_QK, GLA_QK, GLA_WIDTH, GLA_GATE_RANK, GLA_WIDTH, MLA_Q_RANK, MLA_KV_RANK, MLA_ROPE)

LANES = 128
SUBLANES = 8
HALF_ROPE = MLA_ROPE // 2
HALF_NOPE = MLA_NOPE // 2
HEAD_PAD = MLA_HEADS * LANES
D_IN_PAD = 2048
W_IN_BLOCK = 256
VMEM_LIMIT_BYTES = 56 * 1024 * 1024

TM_PROJ = 512
TM_TAIL = 512
FF_CHUNK = 1024
TQ = 512
TK = 256
MLA_UNROLL = 16
MLA_UNROLL_TWO_PASS = 4
ROPE_ROWS = 64
GLA_SEQS_PER_STEP = (8, 4, 2, 1)
GLA_ROWS_PER_STEP = 256
GLA_FAST_MIN_LOG_DECAY = -80.0
MASKED_LOGIT = -1e30
MLA_MAX_SHIFT = 40

ROW_FREQ16, ROW_MASK96, ROW_KRMASK, ROW_GQ, ROW_GQ_SW, ROW_GK, ROW_GK_SW = range(7)


def _const_spec(shape):
    return pl.BlockSpec(shape, lambda *_: (0,) * len(shape), pipeline_mode=pl.Buffered(1))


def _rms_scale(v, n):
    return lax.rsqrt(jnp.sum(v * v, axis=-1, keepdims=True) * (1.0 / n) + EPS)


def _proj_kernel(x_ref, pos_ref, g_attn_ref, w_in_ref, w2_ref, b2_ref, gqn_ref, w_uq_ref,
                 gkvn_ref, w_uk_ref, w_uv_ref, tab_ref,
                 qg_ref, kg_ref, vg_ref, la_ref, gr_ref, q_ref, k_ref, v_ref,
                 qs_ref, ks_ref, slot_ref):
    @pl.when(pl.program_id(0) == 0)
    def _():
        qs_ref[...] = jnp.zeros_like(qs_ref)
        ks_ref[...] = jnp.zeros_like(ks_ref)
        slot_ref[...] = jnp.zeros_like(slot_ref)

    x = x_ref[...]
    xn = (x * _rms_scale(x, D_MODEL) * g_attn_ref[...]).astype(BF16)

    tab = lambda r: tab_ref[r:r + 1, :]
    mask96, krmask = tab(ROW_MASK96), tab(ROW_KRMASK)
    gq, gq_sw = tab(ROW_GQ) * LOG2E, tab(ROW_GQ_SW) * LOG2E
    gk, gk_sw = tab(ROW_GK) * math.sqrt(MLA_QK), tab(ROW_GK_SW) * math.sqrt(MLA_QK)
    half = LANES // 2

    lane = lax.broadcasted_iota(jnp.int32, (1, LANES), 1)
    groups = ROPE_ROWS // SUBLANES
    assert groups * HALF_ROPE == LANES
    x1_lanes = (lane >= HALF_NOPE) & (lane < HALF_NOPE + HALF_ROPE)
    x2_lanes = (lane >= HALF_NOPE + HALF_ROPE) & (lane < HALF_NOPE + MLA_ROPE)

    def rope_tables(rb):
        ang = pos_ref[rb].astype(F32) * tab(ROW_FREQ16)
        cos_p, sin_p = jnp.cos(ang), jnp.sin(ang)
        cos, sin = [], []
        for a in range(groups):
            s1 = (HALF_NOPE - HALF_ROPE * a) % LANES
            s2 = (HALF_NOPE + HALF_ROPE - HALF_ROPE * a) % LANES
            c1, c2 = pltpu.roll(cos_p, s1, 1), pltpu.roll(cos_p, s2, 1)
            n1, n2 = pltpu.roll(sin_p, s1, 1), pltpu.roll(sin_p, s2, 1)
            cos.append(jnp.where(x1_lanes, c1, jnp.where(x2_lanes, c2, mask96)))
            sin.append(jnp.where(x1_lanes, -n1, jnp.where(x2_lanes, n2, 0.0)))
        return jnp.concatenate(cos, axis=0), jnp.concatenate(sin, axis=0)

    sum96 = (lax.broadcasted_iota(jnp.int32, (LANES, LANES), 0) < MLA_QK).astype(BF16)

    def head_rows(rb):
        rows = pl.ds(rb * ROPE_ROWS, ROPE_ROWS)
        cos, sin = rope_tables(rb)
        kr = slot_ref[rows, :] * krmask
        kr_rot = pltpu.roll(kr, half, 1) * (sin * gk_sw)
        cos_q, sin_q, cos_k = cos * gq, sin * gq_sw, cos * gk
        slabs = []
        for h in range(MLA_HEADS):
            sl = slice(h * LANES, (h + 1) * LANES)
            slabs += [qs_ref[rows, sl], ks_ref[rows, sl] + kr]
        squares = jnp.concatenate([(v * v).astype(BF16) for v in slabs], axis=0)
        rms = lax.rsqrt(jnp.dot(squares, sum96, preferred_element_type=F32) + MLA_QK * EPS)
        for h in range(MLA_HEADS):
            sl = slice(h * LANES, (h + 1) * LANES)
            qh, kh = slabs[2 * h], slabs[2 * h + 1]
            rq = rms[(2 * h) * ROPE_ROWS:(2 * h + 1) * ROPE_ROWS]
            rk = rms[(2 * h + 1) * ROPE_ROWS:(2 * h + 2) * ROPE_ROWS]
            q_ref[rows, sl] = ((qh * cos_q + pltpu.roll(qh, half, 1) * sin_q) * rq).astype(BF16)
            k_ref[rows, sl] = ((kh * cos_k + kr_rot) * rk).astype(BF16)

    n_blocks = w_in_ref.shape[0]
    assert x_ref.shape[0] // ROPE_ROWS == n_blocks == 8
    cq = ckv = slot = None
    for r in range(n_blocks):
        z = jnp.dot(xn, w_in_ref[r], preferred_element_type=F32)
        head_rows(r)
        half_cols = slice((r % 2) * W_IN_BLOCK, (r % 2 + 1) * W_IN_BLOCK)
        if r == 0:
            qg_ref[...] = (z * (GLA_DK ** -0.5)).astype(BF16)
        elif r == 1:
            kg_ref[...] = z.astype(BF16)
        elif r in (2, 3):
            vg_ref[:, half_cols] = z.astype(BF16)
        elif r in (4, 5):
            gr_ref[:, half_cols] = (z * jax.nn.sigmoid(z)).astype(BF16)
        elif r == 6:
            cq = z
        else:
            ckv, slot = z[:, :MLA_KV_RANK], z[:, MLA_KV_RANK:]
    slot_ref[...] = slot

    xg = jnp.dot(slot.astype(BF16), w2_ref[...], preferred_element_type=F32) + b2_ref[...]
    la_ref[...] = (jnp.minimum(xg, 0.0) - jnp.log(1.0 + jnp.exp(-jnp.abs(xg)))) * (1.0 / GLA_TAU)

    cqn = (cq * _rms_scale(cq, MLA_Q_RANK) * gqn_ref[...]).astype(BF16)
    qs_ref[...] = jnp.dot(cqn, w_uq_ref[...], preferred_element_type=F32)
    ckvn = (ckv * _rms_scale(ckv, MLA_KV_RANK) * gkvn_ref[...]).astype(BF16)
    ks_ref[...] = jnp.dot(ckvn, w_uk_ref[...], preferred_element_type=F32)
    v_ref[...] = jnp.dot(ckvn, w_uv_ref[...], preferred_element_type=F32).astype(BF16)


def _head_layout(nope, rope):
    x1, x2 = rope[..., :HALF_ROPE], rope[..., HALF_ROPE:]
    return jnp.concatenate([nope[..., :HALF_NOPE], x1, x2, nope[..., HALF_NOPE:], x2, x1], axis=-1)


def _projection(x2, positions, attn_norm, w_in, gate_w2, gate_b, q_norm, w_uq, kv_norm, w_ukv,
                qk_norm_q, qk_norm_k):
    t = x2.shape[0]
    tm = TM_PROJ
    assert t % tm == 0

    gq, gk, gv, g_low, g_r, c_q, c_kv, k_r = jnp.split(w_in, list(_cumsum(IN_SPLITS))[:-1], axis=1)
    zeros = lambda n: jnp.zeros((D_MODEL, n), w_in.dtype)
    kr_sw = jnp.concatenate([k_r[:, HALF_ROPE:], k_r[:, :HALF_ROPE]], axis=1)
    slot = jnp.concatenate([g_low, zeros(HALF_NOPE - GLA_GATE_RANK), k_r, zeros(HALF_NOPE), kr_sw], axis=1)
    w_in_l = jnp.concatenate([gq, gk, gv, g_r, c_q, c_kv, slot], axis=1).astype(BF16)
    assert w_in_l.shape == (D_MODEL, D_IN_PAD)

    w2_l = jnp.zeros((LANES, GLA_QK), F32).at[:GLA_GATE_RANK].set(gate_w2).astype(BF16)

    w_uq_h = w_uq.reshape(MLA_Q_RANK, MLA_HEADS, MLA_QK)
    w_uq_l = _head_layout(w_uq_h[..., :MLA_NOPE], w_uq_h[..., MLA_NOPE:]).reshape(MLA_Q_RANK, HEAD_PAD)
    w_ukv_h = w_ukv.reshape(MLA_KV_RANK, MLA_HEADS, MLA_NOPE + MLA_V)
    w_uk_l = _head_layout(w_ukv_h[..., :MLA_NOPE],
                          jnp.zeros((MLA_KV_RANK, MLA_HEADS, MLA_ROPE), F32)).reshape(MLA_KV_RANK, HEAD_PAD)
    w_uv = w_ukv_h[..., MLA_NOPE:].reshape(MLA_KV_RANK, MLA_WIDTH)
    gq_l = _head_layout(qk_norm_q[:MLA_NOPE], qk_norm_q[MLA_NOPE:])
    gk_l = _head_layout(qk_norm_k[:MLA_NOPE], qk_norm_k[MLA_NOPE:])

    inv_freq = ROPE_THETA ** (-jnp.arange(0, MLA_ROPE, 2, dtype=F32) / MLA_ROPE)
    z32, o32 = jnp.zeros((HALF_NOPE,), F32), jnp.ones((MLA_ROPE,), F32)
    tab = jnp.zeros((SUBLANES, LANES), F32)
    tab = tab.at[ROW_FREQ16].set(jnp.tile(inv_freq, LANES // HALF_ROPE))
    tab = tab.at[ROW_MASK96].set(jnp.concatenate([jnp.ones((MLA_QK,), F32), z32]))
    tab = tab.at[ROW_KRMASK].set(jnp.concatenate([z32, o32, z32, o32]))
    tab = tab.at[ROW_GQ].set(gq_l).at[ROW_GQ_SW].set(jnp.roll(gq_l, LANES // 2))
    tab = tab.at[ROW_GK].set(gk_l).at[ROW_GK_SW].set(jnp.roll(gk_l, LANES // 2))

    groups = ROPE_ROWS // SUBLANES
    pos_packed = jnp.broadcast_to(
        positions.reshape(t // ROPE_ROWS, groups, SUBLANES).transpose(0, 2, 1)[..., None],
        (t // ROPE_ROWS, SUBLANES, groups, HALF_ROPE)).reshape(t // ROPE_ROWS, SUBLANES, LANES)

    n_blocks = D_IN_PAD // W_IN_BLOCK
    w_in_l = w_in_l.reshape(D_MODEL, n_blocks, W_IN_BLOCK).transpose(1, 0, 2)

    n = t // tm
    tile = lambda w: pl.BlockSpec((tm, w), lambda i: (jnp.minimum(i, n - 1), 0))
    prev = lambda w: pl.BlockSpec((tm, w), lambda i: (jnp.maximum(i - 1, 0), 0))
    out_widths = (GLA_QK, GLA_QK, GLA_WIDTH, GLA_QK, GLA_WIDTH, HEAD_PAD, HEAD_PAD, MLA_WIDTH)
    out_dtypes = (BF16, BF16, BF16, F32, BF16, BF16, BF16, BF16)
    out_specs = [tile(w) for w in out_widths]
    out_specs[5], out_specs[6] = prev(HEAD_PAD), prev(HEAD_PAD)
    return pl.pallas_call(
        _proj_kernel,
        grid=(n + 1,),
        in_specs=[
            tile(D_MODEL),
            pl.BlockSpec((tm // ROPE_ROWS, SUBLANES, LANES), lambda i: (jnp.maximum(i - 1, 0), 0, 0)),
            _const_spec((1, D_MODEL)), _const_spec((n_blocks, D_MODEL, W_IN_BLOCK)),
            _const_spec((LANES, GLA_QK)), _const_spec((1, GLA_QK)),
            _const_spec((1, MLA_Q_RANK)), _const_spec((MLA_Q_RANK, HEAD_PAD)),
            _const_spec((1, MLA_KV_RANK)), _const_spec((MLA_KV_RANK, HEAD_PAD)),
            _const_spec((MLA_KV_RANK, MLA_WIDTH)),
            _const_spec((8, LANES)),
        ],
        out_specs=out_specs,
        out_shape=[jax.ShapeDtypeStruct((t, w), d) for w, d in zip(out_widths, out_dtypes)],
        scratch_shapes=[pltpu.VMEM((tm, HEAD_PAD), F32), pltpu.VMEM((tm, HEAD_PAD), F32),
                        pltpu.VMEM((tm, LANES), F32)],
        compiler_params=pltpu.CompilerParams(
            dimension_semantics=("arbitrary",), vmem_limit_bytes=VMEM_LIMIT_BYTES),
        name="proj",
    )(x2, pos_packed, attn_norm.reshape(1, D_MODEL), w_in_l, w2_l, gate_b.reshape(1, GLA_QK),
      q_norm.reshape(1, MLA_Q_RANK), w_uq_l.astype(BF16), kv_norm.reshape(1, MLA_KV_RANK),
      w_uk_l.astype(BF16), w_uv.astype(BF16), tab)


def _cumsum(sizes):
    total = 0
    for s in sizes:
        total += s
        yield total


def _gla_kernel(q_ref, k_ref, v_ref, la_ref, gr_ref, gon_ref, o_ref, st_ref, kf_ref, b_ref):
    nseq, rows_per_step = q_ref.shape[0], q_ref.shape[1]

    @pl.when(pl.program_id(1) == 0)
    def _():
        st_ref[...] = jnp.zeros_like(st_ref)

    row = lax.broadcasted_iota(jnp.int32, (CHUNK, CHUNK), 0)
    col = lax.broadcasted_iota(jnp.int32, (CHUNK, CHUNK), 1)
    causal = col <= row
    lane = lax.broadcasted_iota(jnp.int32, (1, LANES), 1)
    lane_q = lax.broadcasted_iota(jnp.int32, (1, GLA_QK), 1)
    head_mask = [(lane // GLA_DK) == e for e in range(2)]
    gon = gon_ref[...]
    nt = (((1,), (1,)), ((), ()))
    tn = (((0,), (0,)), ((), ()))

    def exact_scores(q, k, b):
        kf_ref[...] = k
        b_ref[...] = b

        def key_row(s, sc):
            k_s = kf_ref[pl.ds(s, 1), :]
            b_s = b_ref[pl.ds(s, 1), :]
            prod = q * k_s * jnp.exp(jnp.minimum(b - b_s, 0.0))
            out = []
            for h in range(GLA_HEADS):
                hm = (lane_q // GLA_DK) == h
                col_h = jnp.sum(jnp.where(hm, prod, 0.0), axis=-1, keepdims=True)
                out.append(jnp.where(col == s, col_h, sc[h]))
            return tuple(out)

        zero = jnp.zeros((CHUNK, CHUNK), F32)
        return lax.fori_loop(0, CHUNK, key_row, (zero,) * GLA_HEADS)

    pair = lambda h: slice((h // 2) * LANES, (h // 2 + 1) * LANES)
    vsl = lambda h: slice(h * GLA_DV, (h + 1) * GLA_DV)
    heads = range(GLA_HEADS)

    def row_cumsum(x):
        row_id = lax.broadcasted_iota(jnp.int32, x.shape, 0)
        step = 1
        while step < CHUNK:
            if step < SUBLANES:
                moved = jnp.where(row_id >= step, pltpu.roll(x, step, 0), 0.0)
            else:
                moved = jnp.concatenate([jnp.zeros((step, x.shape[1]), x.dtype), x[:-step]], axis=0)
            x = x + moved
            step *= 2
        return x

    def chunk(seqs, c, factorised):
        rows = pl.ds(pl.multiple_of(c * CHUNK, CHUNK), CHUNK)
        q, k, b, a_h, kt, kd, eb_last, scores, o_inter = {}, {}, {}, {}, {}, {}, {}, {}, {}
        for n in seqs:
            b[n] = row_cumsum(la_ref[n, rows, :])
            b_last = b[n][CHUNK - 1:CHUNK, :]
            q[n] = q_ref[n, rows, :].astype(F32)
            k[n] = k_ref[n, rows, :].astype(F32)
            qt = q[n] * jnp.exp(b[n])
            eb_last[n] = jnp.exp(b_last)
            if factorised:
                k_grown = k[n] * jnp.exp(-b[n])
                kt[n] = k_grown.astype(BF16)
                kd[n] = k_grown * eb_last[n]
            else:
                kd[n] = k[n] * jnp.exp(b_last - b[n])
            a_h[n] = [jnp.where(head_mask[h % 2], qt[:, pair(h)], 0.0).astype(BF16) for h in heads]
        v, st, o = {}, {}, {}
        for n in seqs:
            for h in heads:
                st[n, h] = st_ref[n, h]
                if factorised:
                    both = lax.dot_general(
                        a_h[n][h], jnp.concatenate([st[n, h].astype(BF16), kt[n][:, pair(h)]], axis=0), nt,
                        preferred_element_type=F32)
                    o_inter[n, h], scores[n, h] = both[:, :GLA_DV], both[:, GLA_DV:]
                else:
                    o_inter[n, h] = lax.dot_general(a_h[n][h], st[n, h].astype(BF16), nt,
                                                    preferred_element_type=F32)
            if not factorised:
                for h, sc in enumerate(exact_scores(q[n], k[n], b[n])):
                    scores[n, h] = sc
        for n in seqs:
            for h in heads:
                sc = jnp.where(causal, scores[n, h], 0.0).astype(BF16)
                v[n, h] = v_ref[n, rows, vsl(h)]
                o[n, h] = o_inter[n, h] + jnp.dot(sc, v[n, h], preferred_element_type=F32)
        for n in seqs:
            for h in heads:
                kd_h = jnp.where(head_mask[h % 2], kd[n][:, pair(h)], 0.0).astype(BF16)
                upd = lax.dot_general(v[n, h], kd_h, tn, preferred_element_type=F32)
                st_ref[n, h] = st[n, h] * eb_last[n][:, pair(h)] + upd
        for n in seqs:
            for h in heads:
                on = o[n, h] * _rms_scale(o[n, h], GLA_DV) * gon
                o_ref[n, rows, vsl(h)] = (on * gr_ref[n, rows, vsl(h)].astype(F32)).astype(BF16)

    def factorised_chunks():
        def body(c, carry):
            chunk(range(nseq), c, True)
            return carry
        lax.fori_loop(0, rows_per_step // CHUNK, body, 0)

    def exact_chunks():
        def body(c, carry):
            for n in range(nseq):
                chunk([n], c, False)
            return carry
        lax.fori_loop(0, rows_per_step // CHUNK, body, 0)

    factorisable = jnp.min(la_ref[...]) >= GLA_FAST_MIN_LOG_DECAY / CHUNK
    lax.cond(factorisable, factorised_chunks, exact_chunks)


def _gla(qg, kg, vg, la, gr, out_norm):
    b, s, _ = qg.shape
    nseq = max(n for n in GLA_SEQS_PER_STEP if b % n == 0)
    rows = min(GLA_ROWS_PER_STEP, s)
    assert s % rows == 0 and rows % CHUNK == 0
    blk = lambda w: pl.BlockSpec((nseq, rows, w), lambda i, j: (i, j, 0))
    return pl.pallas_call(
        _gla_kernel,
        grid=(b // nseq, s // rows),
        in_specs=[blk(GLA_QK), blk(GLA_QK), blk(GLA_WIDTH), blk(GLA_QK), blk(GLA_WIDTH),
                  _const_spec((1, GLA_DV))],
        out_specs=blk(GLA_WIDTH),
        out_shape=jax.ShapeDtypeStruct((b, s, GLA_WIDTH), BF16),
        scratch_shapes=[pltpu.VMEM((nseq, GLA_HEADS, GLA_DV, LANES), F32),
                        pltpu.VMEM((CHUNK, GLA_QK), F32),
                        pltpu.VMEM((CHUNK, GLA_QK), F32)],
        compiler_params=pltpu.CompilerParams(
            dimension_semantics=("arbitrary", "arbitrary"), vmem_limit_bytes=VMEM_LIMIT_BYTES),
        name="gla",
    )(qg, kg, vg, la, gr, out_norm.reshape(1, GLA_DV))


def _mla_tiles(n_q):
    assert TQ == 2 * TK
    first = [(a, 2 * a, 0) for a in range(n_q)]
    full = [(a, j, TQ // CHUNK) for a in range(n_q) for j in range(2 * a)]
    second = [(a, 2 * a + 1, -(TK // CHUNK)) for a in range(n_q)]
    return tuple(zip(*(first + full + second)))


def _mla_kernel(ti_ref, tj_ref, toff_ref, bound_ref, q_ref, k_ref, v_ref, o_ref,
                s_ref, m_ref, acc_ref, v1_ref):
    seq = q_ref.shape[1]
    n_q = seq // TQ
    n_tiles = s_ref.shape[1]
    nt = (((1,), (1,)), ((), ()))
    heads = range(2)
    lane = lax.broadcasted_iota(jnp.int32, (1, LANES), 1)
    own = [(lane // MLA_V) == e for e in heads]
    unroll = lambda count, limit=MLA_UNROLL: max(u for u in range(1, limit + 1) if count % u == 0)

    def q_rows(i):
        return pl.ds(pl.multiple_of(i * TQ, TQ), TQ)

    def k_rows(j):
        return pl.ds(pl.multiple_of(j * TK, TK), TK)

    def masked_logits(e, t):
        hsl = slice(e * LANES, (e + 1) * LANES)
        s = lax.dot_general(q_ref[0, q_rows(ti_ref[t]), hsl], k_ref[0, k_rows(tj_ref[t]), hsl], nt,
                            preferred_element_type=F32)
        row_chunk = lax.broadcasted_iota(jnp.int32, (TQ, TK), 0) // CHUNK
        col_chunk = lax.broadcasted_iota(jnp.int32, (TQ, TK), 1) // CHUNK
        return jnp.where(col_chunk <= row_chunk + toff_ref[t], s, MASKED_LOGIT)

    def lane_fold_max(a):
        out = a[:, :LANES]
        for c in range(1, TK // LANES):
            out = jnp.maximum(out, a[:, c * LANES:(c + 1) * LANES])
        return out

    v = v_ref[0]
    for e in heads:
        v1_ref[e] = jnp.where(own[e], v, 1.0).astype(BF16)

    def exp_shifted(s, shift):
        return jnp.concatenate(
            [jnp.exp2(s[:, c * LANES:(c + 1) * LANES] - shift) for c in range(TK // LANES)],
            axis=1).astype(BF16)

    def accumulate(e, t, s, shift):
        acc_ref[e, ti_ref[t]] += jnp.dot(exp_shifted(s, shift), v1_ref[e, k_rows(tj_ref[t]), :],
                                         preferred_element_type=F32)

    def clear(i, carry):
        for e in heads:
            acc_ref[e, i] = jnp.zeros((TQ, LANES), F32)
        return carry

    def finish(i, carry):
        acc0, acc1 = acc_ref[0, i], acc_ref[1, i]
        num = jnp.where(own[0], acc0, acc1)
        den = pltpu.roll(jnp.where(own[0], acc1, acc0), LANES // 2, 1)
        o_ref[0, q_rows(i), :] = (num / den).astype(BF16)
        return carry

    def single_pass():
        lax.fori_loop(0, n_q, clear, 0)
        shift = bound_ref[0].astype(F32)

        def tile(t, carry):
            for e in heads:
                accumulate(e, t, masked_logits(e, t), shift)
            return carry

        lax.fori_loop(0, n_tiles - n_q, tile, 0, unroll=unroll(n_tiles - n_q))

        def lower_half_tile(t, carry):
            i = ti_ref[t]
            lower = pl.ds(pl.multiple_of(i * TQ + TK, TK), TK)
            row_chunk = lax.broadcasted_iota(jnp.int32, (TK, TK), 0) // CHUNK
            col_chunk = lax.broadcasted_iota(jnp.int32, (TK, TK), 1) // CHUNK
            for e in heads:
                hsl = slice(e * LANES, (e + 1) * LANES)
                s = lax.dot_general(q_ref[0, lower, hsl], k_ref[0, k_rows(tj_ref[t]), hsl], nt,
                                    preferred_element_type=F32)
                p = exp_shifted(jnp.where(col_chunk <= row_chunk, s, MASKED_LOGIT), shift)
                acc_ref[e, i, TK:, :] += jnp.dot(p, v1_ref[e, k_rows(tj_ref[t]), :],
                                                 preferred_element_type=F32)
            return finish(i, carry)

        lax.fori_loop(n_tiles - n_q, n_tiles, lower_half_tile, 0, unroll=unroll(n_q))

    def two_pass():
        def first_tile(t, carry):
            for e in heads:
                s = masked_logits(e, t)
                s_ref[e, t] = s
                m_ref[e, t] = lane_fold_max(s)
            return clear(t, carry)

        lax.fori_loop(0, n_q, first_tile, 0, unroll=unroll(n_q, MLA_UNROLL_TWO_PASS))

        def pass1(t, carry):
            i = ti_ref[t]
            for e in heads:
                s = masked_logits(e, t)
                s_ref[e, t] = s
                m_ref[e, i] = jnp.maximum(m_ref[e, i], lane_fold_max(s))
            return carry

        lax.fori_loop(n_q, n_tiles, pass1, 0, unroll=unroll(n_tiles - n_q, MLA_UNROLL_TWO_PASS))

        def row_max(i, carry):
            for e in heads:
                m_ref[e, i] = jnp.broadcast_to(jnp.max(m_ref[e, i], axis=-1, keepdims=True), (TQ, LANES))
            return carry

        lax.fori_loop(0, n_q, row_max, 0)

        def pass2(t, carry):
            for e in heads:
                accumulate(e, t, s_ref[e, t], m_ref[e, ti_ref[t]])
            return carry

        lax.fori_loop(0, n_tiles, pass2, 0, unroll=unroll(n_tiles, MLA_UNROLL_TWO_PASS))
        lax.fori_loop(0, n_q, finish, 0)

    lax.cond(bound_ref[0] <= MLA_MAX_SHIFT, single_pass, two_pass)


def _mla(q, k, v, gain_q, gain_k):
    b, s, _ = q.shape
    assert s % TQ == 0 and TQ % TK == 0
    n_q = s // TQ
    ti, tj, toff = _mla_tiles(n_q)
    bound = jnp.ceil(1.02 * LOG2E * math.sqrt(MLA_QK) * jnp.max(jnp.abs(gain_q)) * jnp.max(jnp.abs(gain_k)))
    bound = jnp.minimum(bound, 2.0 * MLA_MAX_SHIFT).astype(jnp.int32).reshape(1)
    blk = lambda w: pl.BlockSpec((1, s, w), lambda i, p, *_: (i, 0, p))
    return pl.pallas_call(
        _mla_kernel,
        grid_spec=pltpu.PrefetchScalarGridSpec(
            num_scalar_prefetch=4,
            grid=(b, MLA_HEADS // 2),
            in_specs=[blk(2 * LANES), blk(2 * LANES), blk(LANES)],
            out_specs=blk(LANES),
            scratch_shapes=[pltpu.VMEM((2, len(ti), TQ, TK), F32),
                            pltpu.VMEM((2, n_q, TQ, LANES), F32),
                            pltpu.VMEM((2, n_q, TQ, LANES), F32),
                            pltpu.VMEM((2, s, LANES), BF16)]),
        out_shape=jax.ShapeDtypeStruct((b, s, MLA_WIDTH), BF16),
        compiler_params=pltpu.CompilerParams(
            dimension_semantics=("arbitrary", "arbitrary"),
            vmem_limit_bytes=VMEM_LIMIT_BYTES),
        name="mla",
    )(jnp.asarray(ti, jnp.int32), jnp.asarray(tj, jnp.int32), jnp.asarray(toff, jnp.int32), bound,
      q, k, v)


def _tail_kernel(x_ref, og_ref, om_ref, p_ref, w_out_ref, g_mlp_ref, w_up_ref, w_down_ref,
                 g_ple_ref, w_gate_ref, b_gate_ref, w_pp_ref, o_ref):
    h = (x_ref[...]
         + jnp.dot(og_ref[...], w_out_ref[:GLA_WIDTH, :], preferred_element_type=F32)
         + jnp.dot(om_ref[...], w_out_ref[GLA_WIDTH:, :], preferred_element_type=F32))
    m = (h * _rms_scale(h, D_MODEL) * g_mlp_ref[...]).astype(BF16)
    mlp = None
    for c in range(D_FF // FF_CHUNK):
        cols = slice(c * FF_CHUNK, (c + 1) * FF_CHUNK)
        u = jnp.maximum(jnp.dot(m, w_up_ref[:, cols], preferred_element_type=F32), 0.0)
        d = jnp.dot((u * u).astype(BF16), w_down_ref[cols, :], preferred_element_type=F32)
        mlp = d if mlp is None else mlp + d
    h = h + mlp
    g = (h * _rms_scale(h, D_MODEL) * g_ple_ref[...]).astype(BF16)
    gate = jax.nn.sigmoid(jnp.dot(g, w_gate_ref[...], preferred_element_type=F32) + b_gate_ref[...])
    pp = jnp.dot(p_ref[...].astype(BF16), w_pp_ref[...], preferred_element_type=F32)
    o_ref[...] = h + pp * gate


def _tail(x2, og, om, p2, w_out, mlp_norm, w_up, w_down, ple_norm, w_gate, b_gate, w_pp):
    t = x2.shape[0]
    tm = TM_TAIL
    assert t % tm == 0
    row = lambda w: pl.BlockSpec((tm, w), lambda i: (i, 0))
    return pl.pallas_call(
        _tail_kernel,
        grid=(t // tm,),
        in_specs=[row(D_MODEL), row(GLA_WIDTH), row(MLA_WIDTH), row(PLE_DIM),
                  _const_spec((D_MODEL, D_MODEL)), _const_spec((1, D_MODEL)),
                  _const_spec((D_MODEL, D_FF)), _const_spec((D_FF, D_MODEL)),
                  _const_spec((1, D_MODEL)), _const_spec((D_MODEL, D_MODEL)),
                  _const_spec((1, D_MODEL)), _const_spec((PLE_DIM, D_MODEL))],
        out_specs=row(D_MODEL),
        out_shape=jax.ShapeDtypeStruct((t, D_MODEL), F32),
        compiler_params=pltpu.CompilerParams(
            dimension_semantics=("arbitrary",), vmem_limit_bytes=VMEM_LIMIT_BYTES),
        name="tail",
    )(x2, og, om, p2, w_out.astype(BF16), mlp_norm.reshape(1, D_MODEL), w_up.astype(BF16),
      w_down.astype(BF16), ple_norm.reshape(1, D_MODEL), w_gate.astype(BF16),
      b_gate.reshape(1, D_MODEL), w_pp.astype(BF16))


def kernel(x, p, positions, attn_norm, w_in, gla_gate_w2, gla_gate_b, gla_out_norm, mla_q_norm,
           mla_w_uq, mla_kv_norm, mla_w_ukv, qk_norm_q, qk_norm_k, w_out, mlp_norm, w_mlp_up,
           w_mlp_down, ple_norm, w_ple_gate, b_ple_gate, w_ple_proj):
    b, s, d = x.shape
    depth = w_in.shape[0]
    t = b * s
    h = x.reshape(t, d)
    for i in range(depth):
        qg, kg, vg, la, gr, q, k, v = _projection(
            h, positions, attn_norm[i], w_in[i], gla_gate_w2[i], gla_gate_b[i], mla_q_norm[i],
            mla_w_uq[i], mla_kv_norm[i], mla_w_ukv[i], qk_norm_q[i], qk_norm_k[i])
        seq = lambda a: a.reshape(b, s, a.shape[-1])
        og = _gla(seq(qg), seq(kg), seq(vg), seq(la), seq(gr), gla_out_norm[i])
        om = _mla(seq(q), seq(k), seq(v), qk_norm_q[i], qk_norm_k[i])
        h = _tail(h, og.reshape(t, GLA_WIDTH), om.reshape(t, MLA_WIDTH), p[i].reshape(t, PLE_DIM),
                  w_out[i], mlp_norm[i], w_mlp_up[i], w_mlp_down[i], ple_norm[i], w_ple_gate[i],
                  b_ple_gate[i], w_ple_proj[i])
    return h.reshape(b, s, d)
```

```python
import math

import jax
import jax.numpy as jnp
from jax import lax
from jax.experimental import pallas as pl
from jax.experimental.pallas import tpu as pltpu

F32 = jnp.float32
BF16 = jnp.bfloat16

D_MODEL = 1024
CHUNK = 64
PLE_DIM = 256
D_FF = 4 * D_MODEL
EPS = 1e-6
GLA_HEADS = 4
GLA_DK = 64
GLA_DV = 128
GLA_GATE_RANK = 16
GLA_TAU = 16.0
GLA_QK = GLA_HEADS * GLA_DK
GLA_WIDTH = GLA_HEADS * GLA_DV
MLA_HEADS = 8
MLA_NOPE = 64
MLA_ROPE = 32
MLA_V = 64
MLA_QK = MLA_NOPE + MLA_ROPE
MLA_Q_RANK = 256
MLA_KV_RANK = 128
MLA_WIDTH = MLA_HEADS * MLA_V
ROPE_THETA = 10000.0
LOG2E = math.log2(math.e)
IN_SPLITS = (GLA_QK, GLA_QK, GLA_WIDTH, GLA_GATE_RANK, GLA_WIDTH, MLA_Q_RANK, MLA_KV_RANK, MLA_ROPE)

LANES = 128
SUBLANES = 8
HALF_ROPE = MLA_ROPE // 2
HALF_NOPE = MLA_NOPE // 2
HEAD_PAD = MLA_HEADS * LANES
D_IN_PAD = 2048
W_IN_BLOCK = 256
VMEM_LIMIT_BYTES = 56 * 1024 * 1024

TM_PROJ = 512
TM_TAIL = 1024
TM_TAIL_PART = 512
FF_CHUNK = 1024
TQ = 512
TK = 256
MLA_UNROLL = 16
MLA_UNROLL_TWO_PASS = 4
ROPE_ROWS = 64
GLA_SEQS_PER_STEP = (8, 4, 2, 1)
GLA_ROWS_PER_STEP = 256
GLA_FAST_MIN_LOG_DECAY = -80.0
MASKED_LOGIT = -1e30
MLA_MAX_SHIFT = 40

ROW_FREQ16, ROW_MASK96, ROW_KRMASK, ROW_GQ, ROW_GQ_SW, ROW_GK, ROW_GK_SW = range(7)


def _const_spec(shape):
    return pl.BlockSpec(shape, lambda *_: (0,) * len(shape), pipeline_mode=pl.Buffered(1))


def _rms_scale(v, n):
    return lax.rsqrt(jnp.sum(v * v, axis=-1, keepdims=True) * (1.0 / n) + EPS)


def _proj_kernel(x_ref, pos_ref, g_attn_ref, w_in_ref, w2_ref, b2_ref, gqn_ref, w_uq_ref,
                 gkvn_ref, w_uk_ref, w_uv_ref, tab_ref,
                 qg_ref, kg_ref, vg_ref, la_ref, gr_ref, q_ref, k_ref, v_ref,
                 qs_ref, ks_ref, slot_ref):
    @pl.when(pl.program_id(0) == 0)
    def _():
        qs_ref[...] = jnp.zeros_like(qs_ref)
        ks_ref[...] = jnp.zeros_like(ks_ref)
        slot_ref[...] = jnp.zeros_like(slot_ref)

    x = x_ref[...]
    xn = (x * _rms_scale(x, D_MODEL) * g_attn_ref[...]).astype(BF16)

    tab = lambda r: tab_ref[r:r + 1, :]
    mask96, krmask = tab(ROW_MASK96), tab(ROW_KRMASK)
    gq, gq_sw = tab(ROW_GQ) * LOG2E, tab(ROW_GQ_SW) * LOG2E
    gk, gk_sw = tab(ROW_GK) * math.sqrt(MLA_QK), tab(ROW_GK_SW) * math.sqrt(MLA_QK)
    half = LANES // 2

    lane = lax.broadcasted_iota(jnp.int32, (1, LANES), 1)
    groups = ROPE_ROWS // SUBLANES
    assert groups * HALF_ROPE == LANES
    x1_lanes = (lane >= HALF_NOPE) & (lane < HALF_NOPE + HALF_ROPE)
    x2_lanes = (lane >= HALF_NOPE + HALF_ROPE) & (lane < HALF_NOPE + MLA_ROPE)

    def rope_tables(rb):
        ang = pos_ref[rb].astype(F32) * tab(ROW_FREQ16)
        cos_p, sin_p = jnp.cos(ang), jnp.sin(ang)
        cos, sin = [], []
        for a in range(groups):
            s1 = (HALF_NOPE - HALF_ROPE * a) % LANES
            s2 = (HALF_NOPE + HALF_ROPE - HALF_ROPE * a) % LANES
            c1, c2 = pltpu.roll(cos_p, s1, 1), pltpu.roll(cos_p, s2, 1)
            n1, n2 = pltpu.roll(sin_p, s1, 1), pltpu.roll(sin_p, s2, 1)
            cos.append(jnp.where(x1_lanes, c1, jnp.where(x2_lanes, c2, mask96)))
            sin.append(jnp.where(x1_lanes, -n1, jnp.where(x2_lanes, n2, 0.0)))
        return jnp.concatenate(cos, axis=0), jnp.concatenate(sin, axis=0)

    sum96 = (lax.broadcasted_iota(jnp.int32, (LANES, LANES), 0) < MLA_QK).astype(BF16)

    def head_rows(rb):
        rows = pl.ds(rb * ROPE_ROWS, ROPE_ROWS)
        cos, sin = rope_tables(rb)
        kr = slot_ref[rows, :] * krmask
        kr_rot = pltpu.roll(kr, half, 1) * (sin * gk_sw)
        cos_q, sin_q, cos_k = cos * gq, sin * gq_sw, cos * gk
        slabs = []
        for h in range(MLA_HEADS):
            sl = slice(h * LANES, (h + 1) * LANES)
            slabs += [qs_ref[rows, sl], ks_ref[rows, sl] + kr]
        squares = jnp.concatenate([(v * v).astype(BF16) for v in slabs], axis=0)
        rms = lax.rsqrt(jnp.dot(squares, sum96, preferred_element_type=F32) + MLA_QK * EPS)
        for h in range(MLA_HEADS):
            sl = slice(h * LANES, (h + 1) * LANES)
            qh, kh = slabs[2 * h], slabs[2 * h + 1]
            rq = rms[(2 * h) * ROPE_ROWS:(2 * h + 1) * ROPE_ROWS]
            rk = rms[(2 * h + 1) * ROPE_ROWS:(2 * h + 2) * ROPE_ROWS]
            q_ref[rows, sl] = ((qh * cos_q + pltpu.roll(qh, half, 1) * sin_q) * rq).astype(BF16)
            k_ref[rows, sl] = ((kh * cos_k + kr_rot) * rk).astype(BF16)

    n_blocks = w_in_ref.shape[0]
    assert x_ref.shape[0] // ROPE_ROWS == n_blocks == 8
    cq = ckv = slot = None
    for r in range(n_blocks):
        z = jnp.dot(xn, w_in_ref[r], preferred_element_type=F32)
        head_rows(r)
        half_cols = slice((r % 2) * W_IN_BLOCK, (r % 2 + 1) * W_IN_BLOCK)
        if r == 0:
            qg_ref[...] = (z * (GLA_DK ** -0.5)).astype(BF16)
        elif r == 1:
            kg_ref[...] = z.astype(BF16)
        elif r in (2, 3):
            vg_ref[:, half_cols] = z.astype(BF16)
        elif r in (4, 5):
            gr_ref[:, half_cols] = (z * jax.nn.sigmoid(z)).astype(BF16)
        elif r == 6:
            cq = z
        else:
            ckv, slot = z[:, :MLA_KV_RANK], z[:, MLA_KV_RANK:]
    slot_ref[...] = slot

    xg = jnp.dot(slot.astype(BF16), w2_ref[...], preferred_element_type=F32) + b2_ref[...]
    la_ref[...] = (jnp.minimum(xg, 0.0) - jnp.log(1.0 + jnp.exp(-jnp.abs(xg)))) * (1.0 / GLA_TAU)

    cqn = (cq * _rms_scale(cq, MLA_Q_RANK) * gqn_ref[...]).astype(BF16)
    qs_ref[...] = jnp.dot(cqn, w_uq_ref[...], preferred_element_type=F32)
    ckvn = (ckv * _rms_scale(ckv, MLA_KV_RANK) * gkvn_ref[...]).astype(BF16)
    ks_ref[...] = jnp.dot(ckvn, w_uk_ref[...], preferred_element_type=F32)
    v_ref[...] = jnp.dot(ckvn, w_uv_ref[...], preferred_element_type=F32).astype(BF16)


def _head_layout(nope, rope):
    x1, x2 = rope[..., :HALF_ROPE], rope[..., HALF_ROPE:]
    return jnp.concatenate([nope[..., :HALF_NOPE], x1, x2, nope[..., HALF_NOPE:], x2, x1], axis=-1)


def _projection(x2, positions, attn_norm, w_in, gate_w2, gate_b, q_norm, w_uq, kv_norm, w_ukv,
                qk_norm_q, qk_norm_k):
    t = x2.shape[0]
    tm = TM_PROJ
    assert t % tm == 0

    gq, gk, gv, g_low, g_r, c_q, c_kv, k_r = jnp.split(w_in, list(_cumsum(IN_SPLITS))[:-1], axis=1)
    zeros = lambda n: jnp.zeros((D_MODEL, n), w_in.dtype)
    kr_sw = jnp.concatenate([k_r[:, HALF_ROPE:], k_r[:, :HALF_ROPE]], axis=1)
    slot = jnp.concatenate([g_low, zeros(HALF_NOPE - GLA_GATE_RANK), k_r, zeros(HALF_NOPE), kr_sw], axis=1)
    w_in_l = jnp.concatenate([gq, gk, gv, g_r, c_q, c_kv, slot], axis=1).astype(BF16)
    assert w_in_l.shape == (D_MODEL, D_IN_PAD)

    w2_l = jnp.zeros((LANES, GLA_QK), F32).at[:GLA_GATE_RANK].set(gate_w2).astype(BF16)

    w_uq_h = w_uq.reshape(MLA_Q_RANK, MLA_HEADS, MLA_QK)
    w_uq_l = _head_layout(w_uq_h[..., :MLA_NOPE], w_uq_h[..., MLA_NOPE:]).reshape(MLA_Q_RANK, HEAD_PAD)
    w_ukv_h = w_ukv.reshape(MLA_KV_RANK, MLA_HEADS, MLA_NOPE + MLA_V)
    w_uk_l = _head_layout(w_ukv_h[..., :MLA_NOPE],
                          jnp.zeros((MLA_KV_RANK, MLA_HEADS, MLA_ROPE), F32)).reshape(MLA_KV_RANK, HEAD_PAD)
    w_uv = w_ukv_h[..., MLA_NOPE:].reshape(MLA_KV_RANK, MLA_WIDTH)
    gq_l = _head_layout(qk_norm_q[:MLA_NOPE], qk_norm_q[MLA_NOPE:])
    gk_l = _head_layout(qk_norm_k[:MLA_NOPE], qk_norm_k[MLA_NOPE:])

    inv_freq = ROPE_THETA ** (-jnp.arange(0, MLA_ROPE, 2, dtype=F32) / MLA_ROPE)
    z32, o32 = jnp.zeros((HALF_NOPE,), F32), jnp.ones((MLA_ROPE,), F32)
    tab = jnp.zeros((SUBLANES, LANES), F32)
    tab = tab.at[ROW_FREQ16].set(jnp.tile(inv_freq, LANES // HALF_ROPE))
    tab = tab.at[ROW_MASK96].set(jnp.concatenate([jnp.ones((MLA_QK,), F32), z32]))
    tab = tab.at[ROW_KRMASK].set(jnp.concatenate([z32, o32, z32, o32]))
    tab = tab.at[ROW_GQ].set(gq_l).at[ROW_GQ_SW].set(jnp.roll(gq_l, LANES // 2))
    tab = tab.at[ROW_GK].set(gk_l).at[ROW_GK_SW].set(jnp.roll(gk_l, LANES // 2))

    groups = ROPE_ROWS // SUBLANES
    pos_packed = jnp.broadcast_to(
        positions.reshape(t // ROPE_ROWS, groups, SUBLANES).transpose(0, 2, 1)[..., None],
        (t // ROPE_ROWS, SUBLANES, groups, HALF_ROPE)).reshape(t // ROPE_ROWS, SUBLANES, LANES)

    n_blocks = D_IN_PAD // W_IN_BLOCK
    w_in_l = w_in_l.reshape(D_MODEL, n_blocks, W_IN_BLOCK).transpose(1, 0, 2)

    n = t // tm
    tile = lambda w: pl.BlockSpec((tm, w), lambda i: (jnp.minimum(i, n - 1), 0))
    prev = lambda w: pl.BlockSpec((tm, w), lambda i: (jnp.maximum(i - 1, 0), 0))
    out_widths = (GLA_QK, GLA_QK, GLA_WIDTH, GLA_QK, GLA_WIDTH, HEAD_PAD, HEAD_PAD, MLA_WIDTH)
    out_dtypes = (BF16, BF16, BF16, F32, BF16, BF16, BF16, BF16)
    out_specs = [tile(w) for w in out_widths]
    out_specs[5], out_specs[6] = prev(HEAD_PAD), prev(HEAD_PAD)
    return pl.pallas_call(
        _proj_kernel,
        grid=(n + 1,),
        in_specs=[
            tile(D_MODEL),
            pl.BlockSpec((tm // ROPE_ROWS, SUBLANES, LANES), lambda i: (jnp.maximum(i - 1, 0), 0, 0)),
            _const_spec((1, D_MODEL)), _const_spec((n_blocks, D_MODEL, W_IN_BLOCK)),
            _const_spec((LANES, GLA_QK)), _const_spec((1, GLA_QK)),
            _const_spec((1, MLA_Q_RANK)), _const_spec((MLA_Q_RANK, HEAD_PAD)),
            _const_spec((1, MLA_KV_RANK)), _const_spec((MLA_KV_RANK, HEAD_PAD)),
            _const_spec((MLA_KV_RANK, MLA_WIDTH)),
            _const_spec((8, LANES)),
        ],
        out_specs=out_specs,
        out_shape=[jax.ShapeDtypeStruct((t, w), d) for w, d in zip(out_widths, out_dtypes)],
        scratch_shapes=[pltpu.VMEM((tm, HEAD_PAD), F32), pltpu.VMEM((tm, HEAD_PAD), F32),
                        pltpu.VMEM((tm, LANES), F32)],
        compiler_params=pltpu.CompilerParams(
            dimension_semantics=("arbitrary",), vmem_limit_bytes=VMEM_LIMIT_BYTES),
        name="proj",
    )(x2, pos_packed, attn_norm.reshape(1, D_MODEL), w_in_l, w2_l, gate_b.reshape(1, GLA_QK),
      q_norm.reshape(1, MLA_Q_RANK), w_uq_l.astype(BF16), kv_norm.reshape(1, MLA_KV_RANK),
      w_uk_l.astype(BF16), w_uv.astype(BF16), tab)


def _cumsum(sizes):
    total = 0
    for s in sizes:
        total += s
        yield total


def _gla_kernel(q_ref, k_ref, v_ref, la_ref, gr_ref, gon_ref, o_ref, st_ref, kf_ref, b_ref):
    nseq, rows_per_step = q_ref.shape[0], q_ref.shape[1]

    @pl.when(pl.program_id(1) == 0)
    def _():
        st_ref[...] = jnp.zeros_like(st_ref)

    row = lax.broadcasted_iota(jnp.int32, (CHUNK, CHUNK), 0)
    col = lax.broadcasted_iota(jnp.int32, (CHUNK, CHUNK), 1)
    causal = col <= row
    lane = lax.broadcasted_iota(jnp.int32, (1, LANES), 1)
    lane_q = lax.broadcasted_iota(jnp.int32, (1, GLA_QK), 1)
    head_mask = [(lane // GLA_DK) == e for e in range(2)]
    gon = gon_ref[...]
    nt = (((1,), (1,)), ((), ()))
    tn = (((0,), (0,)), ((), ()))

    def exact_scores(q, k, b):
        kf_ref[...] = k
        b_ref[...] = b

        def key_row(s, sc):
            k_s = kf_ref[pl.ds(s, 1), :]
            b_s = b_ref[pl.ds(s, 1), :]
            prod = q * k_s * jnp.exp(jnp.minimum(b - b_s, 0.0))
            out = []
            for h in range(GLA_HEADS):
                hm = (lane_q // GLA_DK) == h
                col_h = jnp.sum(jnp.where(hm, prod, 0.0), axis=-1, keepdims=True)
                out.append(jnp.where(col == s, col_h, sc[h]))
            return tuple(out)

        zero = jnp.zeros((CHUNK, CHUNK), F32)
        return lax.fori_loop(0, CHUNK, key_row, (zero,) * GLA_HEADS)

    pair = lambda h: slice((h // 2) * LANES, (h // 2 + 1) * LANES)
    vsl = lambda h: slice(h * GLA_DV, (h + 1) * GLA_DV)
    heads = range(GLA_HEADS)

    def row_cumsum(x):
        row_id = lax.broadcasted_iota(jnp.int32, x.shape, 0)
        step = 1
        while step < CHUNK:
            if step < SUBLANES:
                moved = jnp.where(row_id >= step, pltpu.roll(x, step, 0), 0.0)
            else:
                moved = jnp.concatenate([jnp.zeros((step, x.shape[1]), x.dtype), x[:-step]], axis=0)
            x = x + moved
            step *= 2
        return x

    def chunk(seqs, c, factorised):
        rows = pl.ds(pl.multiple_of(c * CHUNK, CHUNK), CHUNK)
        q, k, b, a_h, kt, kd, eb_last, scores, o_inter = {}, {}, {}, {}, {}, {}, {}, {}, {}
        for n in seqs:
            b[n] = row_cumsum(la_ref[n, rows, :])
            b_last = b[n][CHUNK - 1:CHUNK, :]
            q[n] = q_ref[n, rows, :].astype(F32)
            k[n] = k_ref[n, rows, :].astype(F32)
            qt = q[n] * jnp.exp(b[n])
            eb_last[n] = jnp.exp(b_last)
            if factorised:
                k_grown = k[n] * jnp.exp(-b[n])
                kt[n] = k_grown.astype(BF16)
                kd[n] = k_grown * eb_last[n]
            else:
                kd[n] = k[n] * jnp.exp(b_last - b[n])
            a_h[n] = [jnp.where(head_mask[h % 2], qt[:, pair(h)], 0.0).astype(BF16) for h in heads]
        v, st, o = {}, {}, {}
        for n in seqs:
            for h in heads:
                st[n, h] = st_ref[n, h]
                if factorised:
                    both = lax.dot_general(
                        a_h[n][h], jnp.concatenate([st[n, h].astype(BF16), kt[n][:, pair(h)]], axis=0), nt,
                        preferred_element_type=F32)
                    o_inter[n, h], scores[n, h] = both[:, :GLA_DV], both[:, GLA_DV:]
                else:
                    o_inter[n, h] = lax.dot_general(a_h[n][h], st[n, h].astype(BF16), nt,
                                                    preferred_element_type=F32)
            if not factorised:
                for h, sc in enumerate(exact_scores(q[n], k[n], b[n])):
                    scores[n, h] = sc
        for n in seqs:
            for h in heads:
                sc = jnp.where(causal, scores[n, h], 0.0).astype(BF16)
                v[n, h] = v_ref[n, rows, vsl(h)]
                o[n, h] = o_inter[n, h] + jnp.dot(sc, v[n, h], preferred_element_type=F32)
        for n in seqs:
            for h in heads:
                kd_h = jnp.where(head_mask[h % 2], kd[n][:, pair(h)], 0.0).astype(BF16)
                upd = lax.dot_general(v[n, h], kd_h, tn, preferred_element_type=F32)
                st_ref[n, h] = st[n, h] * eb_last[n][:, pair(h)] + upd
        for n in seqs:
            for h in heads:
                on = o[n, h] * _rms_scale(o[n, h], GLA_DV) * gon
                o_ref[n, rows, vsl(h)] = (on * gr_ref[n, rows, vsl(h)].astype(F32)).astype(BF16)

    def factorised_chunks():
        def body(c, carry):
            chunk(range(nseq), c, True)
            return carry
        lax.fori_loop(0, rows_per_step // CHUNK, body, 0)

    def exact_chunks():
        def body(c, carry):
            for n in range(nseq):
                chunk([n], c, False)
            return carry
        lax.fori_loop(0, rows_per_step // CHUNK, body, 0)

    factorisable = jnp.min(la_ref[...]) >= GLA_FAST_MIN_LOG_DECAY / CHUNK
    lax.cond(factorisable, factorised_chunks, exact_chunks)


def _gla(qg, kg, vg, la, gr, out_norm):
    b, s, _ = qg.shape
    nseq = max(n for n in GLA_SEQS_PER_STEP if b % n == 0)
    rows = min(GLA_ROWS_PER_STEP, s)
    assert s % rows == 0 and rows % CHUNK == 0
    blk = lambda w: pl.BlockSpec((nseq, rows, w), lambda i, j: (i, j, 0))
    return pl.pallas_call(
        _gla_kernel,
        grid=(b // nseq, s // rows),
        in_specs=[blk(GLA_QK), blk(GLA_QK), blk(GLA_WIDTH), blk(GLA_QK), blk(GLA_WIDTH),
                  _const_spec((1, GLA_DV))],
        out_specs=blk(GLA_WIDTH),
        out_shape=jax.ShapeDtypeStruct((b, s, GLA_WIDTH), BF16),
        scratch_shapes=[pltpu.VMEM((nseq, GLA_HEADS, GLA_DV, LANES), F32),
                        pltpu.VMEM((CHUNK, GLA_QK), F32),
                        pltpu.VMEM((CHUNK, GLA_QK), F32)],
        compiler_params=pltpu.CompilerParams(
            dimension_semantics=("arbitrary", "arbitrary"), vmem_limit_bytes=VMEM_LIMIT_BYTES),
        name="gla",
    )(qg, kg, vg, la, gr, out_norm.reshape(1, GLA_DV))


def _mla_tiles(n_q):
    assert TQ == 2 * TK
    first = [(a, 2 * a, 0) for a in range(n_q)]
    full = [(a, j, TQ // CHUNK) for a in range(n_q) for j in range(2 * a)]
    second = [(a, 2 * a + 1, -(TK // CHUNK)) for a in range(n_q)]
    return tuple(zip(*(first + full + second)))


def _mla_kernel(ti_ref, tj_ref, toff_ref, bound_ref, q_ref, k_ref, v_ref, o_ref,
                s_ref, m_ref, acc_ref, v1_ref):
    seq = q_ref.shape[1]
    n_q = seq // TQ
    n_tiles = s_ref.shape[1]
    nt = (((1,), (1,)), ((), ()))
    heads = range(2)
    lane = lax.broadcasted_iota(jnp.int32, (1, LANES), 1)
    own = [(lane // MLA_V) == e for e in heads]
    unroll = lambda count, limit=MLA_UNROLL: max(u for u in range(1, limit + 1) if count % u == 0)

    def q_rows(i):
        return pl.ds(pl.multiple_of(i * TQ, TQ), TQ)

    def k_rows(j):
        return pl.ds(pl.multiple_of(j * TK, TK), TK)

    def masked_logits(e, t):
        hsl = slice(e * LANES, (e + 1) * LANES)
        s = lax.dot_general(q_ref[0, q_rows(ti_ref[t]), hsl], k_ref[0, k_rows(tj_ref[t]), hsl], nt,
                            preferred_element_type=F32)
        row_chunk = lax.broadcasted_iota(jnp.int32, (TQ, TK), 0) // CHUNK
        col_chunk = lax.broadcasted_iota(jnp.int32, (TQ, TK), 1) // CHUNK
        return jnp.where(col_chunk <= row_chunk + toff_ref[t], s, MASKED_LOGIT)

    def lane_fold_max(a):
        out = a[:, :LANES]
        for c in range(1, TK // LANES):
            out = jnp.maximum(out, a[:, c * LANES:(c + 1) * LANES])
        return out

    v = v_ref[0]
    for e in heads:
        v1_ref[e] = jnp.where(own[e], v, 1.0).astype(BF16)

    def exp_shifted(s, shift):
        return jnp.concatenate(
            [jnp.exp2(s[:, c * LANES:(c + 1) * LANES] - shift) for c in range(TK // LANES)],
            axis=1).astype(BF16)

    def accumulate(e, t, s, shift):
        acc_ref[e, ti_ref[t]] += jnp.dot(exp_shifted(s, shift), v1_ref[e, k_rows(tj_ref[t]), :],
                                         preferred_element_type=F32)

    def clear(i, carry):
        for e in heads:
            acc_ref[e, i] = jnp.zeros((TQ, LANES), F32)
        return carry

    def finish(i, carry):
        acc0, acc1 = acc_ref[0, i], acc_ref[1, i]
        num = jnp.where(own[0], acc0, acc1)
        den = pltpu.roll(jnp.where(own[0], acc1, acc0), LANES // 2, 1)
        o_ref[0, q_rows(i), :] = (num / den).astype(BF16)
        return carry

    def single_pass():
        lax.fori_loop(0, n_q, clear, 0)
        shift = bound_ref[0].astype(F32)

        def tile(t, carry):
            for e in heads:
                accumulate(e, t, masked_logits(e, t), shift)
            return carry

        lax.fori_loop(0, n_tiles - n_q, tile, 0, unroll=unroll(n_tiles - n_q))

        def lower_half_tile(t, carry):
            i = ti_ref[t]
            lower = pl.ds(pl.multiple_of(i * TQ + TK, TK), TK)
            row_chunk = lax.broadcasted_iota(jnp.int32, (TK, TK), 0) // CHUNK
            col_chunk = lax.broadcasted_iota(jnp.int32, (TK, TK), 1) // CHUNK
            for e in heads:
                hsl = slice(e * LANES, (e + 1) * LANES)
                s = lax.dot_general(q_ref[0, lower, hsl], k_ref[0, k_rows(tj_ref[t]), hsl], nt,
                                    preferred_element_type=F32)
                p = exp_shifted(jnp.where(col_chunk <= row_chunk, s, MASKED_LOGIT), shift)
                acc_ref[e, i, TK:, :] += jnp.dot(p, v1_ref[e, k_rows(tj_ref[t]), :],
                                                 preferred_element_type=F32)
            return finish(i, carry)

        lax.fori_loop(n_tiles - n_q, n_tiles, lower_half_tile, 0, unroll=unroll(n_q))

    def two_pass():
        def first_tile(t, carry):
            for e in heads:
                s = masked_logits(e, t)
                s_ref[e, t] = s
                m_ref[e, t] = lane_fold_max(s)
            return clear(t, carry)

        lax.fori_loop(0, n_q, first_tile, 0, unroll=unroll(n_q, MLA_UNROLL_TWO_PASS))

        def pass1(t, carry):
            i = ti_ref[t]
            for e in heads:
                s = masked_logits(e, t)
                s_ref[e, t] = s
                m_ref[e, i] = jnp.maximum(m_ref[e, i], lane_fold_max(s))
            return carry

        lax.fori_loop(n_q, n_tiles, pass1, 0, unroll=unroll(n_tiles - n_q, MLA_UNROLL_TWO_PASS))

        def row_max(i, carry):
            for e in heads:
                m_ref[e, i] = jnp.broadcast_to(jnp.max(m_ref[e, i], axis=-1, keepdims=True), (TQ, LANES))
            return carry

        lax.fori_loop(0, n_q, row_max, 0)

        def pass2(t, carry):
            for e in heads:
                accumulate(e, t, s_ref[e, t], m_ref[e, ti_ref[t]])
            return carry

        lax.fori_loop(0, n_tiles, pass2, 0, unroll=unroll(n_tiles, MLA_UNROLL_TWO_PASS))
        lax.fori_loop(0, n_q, finish, 0)

    lax.cond(bound_ref[0] <= MLA_MAX_SHIFT, single_pass, two_pass)


def _mla(q, k, v, gain_q, gain_k):
    b, s, _ = q.shape
    assert s % TQ == 0 and TQ % TK == 0
    n_q = s // TQ
    ti, tj, toff = _mla_tiles(n_q)
    bound = jnp.ceil(1.02 * LOG2E * math.sqrt(MLA_QK) * jnp.max(jnp.abs(gain_q)) * jnp.max(jnp.abs(gain_k)))
    bound = jnp.minimum(bound, 2.0 * MLA_MAX_SHIFT).astype(jnp.int32).reshape(1)
    blk = lambda w: pl.BlockSpec((1, s, w), lambda i, p, *_: (i, 0, p))
    return pl.pallas_call(
        _mla_kernel,
        grid_spec=pltpu.PrefetchScalarGridSpec(
            num_scalar_prefetch=4,
            grid=(b, MLA_HEADS // 2),
            in_specs=[blk(2 * LANES), blk(2 * LANES), blk(LANES)],
            out_specs=blk(LANES),
            scratch_shapes=[pltpu.VMEM((2, len(ti), TQ, TK), F32),
                            pltpu.VMEM((2, n_q, TQ, LANES), F32),
                            pltpu.VMEM((2, n_q, TQ, LANES), F32),
                            pltpu.VMEM((2, s, LANES), BF16)]),
        out_shape=jax.ShapeDtypeStruct((b, s, MLA_WIDTH), BF16),
        compiler_params=pltpu.CompilerParams(
            dimension_semantics=("arbitrary", "arbitrary"),
            vmem_limit_bytes=VMEM_LIMIT_BYTES),
        name="mla",
    )(jnp.asarray(ti, jnp.int32), jnp.asarray(tj, jnp.int32), jnp.asarray(toff, jnp.int32), bound,
      q, k, v)


def _tail_kernel(x_ref, og_ref, om_ref, p_ref, w_out_ref, g_mlp_ref, w_up_ref, w_down_ref,
                 g_ple_ref, w_gate_ref, b_gate_ref, w_pp_ref, o_ref):
    for part in range(x_ref.shape[0] // TM_TAIL_PART):
        rows = slice(part * TM_TAIL_PART, (part + 1) * TM_TAIL_PART)
        h = (x_ref[rows, :]
             + jnp.dot(og_ref[rows, :], w_out_ref[:GLA_WIDTH, :], preferred_element_type=F32)
             + jnp.dot(om_ref[rows, :], w_out_ref[GLA_WIDTH:, :], preferred_element_type=F32))
        m = (h * _rms_scale(h, D_MODEL) * g_mlp_ref[...]).astype(BF16)
        mlp = None
        for c in range(D_FF // FF_CHUNK):
            cols = slice(c * FF_CHUNK, (c + 1) * FF_CHUNK)
            u = jnp.maximum(jnp.dot(m, w_up_ref[:, cols], preferred_element_type=F32), 0.0)
            d = jnp.dot((u * u).astype(BF16), w_down_ref[cols, :], preferred_element_type=F32)
            mlp = d if mlp is None else mlp + d
        h = h + mlp
        g = (h * _rms_scale(h, D_MODEL) * g_ple_ref[...]).astype(BF16)
        gate = jax.nn.sigmoid(jnp.dot(g, w_gate_ref[...], preferred_element_type=F32) + b_gate_ref[...])
        pp = jnp.dot(p_ref[rows, :].astype(BF16), w_pp_ref[...], preferred_element_type=F32)
        o_ref[rows, :] = h + pp * gate


def _tail(x2, og, om, p2, w_out, mlp_norm, w_up, w_down, ple_norm, w_gate, b_gate, w_pp):
    t = x2.shape[0]
    tm = TM_TAIL
    assert t % tm == 0
    row = lambda w: pl.BlockSpec((tm, w), lambda i: (i, 0))
    return pl.pallas_call(
        _tail_kernel,
        grid=(t // tm,),
        in_specs=[row(D_MODEL), row(GLA_WIDTH), row(MLA_WIDTH), row(PLE_DIM),
                  _const_spec((D_MODEL, D_MODEL)), _const_spec((1, D_MODEL)),
                  _const_spec((D_MODEL, D_FF)), _const_spec((D_FF, D_MODEL)),
                  _const_spec((1, D_MODEL)), _const_spec((D_MODEL, D_MODEL)),
                  _const_spec((1, D_MODEL)), _const_spec((PLE_DIM, D_MODEL))],
        out_specs=row(D_MODEL),
        out_shape=jax.ShapeDtypeStruct((t, D_MODEL), F32),
        compiler_params=pltpu.CompilerParams(
            dimension_semantics=("arbitrary",), vmem_limit_bytes=VMEM_LIMIT_BYTES),
        name="tail",
    )(x2, og, om, p2, w_out.astype(BF16), mlp_norm.reshape(1, D_MODEL), w_up.astype(BF16),
      w_down.astype(BF16), ple_norm.reshape(1, D_MODEL), w_gate.astype(BF16),
      b_gate.reshape(1, D_MODEL), w_pp.astype(BF16))


def kernel(x, p, positions, attn_norm, w_in, gla_gate_w2, gla_gate_b, gla_out_norm, mla_q_norm,
           mla_w_uq, mla_kv_norm, mla_w_ukv, qk_norm_q, qk_norm_k, w_out, mlp_norm, w_mlp_up,
           w_mlp_down, ple_norm, w_ple_gate, b_ple_gate, w_ple_proj):
    b, s, d = x.shape
    depth = w_in.shape[0]
    t = b * s
    h = x.reshape(t, d)
    for i in range(depth):
        qg, kg, vg, la, gr, q, k, v = _projection(
            h, positions, attn_norm[i], w_in[i], gla_gate_w2[i], gla_gate_b[i], mla_q_norm[i],
            mla_w_uq[i], mla_kv_norm[i], mla_w_ukv[i], qk_norm_q[i], qk_norm_k[i])
        seq = lambda a: a.reshape(b, s, a.shape[-1])
        og = _gla(seq(qg), seq(kg), seq(vg), seq(la), seq(gr), gla_out_norm[i])
        om = _mla(seq(q), seq(k), seq(v), qk_norm_q[i], qk_norm_k[i])
        h = _tail(h, og.reshape(t, GLA_WIDTH), om.reshape(t, MLA_WIDTH), p[i].reshape(t, PLE_DIM),
                  w_out[i], mlp_norm[i], w_mlp_up[i], w_mlp_down[i], ple_norm[i], w_ple_gate[i],
                  b_ple_gate[i], w_ple_proj[i])
    return h.reshape(b, s, d)
```

```python
import math

import jax
import jax.numpy as jnp
from jax import lax
from jax.experimental import pallas as pl
from jax.experimental.pallas import tpu as pltpu

F32 = jnp.float32
BF16 = jnp.bfloat16

D_MODEL = 1024
CHUNK = 64
PLE_DIM = 256
D_FF = 4 * D_MODEL
EPS = 1e-6
GLA_HEADS = 4
GLA_DK = 64
GLA_DV = 128
GLA_GATE_RANK = 16
GLA_TAU = 16.0
GLA_QK = GLA_HEADS * GLA_DK
GLA_WIDTH = GLA_HEADS * GLA_DV
MLA_HEADS = 8
MLA_NOPE = 64
MLA_ROPE = 32
MLA_V = 64
MLA_QK = MLA_NOPE + MLA_ROPE
MLA_Q_RANK = 256
MLA_KV_RANK = 128
MLA_WIDTH = MLA_HEADS * MLA_V
ROPE_THETA = 10000.0
LOG2E = math.log2(math.e)
IN_SPLITS = (GLA_QK, GLA_QK, GLA_WIDTH, GLA_GATE_RANK, GLA_WIDTH, MLA_Q_RANK, MLA_KV_RANK, MLA_ROPE)

LANES = 128
SUBLANES = 8
HALF_ROPE = MLA_ROPE // 2
HALF_NOPE = MLA_NOPE // 2
HEAD_PAD = MLA_HEADS * LANES
D_IN_PAD = 2048
W_IN_BLOCK = 256
VMEM_LIMIT_BYTES = 56 * 1024 * 1024

TM_PROJ = 512
TM_TAIL = 512
FF_CHUNK = 1024
TQ = 512
TK = 256
MLA_UNROLL = 16
MLA_UNROLL_TWO_PASS = 4
ROPE_ROWS = 64
GLA_SEQS_PER_STEP = (8, 4, 2, 1)
GLA_ROWS_PER_STEP = 256
GLA_FAST_MIN_LOG_DECAY = -80.0
MASKED_LOGIT = -1e30
MLA_MAX_SHIFT = 40

ROW_FREQ16, ROW_MASK96, ROW_KRMASK, ROW_GQ, ROW_GQ_SW, ROW_GK, ROW_GK_SW = range(7)


def _const_spec(shape):
    return pl.BlockSpec(shape, lambda *_: (0,) * len(shape), pipeline_mode=pl.Buffered(1))


def _rms_scale(v, n):
    return lax.rsqrt(jnp.sum(v * v, axis=-1, keepdims=True) * (1.0 / n) + EPS)


def _proj_kernel(x_ref, pos_ref, g_attn_ref, w_in_ref, w2_ref, b2_ref, gqn_ref, w_uq_ref,
                 gkvn_ref, w_uk_ref, w_uv_ref, tab_ref,
                 qg_ref, kg_ref, vg_ref, la_ref, gr_ref, q_ref, k_ref, v_even_ref, v_odd_ref,
                 qs_ref, ks_ref, slot_ref):
    @pl.when(pl.program_id(0) == 0)
    def _():
        qs_ref[...] = jnp.zeros_like(qs_ref)
        ks_ref[...] = jnp.zeros_like(ks_ref)
        slot_ref[...] = jnp.zeros_like(slot_ref)

    x = x_ref[...]
    xn = (x * _rms_scale(x, D_MODEL) * g_attn_ref[...]).astype(BF16)

    tab = lambda r: tab_ref[r:r + 1, :]
    mask96, krmask = tab(ROW_MASK96), tab(ROW_KRMASK)
    gq, gq_sw = tab(ROW_GQ) * LOG2E, tab(ROW_GQ_SW) * LOG2E
    gk, gk_sw = tab(ROW_GK) * math.sqrt(MLA_QK), tab(ROW_GK_SW) * math.sqrt(MLA_QK)
    half = LANES // 2

    lane = lax.broadcasted_iota(jnp.int32, (1, LANES), 1)
    groups = ROPE_ROWS // SUBLANES
    assert groups * HALF_ROPE == LANES
    x1_lanes = (lane >= HALF_NOPE) & (lane < HALF_NOPE + HALF_ROPE)
    x2_lanes = (lane >= HALF_NOPE + HALF_ROPE) & (lane < HALF_NOPE + MLA_ROPE)

    def rope_tables(rb):
        ang = pos_ref[rb].astype(F32) * tab(ROW_FREQ16)
        cos_p, sin_p = jnp.cos(ang), jnp.sin(ang)
        cos, sin = [], []
        for a in range(groups):
            s1 = (HALF_NOPE - HALF_ROPE * a) % LANES
            s2 = (HALF_NOPE + HALF_ROPE - HALF_ROPE * a) % LANES
            c1, c2 = pltpu.roll(cos_p, s1, 1), pltpu.roll(cos_p, s2, 1)
            n1, n2 = pltpu.roll(sin_p, s1, 1), pltpu.roll(sin_p, s2, 1)
            cos.append(jnp.where(x1_lanes, c1, jnp.where(x2_lanes, c2, mask96)))
            sin.append(jnp.where(x1_lanes, -n1, jnp.where(x2_lanes, n2, 0.0)))
        return jnp.concatenate(cos, axis=0), jnp.concatenate(sin, axis=0)

    sum96 = (lax.broadcasted_iota(jnp.int32, (LANES, LANES), 0) < MLA_QK).astype(BF16)

    def head_rows(rb):
        rows = pl.ds(rb * ROPE_ROWS, ROPE_ROWS)
        cos, sin = rope_tables(rb)
        kr = slot_ref[rows, :] * krmask
        kr_rot = pltpu.roll(kr, half, 1) * (sin * gk_sw)
        cos_q, sin_q, cos_k = cos * gq, sin * gq_sw, cos * gk
        slabs = []
        for h in range(MLA_HEADS):
            sl = slice(h * LANES, (h + 1) * LANES)
            slabs += [qs_ref[rows, sl], ks_ref[rows, sl] + kr]
        squares = jnp.concatenate([(v * v).astype(BF16) for v in slabs], axis=0)
        rms = lax.rsqrt(jnp.dot(squares, sum96, preferred_element_type=F32) + MLA_QK * EPS)
        for h in range(MLA_HEADS):
            sl = slice(h * LANES, (h + 1) * LANES)
            qh, kh = slabs[2 * h], slabs[2 * h + 1]
            rq = rms[(2 * h) * ROPE_ROWS:(2 * h + 1) * ROPE_ROWS]
            rk = rms[(2 * h + 1) * ROPE_ROWS:(2 * h + 2) * ROPE_ROWS]
            q_ref[rows, sl] = ((qh * cos_q + pltpu.roll(qh, half, 1) * sin_q) * rq).astype(BF16)
            k_ref[rows, sl] = ((kh * cos_k + kr_rot) * rk).astype(BF16)

    n_blocks = w_in_ref.shape[0]
    assert x_ref.shape[0] // ROPE_ROWS == n_blocks == 8
    cq = ckv = slot = None
    for r in range(n_blocks):
        z = jnp.dot(xn, w_in_ref[r], preferred_element_type=F32)
        head_rows(r)
        half_cols = slice((r % 2) * W_IN_BLOCK, (r % 2 + 1) * W_IN_BLOCK)
        if r == 0:
            qg_ref[...] = (z * (GLA_DK ** -0.5)).astype(BF16)
        elif r == 1:
            kg_ref[...] = z.astype(BF16)
        elif r in (2, 3):
            vg_ref[:, half_cols] = z.astype(BF16)
        elif r in (4, 5):
            gr_ref[:, half_cols] = (z * jax.nn.sigmoid(z)).astype(BF16)
        elif r == 6:
            cq = z
        else:
            ckv, slot = z[:, :MLA_KV_RANK], z[:, MLA_KV_RANK:]
    slot_ref[...] = slot

    xg = jnp.dot(slot.astype(BF16), w2_ref[...], preferred_element_type=F32) + b2_ref[...]
    la_ref[...] = (jnp.minimum(xg, 0.0) - jnp.log(1.0 + jnp.exp(-jnp.abs(xg)))) * (1.0 / GLA_TAU)

    cqn = (cq * _rms_scale(cq, MLA_Q_RANK) * gqn_ref[...]).astype(BF16)
    qs_ref[...] = jnp.dot(cqn, w_uq_ref[...], preferred_element_type=F32)
    ckvn = (ckv * _rms_scale(ckv, MLA_KV_RANK) * gkvn_ref[...]).astype(BF16)
    ks_ref[...] = jnp.dot(ckvn, w_uk_ref[...], preferred_element_type=F32)
    v = jnp.dot(ckvn, w_uv_ref[...], preferred_element_type=F32)
    parity = (lax.broadcasted_iota(jnp.int32, (1, MLA_WIDTH), 1) // MLA_V) % 2
    v_even_ref[...] = jnp.where(parity == 0, v, 1.0).astype(BF16)
    v_odd_ref[...] = jnp.where(parity == 1, v, 1.0).astype(BF16)


def _head_layout(nope, rope):
    x1, x2 = rope[..., :HALF_ROPE], rope[..., HALF_ROPE:]
    return jnp.concatenate([nope[..., :HALF_NOPE], x1, x2, nope[..., HALF_NOPE:], x2, x1], axis=-1)


def _projection(x2, positions, attn_norm, w_in, gate_w2, gate_b, q_norm, w_uq, kv_norm, w_ukv,
                qk_norm_q, qk_norm_k):
    t = x2.shape[0]
    tm = TM_PROJ
    assert t % tm == 0

    gq, gk, gv, g_low, g_r, c_q, c_kv, k_r = jnp.split(w_in, list(_cumsum(IN_SPLITS))[:-1], axis=1)
    zeros = lambda n: jnp.zeros((D_MODEL, n), w_in.dtype)
    kr_sw = jnp.concatenate([k_r[:, HALF_ROPE:], k_r[:, :HALF_ROPE]], axis=1)
    slot = jnp.concatenate([g_low, zeros(HALF_NOPE - GLA_GATE_RANK), k_r, zeros(HALF_NOPE), kr_sw], axis=1)
    w_in_l = jnp.concatenate([gq, gk, gv, g_r, c_q, c_kv, slot], axis=1).astype(BF16)
    assert w_in_l.shape == (D_MODEL, D_IN_PAD)

    w2_l = jnp.zeros((LANES, GLA_QK), F32).at[:GLA_GATE_RANK].set(gate_w2).astype(BF16)

    w_uq_h = w_uq.reshape(MLA_Q_RANK, MLA_HEADS, MLA_QK)
    w_uq_l = _head_layout(w_uq_h[..., :MLA_NOPE], w_uq_h[..., MLA_NOPE:]).reshape(MLA_Q_RANK, HEAD_PAD)
    w_ukv_h = w_ukv.reshape(MLA_KV_RANK, MLA_HEADS, MLA_NOPE + MLA_V)
    w_uk_l = _head_layout(w_ukv_h[..., :MLA_NOPE],
                          jnp.zeros((MLA_KV_RANK, MLA_HEADS, MLA_ROPE), F32)).reshape(MLA_KV_RANK, HEAD_PAD)
    w_uv = w_ukv_h[..., MLA_NOPE:].reshape(MLA_KV_RANK, MLA_WIDTH)
    gq_l = _head_layout(qk_norm_q[:MLA_NOPE], qk_norm_q[MLA_NOPE:])
    gk_l = _head_layout(qk_norm_k[:MLA_NOPE], qk_norm_k[MLA_NOPE:])

    inv_freq = ROPE_THETA ** (-jnp.arange(0, MLA_ROPE, 2, dtype=F32) / MLA_ROPE)
    z32, o32 = jnp.zeros((HALF_NOPE,), F32), jnp.ones((MLA_ROPE,), F32)
    tab = jnp.zeros((SUBLANES, LANES), F32)
    tab = tab.at[ROW_FREQ16].set(jnp.tile(inv_freq, LANES // HALF_ROPE))
    tab = tab.at[ROW_MASK96].set(jnp.concatenate([jnp.ones((MLA_QK,), F32), z32]))
    tab = tab.at[ROW_KRMASK].set(jnp.concatenate([z32, o32, z32, o32]))
    tab = tab.at[ROW_GQ].set(gq_l).at[ROW_GQ_SW].set(jnp.roll(gq_l, LANES // 2))
    tab = tab.at[ROW_GK].set(gk_l).at[ROW_GK_SW].set(jnp.roll(gk_l, LANES // 2))

    groups = ROPE_ROWS // SUBLANES
    pos_packed = jnp.broadcast_to(
        positions.reshape(t // ROPE_ROWS, groups, SUBLANES).transpose(0, 2, 1)[..., None],
        (t // ROPE_ROWS, SUBLANES, groups, HALF_ROPE)).reshape(t // ROPE_ROWS, SUBLANES, LANES)

    n_blocks = D_IN_PAD // W_IN_BLOCK
    w_in_l = w_in_l.reshape(D_MODEL, n_blocks, W_IN_BLOCK).transpose(1, 0, 2)

    n = t // tm
    tile = lambda w: pl.BlockSpec((tm, w), lambda i: (jnp.minimum(i, n - 1), 0))
    prev = lambda w: pl.BlockSpec((tm, w), lambda i: (jnp.maximum(i - 1, 0), 0))
    out_widths = (GLA_QK, GLA_QK, GLA_WIDTH, GLA_QK, GLA_WIDTH, HEAD_PAD, HEAD_PAD, MLA_WIDTH, MLA_WIDTH)
    out_dtypes = (BF16, BF16, BF16, F32, BF16, BF16, BF16, BF16, BF16)
    out_specs = [tile(w) for w in out_widths]
    out_specs[5], out_specs[6] = prev(HEAD_PAD), prev(HEAD_PAD)
    return pl.pallas_call(
        _proj_kernel,
        grid=(n + 1,),
        in_specs=[
            tile(D_MODEL),
            pl.BlockSpec((tm // ROPE_ROWS, SUBLANES, LANES), lambda i: (jnp.maximum(i - 1, 0), 0, 0)),
            _const_spec((1, D_MODEL)), _const_spec((n_blocks, D_MODEL, W_IN_BLOCK)),
            _const_spec((LANES, GLA_QK)), _const_spec((1, GLA_QK)),
            _const_spec((1, MLA_Q_RANK)), _const_spec((MLA_Q_RANK, HEAD_PAD)),
            _const_spec((1, MLA_KV_RANK)), _const_spec((MLA_KV_RANK, HEAD_PAD)),
            _const_spec((MLA_KV_RANK, MLA_WIDTH)),
            _const_spec((8, LANES)),
        ],
        out_specs=out_specs,
        out_shape=[jax.ShapeDtypeStruct((t, w), d) for w, d in zip(out_widths, out_dtypes)],
        scratch_shapes=[pltpu.VMEM((tm, HEAD_PAD), F32), pltpu.VMEM((tm, HEAD_PAD), F32),
                        pltpu.VMEM((tm, LANES), F32)],
        compiler_params=pltpu.CompilerParams(
            dimension_semantics=("arbitrary",), vmem_limit_bytes=VMEM_LIMIT_BYTES),
        name="proj",
    )(x2, pos_packed, attn_norm.reshape(1, D_MODEL), w_in_l, w2_l, gate_b.reshape(1, GLA_QK),
      q_norm.reshape(1, MLA_Q_RANK), w_uq_l.astype(BF16), kv_norm.reshape(1, MLA_KV_RANK),
      w_uk_l.astype(BF16), w_uv.astype(BF16), tab)


def _cumsum(sizes):
    total = 0
    for s in sizes:
        total += s
        yield total


def _gla_kernel(q_ref, k_ref, v_ref, la_ref, gr_ref, gon_ref, o_ref, st_ref, kf_ref, b_ref):
    nseq, rows_per_step = q_ref.shape[0], q_ref.shape[1]

    @pl.when(pl.program_id(1) == 0)
    def _():
        st_ref[...] = jnp.zeros_like(st_ref)

    row = lax.broadcasted_iota(jnp.int32, (CHUNK, CHUNK), 0)
    col = lax.broadcasted_iota(jnp.int32, (CHUNK, CHUNK), 1)
    causal = col <= row
    lane = lax.broadcasted_iota(jnp.int32, (1, LANES), 1)
    lane_q = lax.broadcasted_iota(jnp.int32, (1, GLA_QK), 1)
    head_mask = [(lane // GLA_DK) == e for e in range(2)]
    gon = gon_ref[...]
    nt = (((1,), (1,)), ((), ()))
    tn = (((0,), (0,)), ((), ()))

    def exact_scores(q, k, b):
        kf_ref[...] = k
        b_ref[...] = b

        def key_row(s, sc):
            k_s = kf_ref[pl.ds(s, 1), :]
            b_s = b_ref[pl.ds(s, 1), :]
            prod = q * k_s * jnp.exp(jnp.minimum(b - b_s, 0.0))
            out = []
            for h in range(GLA_HEADS):
                hm = (lane_q // GLA_DK) == h
                col_h = jnp.sum(jnp.where(hm, prod, 0.0), axis=-1, keepdims=True)
                out.append(jnp.where(col == s, col_h, sc[h]))
            return tuple(out)

        zero = jnp.zeros((CHUNK, CHUNK), F32)
        return lax.fori_loop(0, CHUNK, key_row, (zero,) * GLA_HEADS)

    pair = lambda h: slice((h // 2) * LANES, (h // 2 + 1) * LANES)
    vsl = lambda h: slice(h * GLA_DV, (h + 1) * GLA_DV)
    heads = range(GLA_HEADS)

    def row_cumsum(x):
        row_id = lax.broadcasted_iota(jnp.int32, x.shape, 0)
        step = 1
        while step < CHUNK:
            if step < SUBLANES:
                moved = jnp.where(row_id >= step, pltpu.roll(x, step, 0), 0.0)
            else:
                moved = jnp.concatenate([jnp.zeros((step, x.shape[1]), x.dtype), x[:-step]], axis=0)
            x = x + moved
            step *= 2
        return x

    def chunk(seqs, c, factorised):
        rows = pl.ds(pl.multiple_of(c * CHUNK, CHUNK), CHUNK)
        q, k, b, a_h, kt, kd, eb_last, scores, o_inter = {}, {}, {}, {}, {}, {}, {}, {}, {}
        for n in seqs:
            b[n] = row_cumsum(la_ref[n, rows, :])
            b_last = b[n][CHUNK - 1:CHUNK, :]
            q[n] = q_ref[n, rows, :].astype(F32)
            k[n] = k_ref[n, rows, :].astype(F32)
            qt = q[n] * jnp.exp(b[n])
            eb_last[n] = jnp.exp(b_last)
            if factorised:
                k_grown = k[n] * jnp.exp(-b[n])
                kt[n] = k_grown.astype(BF16)
                kd[n] = k_grown * eb_last[n]
            else:
                kd[n] = k[n] * jnp.exp(b_last - b[n])
            a_h[n] = [jnp.where(head_mask[h % 2], qt[:, pair(h)], 0.0).astype(BF16) for h in heads]
        v, st, o = {}, {}, {}
        for n in seqs:
            for h in heads:
                st[n, h] = st_ref[n, h]
                if factorised:
                    both = lax.dot_general(
                        a_h[n][h], jnp.concatenate([st[n, h].astype(BF16), kt[n][:, pair(h)]], axis=0), nt,
                        preferred_element_type=F32)
                    o_inter[n, h], scores[n, h] = both[:, :GLA_DV], both[:, GLA_DV:]
                else:
                    o_inter[n, h] = lax.dot_general(a_h[n][h], st[n, h].astype(BF16), nt,
                                                    preferred_element_type=F32)
            if not factorised:
                for h, sc in enumerate(exact_scores(q[n], k[n], b[n])):
                    scores[n, h] = sc
        for n in seqs:
            for h in heads:
                sc = jnp.where(causal, scores[n, h], 0.0).astype(BF16)
                v[n, h] = v_ref[n, rows, vsl(h)]
                o[n, h] = o_inter[n, h] + jnp.dot(sc, v[n, h], preferred_element_type=F32)
        for n in seqs:
            for h in heads:
                kd_h = jnp.where(head_mask[h % 2], kd[n][:, pair(h)], 0.0).astype(BF16)
                upd = lax.dot_general(v[n, h], kd_h, tn, preferred_element_type=F32)
                st_ref[n, h] = st[n, h] * eb_last[n][:, pair(h)] + upd
        for n in seqs:
            for h in heads:
                on = o[n, h] * _rms_scale(o[n, h], GLA_DV) * gon
                o_ref[n, rows, vsl(h)] = (on * gr_ref[n, rows, vsl(h)].astype(F32)).astype(BF16)

    def factorised_chunks():
        def body(c, carry):
            chunk(range(nseq), c, True)
            return carry
        lax.fori_loop(0, rows_per_step // CHUNK, body, 0)

    def exact_chunks():
        def body(c, carry):
            for n in range(nseq):
                chunk([n], c, False)
            return carry
        lax.fori_loop(0, rows_per_step // CHUNK, body, 0)

    factorisable = jnp.min(la_ref[...]) >= GLA_FAST_MIN_LOG_DECAY / CHUNK
    lax.cond(factorisable, factorised_chunks, exact_chunks)


def _gla(qg, kg, vg, la, gr, out_norm):
    b, s, _ = qg.shape
    nseq = max(n for n in GLA_SEQS_PER_STEP if b % n == 0)
    rows = min(GLA_ROWS_PER_STEP, s)
    assert s % rows == 0 and rows % CHUNK == 0
    blk = lambda w: pl.BlockSpec((nseq, rows, w), lambda i, j: (i, j, 0))
    return pl.pallas_call(
        _gla_kernel,
        grid=(b // nseq, s // rows),
        in_specs=[blk(GLA_QK), blk(GLA_QK), blk(GLA_WIDTH), blk(GLA_QK), blk(GLA_WIDTH),
                  _const_spec((1, GLA_DV))],
        out_specs=blk(GLA_WIDTH),
        out_shape=jax.ShapeDtypeStruct((b, s, GLA_WIDTH), BF16),
        scratch_shapes=[pltpu.VMEM((nseq, GLA_HEADS, GLA_DV, LANES), F32),
                        pltpu.VMEM((CHUNK, GLA_QK), F32),
                        pltpu.VMEM((CHUNK, GLA_QK), F32)],
        compiler_params=pltpu.CompilerParams(
            dimension_semantics=("arbitrary", "arbitrary"), vmem_limit_bytes=VMEM_LIMIT_BYTES),
        name="gla",
    )(qg, kg, vg, la, gr, out_norm.reshape(1, GLA_DV))


def _mla_tiles(n_q):
    assert TQ == 2 * TK
    first = [(a, 2 * a, 0) for a in range(n_q)]
    full = [(a, j, TQ // CHUNK) for a in range(n_q) for j in range(2 * a)]
    second = [(a, 2 * a + 1, -(TK // CHUNK)) for a in range(n_q)]
    return tuple(zip(*(first + full + second)))


def _mla_kernel(ti_ref, tj_ref, toff_ref, bound_ref, q_ref, k_ref, v_even_ref, v_odd_ref, o_ref,
                s_ref, m_ref, acc_ref):
    seq = q_ref.shape[1]
    n_q = seq // TQ
    n_tiles = s_ref.shape[1]
    nt = (((1,), (1,)), ((), ()))
    heads = range(2)
    lane = lax.broadcasted_iota(jnp.int32, (1, LANES), 1)
    own = [(lane // MLA_V) == e for e in heads]
    unroll = lambda count, limit=MLA_UNROLL: max(u for u in range(1, limit + 1) if count % u == 0)

    def q_rows(i):
        return pl.ds(pl.multiple_of(i * TQ, TQ), TQ)

    def k_rows(j):
        return pl.ds(pl.multiple_of(j * TK, TK), TK)

    def masked_logits(e, t):
        hsl = slice(e * LANES, (e + 1) * LANES)
        s = lax.dot_general(q_ref[0, q_rows(ti_ref[t]), hsl], k_ref[0, k_rows(tj_ref[t]), hsl], nt,
                            preferred_element_type=F32)
        row_chunk = lax.broadcasted_iota(jnp.int32, (TQ, TK), 0) // CHUNK
        col_chunk = lax.broadcasted_iota(jnp.int32, (TQ, TK), 1) // CHUNK
        return jnp.where(col_chunk <= row_chunk + toff_ref[t], s, MASKED_LOGIT)

    def lane_fold_max(a):
        out = a[:, :LANES]
        for c in range(1, TK // LANES):
            out = jnp.maximum(out, a[:, c * LANES:(c + 1) * LANES])
        return out

    values = (v_even_ref, v_odd_ref)

    def exp_shifted(s, shift):
        return jnp.concatenate(
            [jnp.exp2(s[:, c * LANES:(c + 1) * LANES] - shift) for c in range(TK // LANES)],
            axis=1).astype(BF16)

    def accumulate(e, t, s, shift):
        acc_ref[e, ti_ref[t]] += jnp.dot(exp_shifted(s, shift), values[e][0, k_rows(tj_ref[t]), :],
                                         preferred_element_type=F32)

    def clear(i, carry):
        for e in heads:
            acc_ref[e, i] = jnp.zeros((TQ, LANES), F32)
        return carry

    def finish(i, carry):
        acc0, acc1 = acc_ref[0, i], acc_ref[1, i]
        num = jnp.where(own[0], acc0, acc1)
        den = pltpu.roll(jnp.where(own[0], acc1, acc0), LANES // 2, 1)
        o_ref[0, q_rows(i), :] = (num / den).astype(BF16)
        return carry

    def single_pass():
        lax.fori_loop(0, n_q, clear, 0)
        shift = bound_ref[0].astype(F32)

        def tile(t, carry):
            for e in heads:
                accumulate(e, t, masked_logits(e, t), shift)
            return carry

        lax.fori_loop(0, n_tiles - n_q, tile, 0, unroll=unroll(n_tiles - n_q))

        def lower_half_tile(t, carry):
            i = ti_ref[t]
            lower = pl.ds(pl.multiple_of(i * TQ + TK, TK), TK)
            row_chunk = lax.broadcasted_iota(jnp.int32, (TK, TK), 0) // CHUNK
            col_chunk = lax.broadcasted_iota(jnp.int32, (TK, TK), 1) // CHUNK
            for e in heads:
                hsl = slice(e * LANES, (e + 1) * LANES)
                s = lax.dot_general(q_ref[0, lower, hsl], k_ref[0, k_rows(tj_ref[t]), hsl], nt,
                                    preferred_element_type=F32)
                p = exp_shifted(jnp.where(col_chunk <= row_chunk, s, MASKED_LOGIT), shift)
                acc_ref[e, i, TK:, :] += jnp.dot(p, values[e][0, k_rows(tj_ref[t]), :],
                                                 preferred_element_type=F32)
            return finish(i, carry)

        lax.fori_loop(n_tiles - n_q, n_tiles, lower_half_tile, 0, unroll=unroll(n_q))

    def two_pass():
        def first_tile(t, carry):
            for e in heads:
                s = masked_logits(e, t)
                s_ref[e, t] = s
                m_ref[e, t] = lane_fold_max(s)
            return clear(t, carry)

        lax.fori_loop(0, n_q, first_tile, 0, unroll=unroll(n_q, MLA_UNROLL_TWO_PASS))

        def pass1(t, carry):
            i = ti_ref[t]
            for e in heads:
                s = masked_logits(e, t)
                s_ref[e, t] = s
                m_ref[e, i] = jnp.maximum(m_ref[e, i], lane_fold_max(s))
            return carry

        lax.fori_loop(n_q, n_tiles, pass1, 0, unroll=unroll(n_tiles - n_q, MLA_UNROLL_TWO_PASS))

        def row_max(i, carry):
            for e in heads:
                m_ref[e, i] = jnp.broadcast_to(jnp.max(m_ref[e, i], axis=-1, keepdims=True), (TQ, LANES))
            return carry

        lax.fori_loop(0, n_q, row_max, 0)

        def pass2(t, carry):
            for e in heads:
                accumulate(e, t, s_ref[e, t], m_ref[e, ti_ref[t]])
            return carry

        lax.fori_loop(0, n_tiles, pass2, 0, unroll=unroll(n_tiles, MLA_UNROLL_TWO_PASS))
        lax.fori_loop(0, n_q, finish, 0)

    lax.cond(bound_ref[0] <= MLA_MAX_SHIFT, single_pass, two_pass)


def _mla(q, k, v_even, v_odd, gain_q, gain_k):
    b, s, _ = q.shape
    assert s % TQ == 0 and TQ % TK == 0
    n_q = s // TQ
    ti, tj, toff = _mla_tiles(n_q)
    bound = jnp.ceil(1.02 * LOG2E * math.sqrt(MLA_QK) * jnp.max(jnp.abs(gain_q)) * jnp.max(jnp.abs(gain_k)))
    bound = jnp.minimum(bound, 2.0 * MLA_MAX_SHIFT).astype(jnp.int32).reshape(1)
    blk = lambda w: pl.BlockSpec((1, s, w), lambda i, p, *_: (i, 0, p))
    return pl.pallas_call(
        _mla_kernel,
        grid_spec=pltpu.PrefetchScalarGridSpec(
            num_scalar_prefetch=4,
            grid=(b, MLA_HEADS // 2),
            in_specs=[blk(2 * LANES), blk(2 * LANES), blk(LANES), blk(LANES)],
            out_specs=blk(LANES),
            scratch_shapes=[pltpu.VMEM((2, len(ti), TQ, TK), F32),
                            pltpu.VMEM((2, n_q, TQ, LANES), F32),
                            pltpu.VMEM((2, n_q, TQ, LANES), F32)]),
        out_shape=jax.ShapeDtypeStruct((b, s, MLA_WIDTH), BF16),
        compiler_params=pltpu.CompilerParams(
            dimension_semantics=("arbitrary", "arbitrary"),
            vmem_limit_bytes=VMEM_LIMIT_BYTES),
        name="mla",
    )(jnp.asarray(ti, jnp.int32), jnp.asarray(tj, jnp.int32), jnp.asarray(toff, jnp.int32), bound,
      q, k, v_even, v_odd)


def _tail_kernel(x_ref, og_ref, om_ref, p_ref, w_out_ref, g_mlp_ref, w_up_ref, w_down_ref,
                 g_ple_ref, w_gate_ref, b_gate_ref, w_pp_ref, o_ref):
    h = (x_ref[...]
         + jnp.dot(og_ref[...], w_out_ref[:GLA_WIDTH, :], preferred_element_type=F32)
         + jnp.dot(om_ref[...], w_out_ref[GLA_WIDTH:, :], preferred_element_type=F32))
    m = (h * _rms_scale(h, D_MODEL) * g_mlp_ref[...]).astype(BF16)
    mlp = None
    for c in range(D_FF // FF_CHUNK):
        cols = slice(c * FF_CHUNK, (c + 1) * FF_CHUNK)
        u = jnp.maximum(jnp.dot(m, w_up_ref[:, cols], preferred_element_type=F32), 0.0)
        d = jnp.dot((u * u).astype(BF16), w_down_ref[cols, :], preferred_element_type=F32)
        mlp = d if mlp is None else mlp + d
    h = h + mlp
    g = (h * _rms_scale(h, D_MODEL) * g_ple_ref[...]).astype(BF16)
    gate = jax.nn.sigmoid(jnp.dot(g, w_gate_ref[...], preferred_element_type=F32) + b_gate_ref[...])
    pp = jnp.dot(p_ref[...].astype(BF16), w_pp_ref[...], preferred_element_type=F32)
    o_ref[...] = h + pp * gate


def _tail(x2, og, om, p2, w_out, mlp_norm, w_up, w_down, ple_norm, w_gate, b_gate, w_pp):
    t = x2.shape[0]
    tm = TM_TAIL
    assert t % tm == 0
    row = lambda w: pl.BlockSpec((tm, w), lambda i: (i, 0))
    return pl.pallas_call(
        _tail_kernel,
        grid=(t // tm,),
        in_specs=[row(D_MODEL), row(GLA_WIDTH), row(MLA_WIDTH), row(PLE_DIM),
                  _const_spec((D_MODEL, D_MODEL)), _const_spec((1, D_MODEL)),
                  _const_spec((D_MODEL, D_FF)), _const_spec((D_FF, D_MODEL)),
                  _const_spec((1, D_MODEL)), _const_spec((D_MODEL, D_MODEL)),
                  _const_spec((1, D_MODEL)), _const_spec((PLE_DIM, D_MODEL))],
        out_specs=row(D_MODEL),
        out_shape=jax.ShapeDtypeStruct((t, D_MODEL), F32),
        compiler_params=pltpu.CompilerParams(
            dimension_semantics=("arbitrary",), vmem_limit_bytes=VMEM_LIMIT_BYTES),
        name="tail",
    )(x2, og, om, p2, w_out.astype(BF16), mlp_norm.reshape(1, D_MODEL), w_up.astype(BF16),
      w_down.astype(BF16), ple_norm.reshape(1, D_MODEL), w_gate.astype(BF16),
      b_gate.reshape(1, D_MODEL), w_pp.astype(BF16))


def kernel(x, p, positions, attn_norm, w_in, gla_gate_w2, gla_gate_b, gla_out_norm, mla_q_norm,
           mla_w_uq, mla_kv_norm, mla_w_ukv, qk_norm_q, qk_norm_k, w_out, mlp_norm, w_mlp_up,
           w_mlp_down, ple_norm, w_ple_gate, b_ple_gate, w_ple_proj):
    b, s, d = x.shape
    depth = w_in.shape[0]
    t = b * s
    h = x.reshape(t, d)
    for i in range(depth):
        qg, kg, vg, la, gr, q, k, v_even, v_odd = _projection(
            h, positions, attn_norm[i], w_in[i], gla_gate_w2[i], gla_gate_b[i], mla_q_norm[i],
            mla_w_uq[i], mla_kv_norm[i], mla_w_ukv[i], qk_norm_q[i], qk_norm_k[i])
        seq = lambda a: a.reshape(b, s, a.shape[-1])
        og = _gla(seq(qg), seq(kg), seq(vg), seq(la), seq(gr), gla_out_norm[i])
        om = _mla(seq(q), seq(k), seq(v_even), seq(v_odd), qk_norm_q[i], qk_norm_k[i])
        h = _tail(h, og.reshape(t, GLA_WIDTH), om.reshape(t, MLA_WIDTH), p[i].reshape(t, PLE_DIM),
                  w_out[i], mlp_norm[i], w_mlp_up[i], w_mlp_down[i], ple_norm[i], w_ple_gate[i],
                  b_ple_gate[i], w_ple_proj[i])
    return h.reshape(b, s, d)
```

```python
import math

import jax
import jax.numpy as jnp
from jax import lax
from jax.experimental import pallas as pl
from jax.experimental.pallas import tpu as pltpu

F32 = jnp.float32
BF16 = jnp.bfloat16

D_MODEL = 1024
CHUNK = 64
PLE_DIM = 256
D_FF = 4 * D_MODEL
EPS = 1e-6
GLA_HEADS = 4
GLA_DK = 64
GLA_DV = 128
GLA_GATE_RANK = 16
GLA_TAU = 16.0
GLA_QK = GLA_HEADS * GLA_DK
GLA_WIDTH = GLA_HEADS * GLA_DV
MLA_HEADS = 8
MLA_NOPE = 64
MLA_ROPE = 32
MLA_V = 64
MLA_QK = MLA_NOPE + MLA_ROPE
MLA_Q_RANK = 256
MLA_KV_RANK = 128
MLA_WIDTH = MLA_HEADS * MLA_V
ROPE_THETA = 10000.0
LOG2E = math.log2(math.e)
IN_SPLITS = (GLA_QK, GLA_QK, GLA_WIDTH, GLA_GATE_RANK, GLA_WIDTH, MLA_Q_RANK, MLA_KV_RANK, MLA_ROPE)

LANES = 128
SUBLANES = 8
HALF_ROPE = MLA_ROPE // 2
HALF_NOPE = MLA_NOPE // 2
HEAD_PAD = MLA_HEADS * LANES
D_IN_PAD = 2048
W_IN_BLOCK = 256
VMEM_LIMIT_BYTES = 56 * 1024 * 1024

TM_PROJ = 512
TM_TAIL = 512
FF_CHUNK = 1024
TQ = 512
TK = 256
MLA_UNROLL = 16
MLA_UNROLL_TWO_PASS = 4
ROPE_ROWS = 64
GLA_SEQS_PER_STEP = (8, 4, 2, 1)
GLA_ROWS_PER_STEP = 256
GLA_FAST_MIN_LOG_DECAY = -80.0
MASKED_LOGIT = -1e30
MLA_MAX_SHIFT = 40

ROW_FREQ16, ROW_MASK96, ROW_KRMASK, ROW_GQ, ROW_GQ_SW, ROW_GK, ROW_GK_SW = range(7)


def _const_spec(shape):
    return pl.BlockSpec(shape, lambda *_: (0,) * len(shape), pipeline_mode=pl.Buffered(1))


def _rms_scale(v, n):
    return lax.rsqrt(jnp.sum(v * v, axis=-1, keepdims=True) * (1.0 / n) + EPS)


def _proj_kernel(x_ref, pos_ref, g_attn_ref, w_in_ref, w2_ref, b2_ref, gqn_ref, w_uq_ref,
                 gkvn_ref, w_uk_ref, w_uv_ref, tab_ref,
                 qg_ref, kg_ref, vg_ref, la_ref, gr_ref, q_ref, k_ref, v_even_ref, v_odd_ref,
                 qs_ref, ks_ref, slot_ref):
    @pl.when(pl.program_id(0) == 0)
    def _():
        qs_ref[...] = jnp.zeros_like(qs_ref)
        ks_ref[...] = jnp.zeros_like(ks_ref)
        slot_ref[...] = jnp.zeros_like(slot_ref)

    x = x_ref[...]
    xn = (x * _rms_scale(x, D_MODEL) * g_attn_ref[...]).astype(BF16)

    tab = lambda r: tab_ref[r:r + 1, :]
    mask96, krmask = tab(ROW_MASK96), tab(ROW_KRMASK)
    gq, gq_sw = tab(ROW_GQ) * LOG2E, tab(ROW_GQ_SW) * LOG2E
    gk, gk_sw = tab(ROW_GK) * math.sqrt(MLA_QK), tab(ROW_GK_SW) * math.sqrt(MLA_QK)
    half = LANES // 2

    lane = lax.broadcasted_iota(jnp.int32, (1, LANES), 1)
    groups = ROPE_ROWS // SUBLANES
    assert groups * HALF_ROPE == LANES
    x1_lanes = (lane >= HALF_NOPE) & (lane < HALF_NOPE + HALF_ROPE)
    x2_lanes = (lane >= HALF_NOPE + HALF_ROPE) & (lane < HALF_NOPE + MLA_ROPE)

    def rope_tables(rb):
        ang = pos_ref[rb].astype(F32) * tab(ROW_FREQ16)
        cos_p, sin_p = jnp.cos(ang), jnp.sin(ang)
        cos, sin = [], []
        for a in range(groups):
            s1 = (HALF_NOPE - HALF_ROPE * a) % LANES
            s2 = (HALF_NOPE + HALF_ROPE - HALF_ROPE * a) % LANES
            c1, c2 = pltpu.roll(cos_p, s1, 1), pltpu.roll(cos_p, s2, 1)
            n1, n2 = pltpu.roll(sin_p, s1, 1), pltpu.roll(sin_p, s2, 1)
            cos.append(jnp.where(x1_lanes, c1, jnp.where(x2_lanes, c2, mask96)))
            sin.append(jnp.where(x1_lanes, -n1, jnp.where(x2_lanes, n2, 0.0)))
        return jnp.concatenate(cos, axis=0), jnp.concatenate(sin, axis=0)

    sum96 = (lax.broadcasted_iota(jnp.int32, (LANES, LANES), 0) < MLA_QK).astype(BF16)

    def head_rows(rb):
        rows = pl.ds(rb * ROPE_ROWS, ROPE_ROWS)
        cos, sin = rope_tables(rb)
        kr = slot_ref[rows, :] * krmask
        kr_rot = pltpu.roll(kr, half, 1) * (sin * gk_sw)
        cos_q, sin_q, cos_k = cos * gq, sin * gq_sw, cos * gk
        slabs = []
        for h in range(MLA_HEADS):
            sl = slice(h * LANES, (h + 1) * LANES)
            slabs += [qs_ref[rows, sl], ks_ref[rows, sl] + kr]
        squares = jnp.concatenate([(v * v).astype(BF16) for v in slabs], axis=0)
        rms = lax.rsqrt(jnp.dot(squares, sum96, preferred_element_type=F32) + MLA_QK * EPS)
        for h in range(MLA_HEADS):
            sl = slice(h * LANES, (h + 1) * LANES)
            qh, kh = slabs[2 * h], slabs[2 * h + 1]
            rq = rms[(2 * h) * ROPE_ROWS:(2 * h + 1) * ROPE_ROWS]
            rk = rms[(2 * h + 1) * ROPE_ROWS:(2 * h + 2) * ROPE_ROWS]
            q_ref[rows, sl] = ((qh * cos_q + pltpu.roll(qh, half, 1) * sin_q) * rq).astype(BF16)
            k_ref[rows, sl] = ((kh * cos_k + kr_rot) * rk).astype(BF16)

    n_blocks = w_in_ref.shape[0]
    assert x_ref.shape[0] // ROPE_ROWS == n_blocks == 8
    cq = ckv = slot = None
    for r in range(n_blocks):
        z = jnp.dot(xn, w_in_ref[r], preferred_element_type=F32)
        head_rows(r)
        half_cols = slice((r % 2) * W_IN_BLOCK, (r % 2 + 1) * W_IN_BLOCK)
        if r == 0:
            qg_ref[...] = (z * (GLA_DK ** -0.5)).astype(BF16)
        elif r == 1:
            kg_ref[...] = z.astype(BF16)
        elif r in (2, 3):
            vg_ref[:, half_cols] = z.astype(BF16)
        elif r in (4, 5):
            gr_ref[:, half_cols] = (z * jax.nn.sigmoid(z)).astype(BF16)
        elif r == 6:
            cq = z
        else:
            ckv, slot = z[:, :MLA_KV_RANK], z[:, MLA_KV_RANK:]
    slot_ref[...] = slot

    xg = jnp.dot(slot.astype(BF16), w2_ref[...], preferred_element_type=F32) + b2_ref[...]
    la_ref[...] = (jnp.minimum(xg, 0.0) - jnp.log(1.0 + jnp.exp(-jnp.abs(xg)))) * (1.0 / GLA_TAU)

    cqn = (cq * _rms_scale(cq, MLA_Q_RANK) * gqn_ref[...]).astype(BF16)
    qs_ref[...] = jnp.dot(cqn, w_uq_ref[...], preferred_element_type=F32)
    ckvn = (ckv * _rms_scale(ckv, MLA_KV_RANK) * gkvn_ref[...]).astype(BF16)
    ks_ref[...] = jnp.dot(ckvn, w_uk_ref[...], preferred_element_type=F32)
    v = jnp.dot(ckvn, w_uv_ref[...], preferred_element_type=F32)
    parity = (lax.broadcasted_iota(jnp.int32, (1, MLA_WIDTH), 1) // MLA_V) % 2
    v_even_ref[...] = jnp.where(parity == 0, v, 1.0).astype(BF16)
    v_odd_ref[...] = jnp.where(parity == 1, v, 1.0).astype(BF16)


def _head_layout(nope, rope):
    x1, x2 = rope[..., :HALF_ROPE], rope[..., HALF_ROPE:]
    return jnp.concatenate([nope[..., :HALF_NOPE], x1, x2, nope[..., HALF_NOPE:], x2, x1], axis=-1)


def _projection(x2, positions, attn_norm, w_in, gate_w2, gate_b, q_norm, w_uq, kv_norm, w_ukv,
                qk_norm_q, qk_norm_k):
    t = x2.shape[0]
    tm = TM_PROJ
    assert t % tm == 0

    gq, gk, gv, g_low, g_r, c_q, c_kv, k_r = jnp.split(w_in, list(_cumsum(IN_SPLITS))[:-1], axis=1)
    zeros = lambda n: jnp.zeros((D_MODEL, n), w_in.dtype)
    kr_sw = jnp.concatenate([k_r[:, HALF_ROPE:], k_r[:, :HALF_ROPE]], axis=1)
    slot = jnp.concatenate([g_low, zeros(HALF_NOPE - GLA_GATE_RANK), k_r, zeros(HALF_NOPE), kr_sw], axis=1)
    w_in_l = jnp.concatenate([gq, gk, gv, g_r, c_q, c_kv, slot], axis=1).astype(BF16)
    assert w_in_l.shape == (D_MODEL, D_IN_PAD)

    w2_l = jnp.zeros((LANES, GLA_QK), F32).at[:GLA_GATE_RANK].set(gate_w2).astype(BF16)

    w_uq_h = w_uq.reshape(MLA_Q_RANK, MLA_HEADS, MLA_QK)
    w_uq_l = _head_layout(w_uq_h[..., :MLA_NOPE], w_uq_h[..., MLA_NOPE:]).reshape(MLA_Q_RANK, HEAD_PAD)
    w_ukv_h = w_ukv.reshape(MLA_KV_RANK, MLA_HEADS, MLA_NOPE + MLA_V)
    w_uk_l = _head_layout(w_ukv_h[..., :MLA_NOPE],
                          jnp.zeros((MLA_KV_RANK, MLA_HEADS, MLA_ROPE), F32)).reshape(MLA_KV_RANK, HEAD_PAD)
    w_uv = w_ukv_h[..., MLA_NOPE:].reshape(MLA_KV_RANK, MLA_WIDTH)
    gq_l = _head_layout(qk_norm_q[:MLA_NOPE], qk_norm_q[MLA_NOPE:])
    gk_l = _head_layout(qk_norm_k[:MLA_NOPE], qk_norm_k[MLA_NOPE:])

    inv_freq = ROPE_THETA ** (-jnp.arange(0, MLA_ROPE, 2, dtype=F32) / MLA_ROPE)
    z32, o32 = jnp.zeros((HALF_NOPE,), F32), jnp.ones((MLA_ROPE,), F32)
    tab = jnp.zeros((SUBLANES, LANES), F32)
    tab = tab.at[ROW_FREQ16].set(jnp.tile(inv_freq, LANES // HALF_ROPE))
    tab = tab.at[ROW_MASK96].set(jnp.concatenate([jnp.ones((MLA_QK,), F32), z32]))
    tab = tab.at[ROW_KRMASK].set(jnp.concatenate([z32, o32, z32, o32]))
    tab = tab.at[ROW_GQ].set(gq_l).at[ROW_GQ_SW].set(jnp.roll(gq_l, LANES // 2))
    tab = tab.at[ROW_GK].set(gk_l).at[ROW_GK_SW].set(jnp.roll(gk_l, LANES // 2))

    groups = ROPE_ROWS // SUBLANES
    pos_packed = jnp.broadcast_to(
        positions.reshape(t // ROPE_ROWS, groups, SUBLANES).transpose(0, 2, 1)[..., None],
        (t // ROPE_ROWS, SUBLANES, groups, HALF_ROPE)).reshape(t // ROPE_ROWS, SUBLANES, LANES)

    n_blocks = D_IN_PAD // W_IN_BLOCK
    w_in_l = w_in_l.reshape(D_MODEL, n_blocks, W_IN_BLOCK).transpose(1, 0, 2)

    n = t // tm
    tile = lambda w: pl.BlockSpec((tm, w), lambda i: (jnp.minimum(i, n - 1), 0))
    prev = lambda w: pl.BlockSpec((tm, w), lambda i: (jnp.maximum(i - 1, 0), 0))
    out_widths = (GLA_QK, GLA_QK, GLA_WIDTH, GLA_QK, GLA_WIDTH, HEAD_PAD, HEAD_PAD, MLA_WIDTH, MLA_WIDTH)
    out_dtypes = (BF16, BF16, BF16, F32, BF16, BF16, BF16, BF16, BF16)
    out_specs = [tile(w) for w in out_widths]
    out_specs[5], out_specs[6] = prev(HEAD_PAD), prev(HEAD_PAD)
    return pl.pallas_call(
        _proj_kernel,
        grid=(n + 1,),
        in_specs=[
            tile(D_MODEL),
            pl.BlockSpec((tm // ROPE_ROWS, SUBLANES, LANES), lambda i: (jnp.maximum(i - 1, 0), 0, 0)),
            _const_spec((1, D_MODEL)), _const_spec((n_blocks, D_MODEL, W_IN_BLOCK)),
            _const_spec((LANES, GLA_QK)), _const_spec((1, GLA_QK)),
            _const_spec((1, MLA_Q_RANK)), _const_spec((MLA_Q_RANK, HEAD_PAD)),
            _const_spec((1, MLA_KV_RANK)), _const_spec((MLA_KV_RANK, HEAD_PAD)),
            _const_spec((MLA_KV_RANK, MLA_WIDTH)),
            _const_spec((8, LANES)),
        ],
        out_specs=out_specs,
        out_shape=[jax.ShapeDtypeStruct((t, w), d) for w, d in zip(out_widths, out_dtypes)],
        scratch_shapes=[pltpu.VMEM((tm, HEAD_PAD), F32), pltpu.VMEM((tm, HEAD_PAD), F32),
                        pltpu.VMEM((tm, LANES), F32)],
        compiler_params=pltpu.CompilerParams(
            dimension_semantics=("arbitrary",), vmem_limit_bytes=VMEM_LIMIT_BYTES),
        name="proj",
    )(x2, pos_packed, attn_norm.reshape(1, D_MODEL), w_in_l, w2_l, gate_b.reshape(1, GLA_QK),
      q_norm.reshape(1, MLA_Q_RANK), w_uq_l.astype(BF16), kv_norm.reshape(1, MLA_KV_RANK),
      w_uk_l.astype(BF16), w_uv.astype(BF16), tab)


def _cumsum(sizes):
    total = 0
    for s in sizes:
        total += s
        yield total


def _gla_kernel(q_ref, k_ref, v_ref, la_ref, o_ref, st_ref, kf_ref, b_ref):
    nseq, rows_per_step = q_ref.shape[0], q_ref.shape[1]

    @pl.when(pl.program_id(1) == 0)
    def _():
        st_ref[...] = jnp.zeros_like(st_ref)

    row = lax.broadcasted_iota(jnp.int32, (CHUNK, CHUNK), 0)
    col = lax.broadcasted_iota(jnp.int32, (CHUNK, CHUNK), 1)
    causal = col <= row
    lane = lax.broadcasted_iota(jnp.int32, (1, LANES), 1)
    lane_q = lax.broadcasted_iota(jnp.int32, (1, GLA_QK), 1)
    head_mask = [(lane // GLA_DK) == e for e in range(2)]
    nt = (((1,), (1,)), ((), ()))
    tn = (((0,), (0,)), ((), ()))

    def exact_scores(q, k, b):
        kf_ref[...] = k
        b_ref[...] = b

        def key_row(s, sc):
            k_s = kf_ref[pl.ds(s, 1), :]
            b_s = b_ref[pl.ds(s, 1), :]
            prod = q * k_s * jnp.exp(jnp.minimum(b - b_s, 0.0))
            out = []
            for h in range(GLA_HEADS):
                hm = (lane_q // GLA_DK) == h
                col_h = jnp.sum(jnp.where(hm, prod, 0.0), axis=-1, keepdims=True)
                out.append(jnp.where(col == s, col_h, sc[h]))
            return tuple(out)

        zero = jnp.zeros((CHUNK, CHUNK), F32)
        return lax.fori_loop(0, CHUNK, key_row, (zero,) * GLA_HEADS)

    pair = lambda h: slice((h // 2) * LANES, (h // 2 + 1) * LANES)
    vsl = lambda h: slice(h * GLA_DV, (h + 1) * GLA_DV)
    heads = range(GLA_HEADS)

    def row_cumsum(x):
        row_id = lax.broadcasted_iota(jnp.int32, x.shape, 0)
        step = 1
        while step < CHUNK:
            if step < SUBLANES:
                moved = jnp.where(row_id >= step, pltpu.roll(x, step, 0), 0.0)
            else:
                moved = jnp.concatenate([jnp.zeros((step, x.shape[1]), x.dtype), x[:-step]], axis=0)
            x = x + moved
            step *= 2
        return x

    def chunk(seqs, c, factorised):
        rows = pl.ds(pl.multiple_of(c * CHUNK, CHUNK), CHUNK)
        q, k, b, a_h, kt, kd, eb_last, scores, o_inter = {}, {}, {}, {}, {}, {}, {}, {}, {}
        for n in seqs:
            b[n] = row_cumsum(la_ref[n, rows, :])
            b_last = b[n][CHUNK - 1:CHUNK, :]
            q[n] = q_ref[n, rows, :].astype(F32)
            k[n] = k_ref[n, rows, :].astype(F32)
            qt = q[n] * jnp.exp(b[n])
            eb_last[n] = jnp.exp(b_last)
            if factorised:
                k_grown = k[n] * jnp.exp(-b[n])
                kt[n] = k_grown.astype(BF16)
                kd[n] = k_grown * eb_last[n]
            else:
                kd[n] = k[n] * jnp.exp(b_last - b[n])
            a_h[n] = [jnp.where(head_mask[h % 2], qt[:, pair(h)], 0.0).astype(BF16) for h in heads]
        v, st, o = {}, {}, {}
        for n in seqs:
            for h in heads:
                st[n, h] = st_ref[n, h]
                if factorised:
                    both = lax.dot_general(
                        a_h[n][h], jnp.concatenate([st[n, h].astype(BF16), kt[n][:, pair(h)]], axis=0), nt,
                        preferred_element_type=F32)
                    o_inter[n, h], scores[n, h] = both[:, :GLA_DV], both[:, GLA_DV:]
                else:
                    o_inter[n, h] = lax.dot_general(a_h[n][h], st[n, h].astype(BF16), nt,
                                                    preferred_element_type=F32)
            if not factorised:
                for h, sc in enumerate(exact_scores(q[n], k[n], b[n])):
                    scores[n, h] = sc
        for n in seqs:
            for h in heads:
                sc = jnp.where(causal, scores[n, h], 0.0).astype(BF16)
                v[n, h] = v_ref[n, rows, vsl(h)]
                o[n, h] = o_inter[n, h] + jnp.dot(sc, v[n, h], preferred_element_type=F32)
        for n in seqs:
            for h in heads:
                kd_h = jnp.where(head_mask[h % 2], kd[n][:, pair(h)], 0.0).astype(BF16)
                upd = lax.dot_general(v[n, h], kd_h, tn, preferred_element_type=F32)
                st_ref[n, h] = st[n, h] * eb_last[n][:, pair(h)] + upd
        for n in seqs:
            for h in heads:
                o_ref[n, rows, vsl(h)] = o[n, h].astype(BF16)

    def factorised_chunks():
        def body(c, carry):
            chunk(range(nseq), c, True)
            return carry
        lax.fori_loop(0, rows_per_step // CHUNK, body, 0)

    def exact_chunks():
        def body(c, carry):
            for n in range(nseq):
                chunk([n], c, False)
            return carry
        lax.fori_loop(0, rows_per_step // CHUNK, body, 0)

    factorisable = jnp.min(la_ref[...]) >= GLA_FAST_MIN_LOG_DECAY / CHUNK
    lax.cond(factorisable, factorised_chunks, exact_chunks)


def _gla(qg, kg, vg, la):
    b, s, _ = qg.shape
    nseq = max(n for n in GLA_SEQS_PER_STEP if b % n == 0)
    rows = min(GLA_ROWS_PER_STEP, s)
    assert s % rows == 0 and rows % CHUNK == 0
    blk = lambda w: pl.BlockSpec((nseq, rows, w), lambda i, j: (i, j, 0))
    return pl.pallas_call(
        _gla_kernel,
        grid=(b // nseq, s // rows),
        in_specs=[blk(GLA_QK), blk(GLA_QK), blk(GLA_WIDTH), blk(GLA_QK)],
        out_specs=blk(GLA_WIDTH),
        out_shape=jax.ShapeDtypeStruct((b, s, GLA_WIDTH), BF16),
        scratch_shapes=[pltpu.VMEM((nseq, GLA_HEADS, GLA_DV, LANES), F32),
                        pltpu.VMEM((CHUNK, GLA_QK), F32),
                        pltpu.VMEM((CHUNK, GLA_QK), F32)],
        compiler_params=pltpu.CompilerParams(
            dimension_semantics=("arbitrary", "arbitrary"), vmem_limit_bytes=VMEM_LIMIT_BYTES),
        name="gla",
    )(qg, kg, vg, la)


def _mla_tiles(n_q):
    assert TQ == 2 * TK
    first = [(a, 2 * a, 0) for a in range(n_q)]
    full = [(a, j, TQ // CHUNK) for a in range(n_q) for j in range(2 * a)]
    second = [(a, 2 * a + 1, -(TK // CHUNK)) for a in range(n_q)]
    return tuple(zip(*(first + full + second)))


def _mla_kernel(ti_ref, tj_ref, toff_ref, bound_ref, q_ref, k_ref, v_even_ref, v_odd_ref, o_ref,
                s_ref, m_ref, acc_ref):
    seq = q_ref.shape[1]
    n_q = seq // TQ
    n_tiles = s_ref.shape[1]
    nt = (((1,), (1,)), ((), ()))
    heads = range(2)
    lane = lax.broadcasted_iota(jnp.int32, (1, LANES), 1)
    own = [(lane // MLA_V) == e for e in heads]
    unroll = lambda count, limit=MLA_UNROLL: max(u for u in range(1, limit + 1) if count % u == 0)

    def q_rows(i):
        return pl.ds(pl.multiple_of(i * TQ, TQ), TQ)

    def k_rows(j):
        return pl.ds(pl.multiple_of(j * TK, TK), TK)

    def masked_logits(e, t):
        hsl = slice(e * LANES, (e + 1) * LANES)
        s = lax.dot_general(q_ref[0, q_rows(ti_ref[t]), hsl], k_ref[0, k_rows(tj_ref[t]), hsl], nt,
                            preferred_element_type=F32)
        row_chunk = lax.broadcasted_iota(jnp.int32, (TQ, TK), 0) // CHUNK
        col_chunk = lax.broadcasted_iota(jnp.int32, (TQ, TK), 1) // CHUNK
        return jnp.where(col_chunk <= row_chunk + toff_ref[t], s, MASKED_LOGIT)

    def lane_fold_max(a):
        out = a[:, :LANES]
        for c in range(1, TK // LANES):
            out = jnp.maximum(out, a[:, c * LANES:(c + 1) * LANES])
        return out

    values = (v_even_ref, v_odd_ref)

    def exp_shifted(s, shift):
        return jnp.concatenate(
            [jnp.exp2(s[:, c * LANES:(c + 1) * LANES] - shift) for c in range(TK // LANES)],
            axis=1).astype(BF16)

    def accumulate(e, t, s, shift):
        acc_ref[e, ti_ref[t]] += jnp.dot(exp_shifted(s, shift), values[e][0, k_rows(tj_ref[t]), :],
                                         preferred_element_type=F32)

    def clear(i, carry):
        for e in heads:
            acc_ref[e, i] = jnp.zeros((TQ, LANES), F32)
        return carry

    def finish(i, carry):
        acc0, acc1 = acc_ref[0, i], acc_ref[1, i]
        num = jnp.where(own[0], acc0, acc1)
        den = pltpu.roll(jnp.where(own[0], acc1, acc0), LANES // 2, 1)
        o_ref[0, q_rows(i), :] = (num / den).astype(BF16)
        return carry

    def single_pass():
        lax.fori_loop(0, n_q, clear, 0)
        shift = bound_ref[0].astype(F32)

        def tile(t, carry):
            for e in heads:
                accumulate(e, t, masked_logits(e, t), shift)
            return carry

        lax.fori_loop(0, n_tiles - n_q, tile, 0, unroll=unroll(n_tiles - n_q))

        def lower_half_tile(t, carry):
            i = ti_ref[t]
            lower = pl.ds(pl.multiple_of(i * TQ + TK, TK), TK)
            row_chunk = lax.broadcasted_iota(jnp.int32, (TK, TK), 0) // CHUNK
            col_chunk = lax.broadcasted_iota(jnp.int32, (TK, TK), 1) // CHUNK
            for e in heads:
                hsl = slice(e * LANES, (e + 1) * LANES)
                s = lax.dot_general(q_ref[0, lower, hsl], k_ref[0, k_rows(tj_ref[t]), hsl], nt,
                                    preferred_element_type=F32)
                p = exp_shifted(jnp.where(col_chunk <= row_chunk, s, MASKED_LOGIT), shift)
                acc_ref[e, i, TK:, :] += jnp.dot(p, values[e][0, k_rows(tj_ref[t]), :],
                                                 preferred_element_type=F32)
            return finish(i, carry)

        lax.fori_loop(n_tiles - n_q, n_tiles, lower_half_tile, 0, unroll=unroll(n_q))

    def two_pass():
        def first_tile(t, carry):
            for e in heads:
                s = masked_logits(e, t)
                s_ref[e, t] = s
                m_ref[e, t] = lane_fold_max(s)
            return clear(t, carry)

        lax.fori_loop(0, n_q, first_tile, 0, unroll=unroll(n_q, MLA_UNROLL_TWO_PASS))

        def pass1(t, carry):
            i = ti_ref[t]
            for e in heads:
                s = masked_logits(e, t)
                s_ref[e, t] = s
                m_ref[e, i] = jnp.maximum(m_ref[e, i], lane_fold_max(s))
            return carry

        lax.fori_loop(n_q, n_tiles, pass1, 0, unroll=unroll(n_tiles - n_q, MLA_UNROLL_TWO_PASS))

        def row_max(i, carry):
            for e in heads:
                m_ref[e, i] = jnp.broadcast_to(jnp.max(m_ref[e, i], axis=-1, keepdims=True), (TQ, LANES))
            return carry

        lax.fori_loop(0, n_q, row_max, 0)

        def pass2(t, carry):
            for e in heads:
                accumulate(e, t, s_ref[e, t], m_ref[e, ti_ref[t]])
            return carry

        lax.fori_loop(0, n_tiles, pass2, 0, unroll=unroll(n_tiles, MLA_UNROLL_TWO_PASS))
        lax.fori_loop(0, n_q, finish, 0)

    lax.cond(bound_ref[0] <= MLA_MAX_SHIFT, single_pass, two_pass)


def _mla(q, k, v_even, v_odd, gain_q, gain_k):
    b, s, _ = q.shape
    assert s % TQ == 0 and TQ % TK == 0
    n_q = s // TQ
    ti, tj, toff = _mla_tiles(n_q)
    bound = jnp.ceil(1.02 * LOG2E * math.sqrt(MLA_QK) * jnp.max(jnp.abs(gain_q)) * jnp.max(jnp.abs(gain_k)))
    bound = jnp.minimum(bound, 2.0 * MLA_MAX_SHIFT).astype(jnp.int32).reshape(1)
    blk = lambda w: pl.BlockSpec((1, s, w), lambda i, p, *_: (i, 0, p))
    return pl.pallas_call(
        _mla_kernel,
        grid_spec=pltpu.PrefetchScalarGridSpec(
            num_scalar_prefetch=4,
            grid=(b, MLA_HEADS // 2),
            in_specs=[blk(2 * LANES), blk(2 * LANES), blk(LANES), blk(LANES)],
            out_specs=blk(LANES),
            scratch_shapes=[pltpu.VMEM((2, len(ti), TQ, TK), F32),
                            pltpu.VMEM((2, n_q, TQ, LANES), F32),
                            pltpu.VMEM((2, n_q, TQ, LANES), F32)]),
        out_shape=jax.ShapeDtypeStruct((b, s, MLA_WIDTH), BF16),
        compiler_params=pltpu.CompilerParams(
            dimension_semantics=("arbitrary", "arbitrary"),
            vmem_limit_bytes=VMEM_LIMIT_BYTES),
        name="mla",
    )(jnp.asarray(ti, jnp.int32), jnp.asarray(tj, jnp.int32), jnp.asarray(toff, jnp.int32), bound,
      q, k, v_even, v_odd)


def _tail_kernel(x_ref, og_ref, gr_ref, om_ref, p_ref, gon_ref, w_out_ref, g_mlp_ref, w_up_ref,
                 w_down_ref, g_ple_ref, w_gate_ref, b_gate_ref, w_pp_ref, o_ref):
    gated = []
    for hd in range(GLA_HEADS):
        sl = slice(hd * GLA_DV, (hd + 1) * GLA_DV)
        o_h = og_ref[:, sl].astype(F32)
        gated.append((o_h * _rms_scale(o_h, GLA_DV) * gon_ref[...] * gr_ref[:, sl].astype(F32)).astype(BF16))
    h = (x_ref[...]
         + jnp.dot(om_ref[...], w_out_ref[GLA_WIDTH:, :], preferred_element_type=F32)
         + jnp.dot(jnp.concatenate(gated, axis=1), w_out_ref[:GLA_WIDTH, :], preferred_element_type=F32))
    m = (h * _rms_scale(h, D_MODEL) * g_mlp_ref[...]).astype(BF16)
    mlp = None
    for c in range(D_FF // FF_CHUNK):
        cols = slice(c * FF_CHUNK, (c + 1) * FF_CHUNK)
        u = jnp.maximum(jnp.dot(m, w_up_ref[:, cols], preferred_element_type=F32), 0.0)
        d = jnp.dot((u * u).astype(BF16), w_down_ref[cols, :], preferred_element_type=F32)
        mlp = d if mlp is None else mlp + d
    h = h + mlp
    g = (h * _rms_scale(h, D_MODEL) * g_ple_ref[...]).astype(BF16)
    gate = jax.nn.sigmoid(jnp.dot(g, w_gate_ref[...], preferred_element_type=F32) + b_gate_ref[...])
    pp = jnp.dot(p_ref[...].astype(BF16), w_pp_ref[...], preferred_element_type=F32)
    o_ref[...] = h + pp * gate


def _tail(x2, og, gr, om, p2, out_norm, w_out, mlp_norm, w_up, w_down, ple_norm, w_gate, b_gate, w_pp):
    t = x2.shape[0]
    tm = TM_TAIL
    assert t % tm == 0
    row = lambda w: pl.BlockSpec((tm, w), lambda i: (i, 0))
    return pl.pallas_call(
        _tail_kernel,
        grid=(t // tm,),
        in_specs=[row(D_MODEL), row(GLA_WIDTH), row(GLA_WIDTH), row(MLA_WIDTH), row(PLE_DIM),
                  _const_spec((1, GLA_DV)), _const_spec((D_MODEL, D_MODEL)), _const_spec((1, D_MODEL)),
                  _const_spec((D_MODEL, D_FF)), _const_spec((D_FF, D_MODEL)),
                  _const_spec((1, D_MODEL)), _const_spec((D_MODEL, D_MODEL)),
                  _const_spec((1, D_MODEL)), _const_spec((PLE_DIM, D_MODEL))],
        out_specs=row(D_MODEL),
        out_shape=jax.ShapeDtypeStruct((t, D_MODEL), F32),
        compiler_params=pltpu.CompilerParams(
            dimension_semantics=("arbitrary",), vmem_limit_bytes=VMEM_LIMIT_BYTES),
        name="tail",
    )(x2, og, gr, om, p2, out_norm.reshape(1, GLA_DV), w_out.astype(BF16), mlp_norm.reshape(1, D_MODEL), w_up.astype(BF16),
      w_down.astype(BF16), ple_norm.reshape(1, D_MODEL), w_gate.astype(BF16),
      b_gate.reshape(1, D_MODEL), w_pp.astype(BF16))


def kernel(x, p, positions, attn_norm, w_in, gla_gate_w2, gla_gate_b, gla_out_norm, mla_q_norm,
           mla_w_uq, mla_kv_norm, mla_w_ukv, qk_norm_q, qk_norm_k, w_out, mlp_norm, w_mlp_up,
           w_mlp_down, ple_norm, w_ple_gate, b_ple_gate, w_ple_proj):
    b, s, d = x.shape
    depth = w_in.shape[0]
    t = b * s
    h = x.reshape(t, d)
    for i in range(depth):
        qg, kg, vg, la, gr, q, k, v_even, v_odd = _projection(
            h, positions, attn_norm[i], w_in[i], gla_gate_w2[i], gla_gate_b[i], mla_q_norm[i],
            mla_w_uq[i], mla_kv_norm[i], mla_w_ukv[i], qk_norm_q[i], qk_norm_k[i])
        seq = lambda a: a.reshape(b, s, a.shape[-1])
        og = _gla(seq(qg), seq(kg), seq(vg), seq(la))
        om = _mla(seq(q), seq(k), seq(v_even), seq(v_odd), qk_norm_q[i], qk_norm_k[i])
        h = _tail(h, og.reshape(t, GLA_WIDTH), gr, om.reshape(t, MLA_WIDTH), p[i].reshape(t, PLE_DIM),
                  gla_out_norm[i], w_out[i], mlp_norm[i], w_mlp_up[i], w_mlp_down[i], ple_norm[i], w_ple_gate[i],
                  b_ple_gate[i], w_ple_proj[i])
    return h.reshape(b, s, d)
```

```python
import math

import jax
import jax.numpy as jnp
from jax import lax
from jax.experimental import pallas as pl
from jax.experimental.pallas import tpu as pltpu

F32 = jnp.float32
BF16 = jnp.bfloat16

D_MODEL = 1024
CHUNK = 64
PLE_DIM = 256
D_FF = 4 * D_MODEL
EPS = 1e-6
GLA_HEADS = 4
GLA_DK = 64
GLA_DV = 128
GLA_GATE_RANK = 16
GLA_TAU = 16.0
GLA_QK = GLA_HEADS * GLA_DK
GLA_WIDTH = GLA_HEADS * GLA_DV
MLA_HEADS = 8
MLA_NOPE = 64
MLA_ROPE = 32
MLA_V = 64
MLA_QK = MLA_NOPE + MLA_ROPE
MLA_Q_RANK = 256
MLA_KV_RANK = 128
MLA_WIDTH = MLA_HEADS * MLA_V
ROPE_THETA = 10000.0
LOG2E = math.log2(math.e)
IN_SPLITS = (GLA_QK, GLA_QK, GLA_WIDTH, GLA_GATE_RANK, GLA_WIDTH, MLA_Q_RANK, MLA_KV_RANK, MLA_ROPE)

LANES = 128
SUBLANES = 8
HALF_ROPE = MLA_ROPE // 2
HALF_NOPE = MLA_NOPE // 2
HEAD_PAD = MLA_HEADS * LANES
D_IN_PAD = 2048
W_IN_BLOCK = 256
VMEM_LIMIT_BYTES = 56 * 1024 * 1024

TM_PROJ = 512
TM_TAIL = 512
FF_CHUNK = 1024
TQ = 512
TK = 256
MLA_UNROLL = 16
MLA_UNROLL_TWO_PASS = 4
ROPE_ROWS = 64
GLA_SEQS_PER_STEP = (8, 4, 2, 1)
GLA_ROWS_PER_STEP = 256
GLA_FAST_MIN_LOG_DECAY = -80.0
MASKED_LOGIT = -1e30
MLA_MAX_SHIFT = 40

ROW_FREQ16, ROW_MASK96, ROW_KRMASK, ROW_GQ, ROW_GQ_SW, ROW_GK, ROW_GK_SW = range(7)


def _const_spec(shape):
    return pl.BlockSpec(shape, lambda *_: (0,) * len(shape), pipeline_mode=pl.Buffered(1))


def _rms_scale(v, n):
    return lax.rsqrt(jnp.sum(v * v, axis=-1, keepdims=True) * (1.0 / n) + EPS)


def _proj_kernel(x_ref, pos_ref, g_attn_ref, w_in_ref, w2_ref, b2_ref, gqn_ref, w_uq_ref,
                 gkvn_ref, w_uk_ref, w_uv_ref, tab_ref,
                 qg_ref, kg_ref, vg_ref, la_ref, gr_ref, q_ref, k_ref, v_even_ref, v_odd_ref,
                 qs_ref, ks_ref, slot_ref):
    @pl.when(pl.program_id(0) == 0)
    def _():
        qs_ref[...] = jnp.zeros_like(qs_ref)
        ks_ref[...] = jnp.zeros_like(ks_ref)
        slot_ref[...] = jnp.zeros_like(slot_ref)

    x = x_ref[...]
    xn = (x * _rms_scale(x, D_MODEL) * g_attn_ref[...]).astype(BF16)

    tab = lambda r: tab_ref[r:r + 1, :]
    mask96, krmask = tab(ROW_MASK96), tab(ROW_KRMASK)
    gq, gq_sw = tab(ROW_GQ) * LOG2E, tab(ROW_GQ_SW) * LOG2E
    gk, gk_sw = tab(ROW_GK) * math.sqrt(MLA_QK), tab(ROW_GK_SW) * math.sqrt(MLA_QK)
    half = LANES // 2

    lane = lax.broadcasted_iota(jnp.int32, (1, LANES), 1)
    groups = ROPE_ROWS // SUBLANES
    assert groups * HALF_ROPE == LANES
    x1_lanes = (lane >= HALF_NOPE) & (lane < HALF_NOPE + HALF_ROPE)
    x2_lanes = (lane >= HALF_NOPE + HALF_ROPE) & (lane < HALF_NOPE + MLA_ROPE)

    def rope_tables(rb):
        ang = pos_ref[rb].astype(F32) * tab(ROW_FREQ16)
        cos_p, sin_p = jnp.cos(ang), jnp.sin(ang)
        cos, sin = [], []
        for a in range(groups):
            s1 = (HALF_NOPE - HALF_ROPE * a) % LANES
            s2 = (HALF_NOPE + HALF_ROPE - HALF_ROPE * a) % LANES
            c1, c2 = pltpu.roll(cos_p, s1, 1), pltpu.roll(cos_p, s2, 1)
            n1, n2 = pltpu.roll(sin_p, s1, 1), pltpu.roll(sin_p, s2, 1)
            cos.append(jnp.where(x1_lanes, c1, jnp.where(x2_lanes, c2, mask96)))
            sin.append(jnp.where(x1_lanes, -n1, jnp.where(x2_lanes, n2, 0.0)))
        return jnp.concatenate(cos, axis=0), jnp.concatenate(sin, axis=0)

    sum96 = (lax.broadcasted_iota(jnp.int32, (LANES, LANES), 0) < MLA_QK).astype(BF16)

    def head_rows(rb):
        rows = pl.ds(rb * ROPE_ROWS, ROPE_ROWS)
        cos, sin = rope_tables(rb)
        kr = slot_ref[rows, :] * krmask
        kr_rot = pltpu.roll(kr, half, 1) * (sin * gk_sw)
        cos_q, sin_q, cos_k = cos * gq, sin * gq_sw, cos * gk
        slabs = []
        for h in range(MLA_HEADS):
            sl = slice(h * LANES, (h + 1) * LANES)
            slabs += [qs_ref[rows, sl], ks_ref[rows, sl] + kr]
        squares = jnp.concatenate([(v * v).astype(BF16) for v in slabs], axis=0)
        rms = lax.rsqrt(jnp.dot(squares, sum96, preferred_element_type=F32) + MLA_QK * EPS)
        for h in range(MLA_HEADS):
            sl = slice(h * LANES, (h + 1) * LANES)
            qh, kh = slabs[2 * h], slabs[2 * h + 1]
            rq = rms[(2 * h) * ROPE_ROWS:(2 * h + 1) * ROPE_ROWS]
            rk = rms[(2 * h + 1) * ROPE_ROWS:(2 * h + 2) * ROPE_ROWS]
            q_ref[rows, sl] = ((qh * cos_q + pltpu.roll(qh, half, 1) * sin_q) * rq).astype(BF16)
            k_ref[rows, sl] = ((kh * cos_k + kr_rot) * rk).astype(BF16)

    n_blocks = w_in_ref.shape[0]
    assert x_ref.shape[0] // ROPE_ROWS == n_blocks == 8
    cq = ckv = slot = None
    for r in range(n_blocks):
        z = jnp.dot(xn, w_in_ref[r], preferred_element_type=F32)
        head_rows(r)
        half_cols = slice((r % 2) * W_IN_BLOCK, (r % 2 + 1) * W_IN_BLOCK)
        if r == 0:
            qg_ref[...] = (z * (GLA_DK ** -0.5)).astype(BF16)
        elif r == 1:
            kg_ref[...] = z.astype(BF16)
        elif r in (2, 3):
            vg_ref[:, half_cols] = z.astype(BF16)
        elif r in (4, 5):
            gr_ref[:, half_cols] = (z * jax.nn.sigmoid(z)).astype(BF16)
        elif r == 6:
            cq = z
        else:
            ckv, slot = z[:, :MLA_KV_RANK], z[:, MLA_KV_RANK:]
    slot_ref[...] = slot

    xg = jnp.dot(slot.astype(BF16), w2_ref[...], preferred_element_type=F32) + b2_ref[...]
    la_ref[...] = (jnp.minimum(xg, 0.0) - jnp.log(1.0 + jnp.exp(-jnp.abs(xg)))) * (LOG2E / GLA_TAU)

    cqn = (cq * _rms_scale(cq, MLA_Q_RANK) * gqn_ref[...]).astype(BF16)
    qs_ref[...] = jnp.dot(cqn, w_uq_ref[...], preferred_element_type=F32)
    ckvn = (ckv * _rms_scale(ckv, MLA_KV_RANK) * gkvn_ref[...]).astype(BF16)
    ks_ref[...] = jnp.dot(ckvn, w_uk_ref[...], preferred_element_type=F32)
    v = jnp.dot(ckvn, w_uv_ref[...], preferred_element_type=F32)
    parity = (lax.broadcasted_iota(jnp.int32, (1, MLA_WIDTH), 1) // MLA_V) % 2
    v_even_ref[...] = jnp.where(parity == 0, v, 1.0).astype(BF16)
    v_odd_ref[...] = jnp.where(parity == 1, v, 1.0).astype(BF16)


def _head_layout(nope, rope):
    x1, x2 = rope[..., :HALF_ROPE], rope[..., HALF_ROPE:]
    return jnp.concatenate([nope[..., :HALF_NOPE], x1, x2, nope[..., HALF_NOPE:], x2, x1], axis=-1)


def _projection(x2, positions, attn_norm, w_in, gate_w2, gate_b, q_norm, w_uq, kv_norm, w_ukv,
                qk_norm_q, qk_norm_k):
    t = x2.shape[0]
    tm = TM_PROJ
    assert t % tm == 0

    gq, gk, gv, g_low, g_r, c_q, c_kv, k_r = jnp.split(w_in, list(_cumsum(IN_SPLITS))[:-1], axis=1)
    zeros = lambda n: jnp.zeros((D_MODEL, n), w_in.dtype)
    kr_sw = jnp.concatenate([k_r[:, HALF_ROPE:], k_r[:, :HALF_ROPE]], axis=1)
    slot = jnp.concatenate([g_low, zeros(HALF_NOPE - GLA_GATE_RANK), k_r, zeros(HALF_NOPE), kr_sw], axis=1)
    w_in_l = jnp.concatenate([gq, gk, gv, g_r, c_q, c_kv, slot], axis=1).astype(BF16)
    assert w_in_l.shape == (D_MODEL, D_IN_PAD)

    w2_l = jnp.zeros((LANES, GLA_QK), F32).at[:GLA_GATE_RANK].set(gate_w2).astype(BF16)

    w_uq_h = w_uq.reshape(MLA_Q_RANK, MLA_HEADS, MLA_QK)
    w_uq_l = _head_layout(w_uq_h[..., :MLA_NOPE], w_uq_h[..., MLA_NOPE:]).reshape(MLA_Q_RANK, HEAD_PAD)
    w_ukv_h = w_ukv.reshape(MLA_KV_RANK, MLA_HEADS, MLA_NOPE + MLA_V)
    w_uk_l = _head_layout(w_ukv_h[..., :MLA_NOPE],
                          jnp.zeros((MLA_KV_RANK, MLA_HEADS, MLA_ROPE), F32)).reshape(MLA_KV_RANK, HEAD_PAD)
    w_uv = w_ukv_h[..., MLA_NOPE:].reshape(MLA_KV_RANK, MLA_WIDTH)
    gq_l = _head_layout(qk_norm_q[:MLA_NOPE], qk_norm_q[MLA_NOPE:])
    gk_l = _head_layout(qk_norm_k[:MLA_NOPE], qk_norm_k[MLA_NOPE:])

    inv_freq = ROPE_THETA ** (-jnp.arange(0, MLA_ROPE, 2, dtype=F32) / MLA_ROPE)
    z32, o32 = jnp.zeros((HALF_NOPE,), F32), jnp.ones((MLA_ROPE,), F32)
    tab = jnp.zeros((SUBLANES, LANES), F32)
    tab = tab.at[ROW_FREQ16].set(jnp.tile(inv_freq, LANES // HALF_ROPE))
    tab = tab.at[ROW_MASK96].set(jnp.concatenate([jnp.ones((MLA_QK,), F32), z32]))
    tab = tab.at[ROW_KRMASK].set(jnp.concatenate([z32, o32, z32, o32]))
    tab = tab.at[ROW_GQ].set(gq_l).at[ROW_GQ_SW].set(jnp.roll(gq_l, LANES // 2))
    tab = tab.at[ROW_GK].set(gk_l).at[ROW_GK_SW].set(jnp.roll(gk_l, LANES // 2))

    groups = ROPE_ROWS // SUBLANES
    pos_packed = jnp.broadcast_to(
        positions.reshape(t // ROPE_ROWS, groups, SUBLANES).transpose(0, 2, 1)[..., None],
        (t // ROPE_ROWS, SUBLANES, groups, HALF_ROPE)).reshape(t // ROPE_ROWS, SUBLANES, LANES)

    n_blocks = D_IN_PAD // W_IN_BLOCK
    w_in_l = w_in_l.reshape(D_MODEL, n_blocks, W_IN_BLOCK).transpose(1, 0, 2)

    n = t // tm
    tile = lambda w: pl.BlockSpec((tm, w), lambda i: (jnp.minimum(i, n - 1), 0))
    prev = lambda w: pl.BlockSpec((tm, w), lambda i: (jnp.maximum(i - 1, 0), 0))
    out_widths = (GLA_QK, GLA_QK, GLA_WIDTH, GLA_QK, GLA_WIDTH, HEAD_PAD, HEAD_PAD, MLA_WIDTH, MLA_WIDTH)
    out_dtypes = (BF16, BF16, BF16, F32, BF16, BF16, BF16, BF16, BF16)
    out_specs = [tile(w) for w in out_widths]
    out_specs[5], out_specs[6] = prev(HEAD_PAD), prev(HEAD_PAD)
    return pl.pallas_call(
        _proj_kernel,
        grid=(n + 1,),
        in_specs=[
            tile(D_MODEL),
            pl.BlockSpec((tm // ROPE_ROWS, SUBLANES, LANES), lambda i: (jnp.maximum(i - 1, 0), 0, 0)),
            _const_spec((1, D_MODEL)), _const_spec((n_blocks, D_MODEL, W_IN_BLOCK)),
            _const_spec((LANES, GLA_QK)), _const_spec((1, GLA_QK)),
            _const_spec((1, MLA_Q_RANK)), _const_spec((MLA_Q_RANK, HEAD_PAD)),
            _const_spec((1, MLA_KV_RANK)), _const_spec((MLA_KV_RANK, HEAD_PAD)),
            _const_spec((MLA_KV_RANK, MLA_WIDTH)),
            _const_spec((8, LANES)),
        ],
        out_specs=out_specs,
        out_shape=[jax.ShapeDtypeStruct((t, w), d) for w, d in zip(out_widths, out_dtypes)],
        scratch_shapes=[pltpu.VMEM((tm, HEAD_PAD), F32), pltpu.VMEM((tm, HEAD_PAD), F32),
                        pltpu.VMEM((tm, LANES), F32)],
        compiler_params=pltpu.CompilerParams(
            dimension_semantics=("arbitrary",), vmem_limit_bytes=VMEM_LIMIT_BYTES),
        name="proj",
    )(x2, pos_packed, attn_norm.reshape(1, D_MODEL), w_in_l, w2_l, gate_b.reshape(1, GLA_QK),
      q_norm.reshape(1, MLA_Q_RANK), w_uq_l.astype(BF16), kv_norm.reshape(1, MLA_KV_RANK),
      w_uk_l.astype(BF16), w_uv.astype(BF16), tab)


def _cumsum(sizes):
    total = 0
    for s in sizes:
        total += s
        yield total


def _gla_kernel(q_ref, k_ref, v_ref, la_ref, o_ref, st_ref, kf_ref, b_ref):
    nseq, rows_per_step = q_ref.shape[0], q_ref.shape[1]

    @pl.when(pl.program_id(1) == 0)
    def _():
        st_ref[...] = jnp.zeros_like(st_ref)

    row = lax.broadcasted_iota(jnp.int32, (CHUNK, CHUNK), 0)
    col = lax.broadcasted_iota(jnp.int32, (CHUNK, CHUNK), 1)
    causal = col <= row
    lane = lax.broadcasted_iota(jnp.int32, (1, LANES), 1)
    lane_q = lax.broadcasted_iota(jnp.int32, (1, GLA_QK), 1)
    head_mask = [(lane // GLA_DK) == e for e in range(2)]
    nt = (((1,), (1,)), ((), ()))
    tn = (((0,), (0,)), ((), ()))

    def exact_scores(q, k, b):
        kf_ref[...] = k
        b_ref[...] = b

        def key_row(s, sc):
            k_s = kf_ref[pl.ds(s, 1), :]
            b_s = b_ref[pl.ds(s, 1), :]
            prod = q * k_s * jnp.exp2(jnp.minimum(b - b_s, 0.0))
            out = []
            for h in range(GLA_HEADS):
                hm = (lane_q // GLA_DK) == h
                col_h = jnp.sum(jnp.where(hm, prod, 0.0), axis=-1, keepdims=True)
                out.append(jnp.where(col == s, col_h, sc[h]))
            return tuple(out)

        zero = jnp.zeros((CHUNK, CHUNK), F32)
        return lax.fori_loop(0, CHUNK, key_row, (zero,) * GLA_HEADS)

    pair = lambda h: slice((h // 2) * LANES, (h // 2 + 1) * LANES)
    vsl = lambda h: slice(h * GLA_DV, (h + 1) * GLA_DV)
    heads = range(GLA_HEADS)

    def row_cumsum(x):
        row_id = lax.broadcasted_iota(jnp.int32, x.shape, 0)
        step = 1
        while step < CHUNK:
            if step < SUBLANES:
                moved = jnp.where(row_id >= step, pltpu.roll(x, step, 0), 0.0)
            else:
                moved = jnp.concatenate([jnp.zeros((step, x.shape[1]), x.dtype), x[:-step]], axis=0)
            x = x + moved
            step *= 2
        return x

    def chunk(seqs, c, factorised):
        rows = pl.ds(pl.multiple_of(c * CHUNK, CHUNK), CHUNK)
        q, k, b, a_h, kt, kd, eb_last, scores, o_inter = {}, {}, {}, {}, {}, {}, {}, {}, {}
        for n in seqs:
            b[n] = row_cumsum(la_ref[n, rows, :])
            b_last = b[n][CHUNK - 1:CHUNK, :]
            q[n] = q_ref[n, rows, :].astype(F32)
            k[n] = k_ref[n, rows, :].astype(F32)
            qt = q[n] * jnp.exp2(b[n])
            eb_last[n] = jnp.exp2(b_last)
            if factorised:
                k_grown = k[n] * jnp.exp2(-b[n])
                kt[n] = k_grown.astype(BF16)
                kd[n] = k_grown * eb_last[n]
            else:
                kd[n] = k[n] * jnp.exp2(b_last - b[n])
            a_h[n] = [jnp.where(head_mask[h % 2], qt[:, pair(h)], 0.0).astype(BF16) for h in heads]
        v, st, o = {}, {}, {}
        for n in seqs:
            for h in heads:
                st[n, h] = st_ref[n, h]
                if factorised:
                    both = lax.dot_general(
                        a_h[n][h], jnp.concatenate([st[n, h].astype(BF16), kt[n][:, pair(h)]], axis=0), nt,
                        preferred_element_type=F32)
                    o_inter[n, h], scores[n, h] = both[:, :GLA_DV], both[:, GLA_DV:]
                else:
                    o_inter[n, h] = lax.dot_general(a_h[n][h], st[n, h].astype(BF16), nt,
                                                    preferred_element_type=F32)
            if not factorised:
                for h, sc in enumerate(exact_scores(q[n], k[n], b[n])):
                    scores[n, h] = sc
        for n in seqs:
            for h in heads:
                sc = jnp.where(causal, scores[n, h], 0.0).astype(BF16)
                v[n, h] = v_ref[n, rows, vsl(h)]
                o[n, h] = o_inter[n, h] + jnp.dot(sc, v[n, h], preferred_element_type=F32)
        for n in seqs:
            for h in heads:
                kd_h = jnp.where(head_mask[h % 2], kd[n][:, pair(h)], 0.0).astype(BF16)
                upd = lax.dot_general(v[n, h], kd_h, tn, preferred_element_type=F32)
                st_ref[n, h] = st[n, h] * eb_last[n][:, pair(h)] + upd
        for n in seqs:
            for h in heads:
                o_ref[n, rows, vsl(h)] = o[n, h].astype(BF16)

    def factorised_chunks():
        def body(c, carry):
            chunk(range(nseq), c, True)
            return carry
        lax.fori_loop(0, rows_per_step // CHUNK, body, 0)

    def exact_chunks():
        def body(c, carry):
            for n in range(nseq):
                chunk([n], c, False)
            return carry
        lax.fori_loop(0, rows_per_step // CHUNK, body, 0)

    factorisable = jnp.min(la_ref[...]) >= GLA_FAST_MIN_LOG_DECAY * LOG2E / CHUNK
    lax.cond(factorisable, factorised_chunks, exact_chunks)


def _gla(qg, kg, vg, la):
    b, s, _ = qg.shape
    nseq = max(n for n in GLA_SEQS_PER_STEP if b % n == 0)
    rows = min(GLA_ROWS_PER_STEP, s)
    assert s % rows == 0 and rows % CHUNK == 0
    blk = lambda w: pl.BlockSpec((nseq, rows, w), lambda i, j: (i, j, 0))
    return pl.pallas_call(
        _gla_kernel,
        grid=(b // nseq, s // rows),
        in_specs=[blk(GLA_QK), blk(GLA_QK), blk(GLA_WIDTH), blk(GLA_QK)],
        out_specs=blk(GLA_WIDTH),
        out_shape=jax.ShapeDtypeStruct((b, s, GLA_WIDTH), BF16),
        scratch_shapes=[pltpu.VMEM((nseq, GLA_HEADS, GLA_DV, LANES), F32),
                        pltpu.VMEM((CHUNK, GLA_QK), F32),
                        pltpu.VMEM((CHUNK, GLA_QK), F32)],
        compiler_params=pltpu.CompilerParams(
            dimension_semantics=("arbitrary", "arbitrary"), vmem_limit_bytes=VMEM_LIMIT_BYTES),
        name="gla",
    )(qg, kg, vg, la)


def _mla_tiles(n_q):
    assert TQ == 2 * TK
    first = [(a, 2 * a, 0) for a in range(n_q)]
    full = [(a, j, TQ // CHUNK) for a in range(n_q) for j in range(2 * a)]
    second = [(a, 2 * a + 1, -(TK // CHUNK)) for a in range(n_q)]
    return tuple(zip(*(first + full + second)))


def _mla_kernel(ti_ref, tj_ref, toff_ref, bound_ref, q_ref, k_ref, v_even_ref, v_odd_ref, o_ref,
                s_ref, m_ref, acc_ref):
    seq = q_ref.shape[1]
    n_q = seq // TQ
    n_tiles = s_ref.shape[1]
    nt = (((1,), (1,)), ((), ()))
    heads = range(2)
    lane = lax.broadcasted_iota(jnp.int32, (1, LANES), 1)
    own = [(lane // MLA_V) == e for e in heads]
    unroll = lambda count, limit=MLA_UNROLL: max(u for u in range(1, limit + 1) if count % u == 0)

    def q_rows(i):
        return pl.ds(pl.multiple_of(i * TQ, TQ), TQ)

    def k_rows(j):
        return pl.ds(pl.multiple_of(j * TK, TK), TK)

    def masked_logits(e, t):
        hsl = slice(e * LANES, (e + 1) * LANES)
        s = lax.dot_general(q_ref[0, q_rows(ti_ref[t]), hsl], k_ref[0, k_rows(tj_ref[t]), hsl], nt,
                            preferred_element_type=F32)
        row_chunk = lax.broadcasted_iota(jnp.int32, (TQ, TK), 0) // CHUNK
        col_chunk = lax.broadcasted_iota(jnp.int32, (TQ, TK), 1) // CHUNK
        return jnp.where(col_chunk <= row_chunk + toff_ref[t], s, MASKED_LOGIT)

    def lane_fold_max(a):
        out = a[:, :LANES]
        for c in range(1, TK // LANES):
            out = jnp.maximum(out, a[:, c * LANES:(c + 1) * LANES])
        return out

    values = (v_even_ref, v_odd_ref)

    def exp_shifted(s, shift):
        return jnp.concatenate(
            [jnp.exp2(s[:, c * LANES:(c + 1) * LANES] - shift) for c in range(TK // LANES)],
            axis=1).astype(BF16)

    def accumulate(e, t, s, shift):
        acc_ref[e, ti_ref[t]] += jnp.dot(exp_shifted(s, shift), values[e][0, k_rows(tj_ref[t]), :],
                                         preferred_element_type=F32)

    def clear(i, carry):
        for e in heads:
            acc_ref[e, i] = jnp.zeros((TQ, LANES), F32)
        return carry

    def finish(i, carry):
        acc0, acc1 = acc_ref[0, i], acc_ref[1, i]
        num = jnp.where(own[0], acc0, acc1)
        den = pltpu.roll(jnp.where(own[0], acc1, acc0), LANES // 2, 1)
        o_ref[0, q_rows(i), :] = (num / den).astype(BF16)
        return carry

    def single_pass():
        lax.fori_loop(0, n_q, clear, 0)
        shift = bound_ref[0].astype(F32)

        def tile(t, carry):
            for e in heads:
                accumulate(e, t, masked_logits(e, t), shift)
            return carry

        lax.fori_loop(0, n_tiles - n_q, tile, 0, unroll=unroll(n_tiles - n_q))

        def lower_half_tile(t, carry):
            i = ti_ref[t]
            lower = pl.ds(pl.multiple_of(i * TQ + TK, TK), TK)
            row_chunk = lax.broadcasted_iota(jnp.int32, (TK, TK), 0) // CHUNK
            col_chunk = lax.broadcasted_iota(jnp.int32, (TK, TK), 1) // CHUNK
            for e in heads:
                hsl = slice(e * LANES, (e + 1) * LANES)
                s = lax.dot_general(q_ref[0, lower, hsl], k_ref[0, k_rows(tj_ref[t]), hsl], nt,
                                    preferred_element_type=F32)
                p = exp_shifted(jnp.where(col_chunk <= row_chunk, s, MASKED_LOGIT), shift)
                acc_ref[e, i, TK:, :] += jnp.dot(p, values[e][0, k_rows(tj_ref[t]), :],
                                                 preferred_element_type=F32)
            return finish(i, carry)

        lax.fori_loop(n_tiles - n_q, n_tiles, lower_half_tile, 0, unroll=unroll(n_q))

    def two_pass():
        def first_tile(t, carry):
            for e in heads:
                s = masked_logits(e, t)
                s_ref[e, t] = s
                m_ref[e, t] = lane_fold_max(s)
            return clear(t, carry)

        lax.fori_loop(0, n_q, first_tile, 0, unroll=unroll(n_q, MLA_UNROLL_TWO_PASS))

        def pass1(t, carry):
            i = ti_ref[t]
            for e in heads:
                s = masked_logits(e, t)
                s_ref[e, t] = s
                m_ref[e, i] = jnp.maximum(m_ref[e, i], lane_fold_max(s))
            return carry

        lax.fori_loop(n_q, n_tiles, pass1, 0, unroll=unroll(n_tiles - n_q, MLA_UNROLL_TWO_PASS))

        def row_max(i, carry):
            for e in heads:
                m_ref[e, i] = jnp.broadcast_to(jnp.max(m_ref[e, i], axis=-1, keepdims=True), (TQ, LANES))
            return carry

        lax.fori_loop(0, n_q, row_max, 0)

        def pass2(t, carry):
            for e in heads:
                accumulate(e, t, s_ref[e, t], m_ref[e, ti_ref[t]])
            return carry

        lax.fori_loop(0, n_tiles, pass2, 0, unroll=unroll(n_tiles, MLA_UNROLL_TWO_PASS))
        lax.fori_loop(0, n_q, finish, 0)

    lax.cond(bound_ref[0] <= MLA_MAX_SHIFT, single_pass, two_pass)


def _mla(q, k, v_even, v_odd, gain_q, gain_k):
    b, s, _ = q.shape
    assert s % TQ == 0 and TQ % TK == 0
    n_q = s // TQ
    ti, tj, toff = _mla_tiles(n_q)
    bound = jnp.ceil(1.02 * LOG2E * math.sqrt(MLA_QK) * jnp.max(jnp.abs(gain_q)) * jnp.max(jnp.abs(gain_k)))
    bound = jnp.minimum(bound, 2.0 * MLA_MAX_SHIFT).astype(jnp.int32).reshape(1)
    blk = lambda w: pl.BlockSpec((1, s, w), lambda i, p, *_: (i, 0, p))
    return pl.pallas_call(
        _mla_kernel,
        grid_spec=pltpu.PrefetchScalarGridSpec(
            num_scalar_prefetch=4,
            grid=(b, MLA_HEADS // 2),
            in_specs=[blk(2 * LANES), blk(2 * LANES), blk(LANES), blk(LANES)],
            out_specs=blk(LANES),
            scratch_shapes=[pltpu.VMEM((2, len(ti), TQ, TK), F32),
                            pltpu.VMEM((2, n_q, TQ, LANES), F32),
                            pltpu.VMEM((2, n_q, TQ, LANES), F32)]),
        out_shape=jax.ShapeDtypeStruct((b, s, MLA_WIDTH), BF16),
        compiler_params=pltpu.CompilerParams(
            dimension_semantics=("arbitrary", "arbitrary"),
            vmem_limit_bytes=VMEM_LIMIT_BYTES),
        name="mla",
    )(jnp.asarray(ti, jnp.int32), jnp.asarray(tj, jnp.int32), jnp.asarray(toff, jnp.int32), bound,
      q, k, v_even, v_odd)


def _tail_kernel(x_ref, og_ref, gr_ref, om_ref, p_ref, gon_ref, w_out_ref, g_mlp_ref, w_up_ref,
                 w_down_ref, g_ple_ref, w_gate_ref, b_gate_ref, w_pp_ref, o_ref):
    gated = []
    for hd in range(GLA_HEADS):
        sl = slice(hd * GLA_DV, (hd + 1) * GLA_DV)
        o_h = og_ref[:, sl].astype(F32)
        gated.append((o_h * _rms_scale(o_h, GLA_DV) * gon_ref[...] * gr_ref[:, sl].astype(F32)).astype(BF16))
    h = (x_ref[...]
         + jnp.dot(om_ref[...], w_out_ref[GLA_WIDTH:, :], preferred_element_type=F32)
         + jnp.dot(jnp.concatenate(gated, axis=1), w_out_ref[:GLA_WIDTH, :], preferred_element_type=F32))
    m = (h * _rms_scale(h, D_MODEL) * g_mlp_ref[...]).astype(BF16)
    mlp = None
    for c in range(D_FF // FF_CHUNK):
        cols = slice(c * FF_CHUNK, (c + 1) * FF_CHUNK)
        u = jnp.maximum(jnp.dot(m, w_up_ref[:, cols], preferred_element_type=F32), 0.0)
        d = jnp.dot((u * u).astype(BF16), w_down_ref[cols, :], preferred_element_type=F32)
        mlp = d if mlp is None else mlp + d
    h = h + mlp
    g = (h * _rms_scale(h, D_MODEL) * g_ple_ref[...]).astype(BF16)
    gate = jax.nn.sigmoid(jnp.dot(g, w_gate_ref[...], preferred_element_type=F32) + b_gate_ref[...])
    pp = jnp.dot(p_ref[...].astype(BF16), w_pp_ref[...], preferred_element_type=F32)
    o_ref[...] = h + pp * gate


def _tail(x2, og, gr, om, p2, out_norm, w_out, mlp_norm, w_up, w_down, ple_norm, w_gate, b_gate, w_pp):
    t = x2.shape[0]
    tm = TM_TAIL
    assert t % tm == 0
    row = lambda w: pl.BlockSpec((tm, w), lambda i: (i, 0))
    return pl.pallas_call(
        _tail_kernel,
        grid=(t // tm,),
        in_specs=[row(D_MODEL), row(GLA_WIDTH), row(GLA_WIDTH), row(MLA_WIDTH), row(PLE_DIM),
                  _const_spec((1, GLA_DV)), _const_spec((D_MODEL, D_MODEL)), _const_spec((1, D_MODEL)),
                  _const_spec((D_MODEL, D_FF)), _const_spec((D_FF, D_MODEL)),
                  _const_spec((1, D_MODEL)), _const_spec((D_MODEL, D_MODEL)),
                  _const_spec((1, D_MODEL)), _const_spec((PLE_DIM, D_MODEL))],
        out_specs=row(D_MODEL),
        out_shape=jax.ShapeDtypeStruct((t, D_MODEL), F32),
        compiler_params=pltpu.CompilerParams(
            dimension_semantics=("arbitrary",), vmem_limit_bytes=VMEM_LIMIT_BYTES),
        name="tail",
    )(x2, og, gr, om, p2, out_norm.reshape(1, GLA_DV), w_out.astype(BF16), mlp_norm.reshape(1, D_MODEL), w_up.astype(BF16),
      w_down.astype(BF16), ple_norm.reshape(1, D_MODEL), w_gate.astype(BF16),
      b_gate.reshape(1, D_MODEL), w_pp.astype(BF16))


def kernel(x, p, positions, attn_norm, w_in, gla_gate_w2, gla_gate_b, gla_out_norm, mla_q_norm,
           mla_w_uq, mla_kv_norm, mla_w_ukv, qk_norm_q, qk_norm_k, w_out, mlp_norm, w_mlp_up,
           w_mlp_down, ple_norm, w_ple_gate, b_ple_gate, w_ple_proj):
    b, s, d = x.shape
    depth = w_in.shape[0]
    t = b * s
    h = x.reshape(t, d)
    for i in range(depth):
        qg, kg, vg, la, gr, q, k, v_even, v_odd = _projection(
            h, positions, attn_norm[i], w_in[i], gla_gate_w2[i], gla_gate_b[i], mla_q_norm[i],
            mla_w_uq[i], mla_kv_norm[i], mla_w_ukv[i], qk_norm_q[i], qk_norm_k[i])
        seq = lambda a: a.reshape(b, s, a.shape[-1])
        og = _gla(seq(qg), seq(kg), seq(vg), seq(la))
        om = _mla(seq(q), seq(k), seq(v_even), seq(v_odd), qk_norm_q[i], qk_norm_k[i])
        h = _tail(h, og.reshape(t, GLA_WIDTH), gr, om.reshape(t, MLA_WIDTH), p[i].reshape(t, PLE_DIM),
                  gla_out_norm[i], w_out[i], mlp_norm[i], w_mlp_up[i], w_mlp_down[i], ple_norm[i], w_ple_gate[i],
                  b_ple_gate[i], w_ple_proj[i])
    return h.reshape(b, s, d)
```
